```python
import jax, jax.numpy as jnp
from jax import lax
import numpy as np

D_MODEL = 1024
BATCH = 8
SEQ = 4096
DEPTH = 4

HEAD_DIM = 64
RWKV_WIDTH = D_MODEL // 2
RWKV_HEADS = RWKV_WIDTH // HEAD_DIM
RWKV_DECAY_LORA = 64
RWKV_ICLR_LORA = 64
RWKV_GATE_LORA = 128
RWKV_GN_EPS = 64e-5
SB_WIDTH = D_MODEL // 4
SB_HEADS = SB_WIDTH // HEAD_DIM
SB_BLOCK = 128
GLA_VAL_WIDTH = D_MODEL // 4
GLA_HEADS = GLA_VAL_WIDTH // HEAD_DIM
GLA_KEY_WIDTH = GLA_VAL_WIDTH // 2
GLA_KEY_DIM = GLA_KEY_WIDTH // GLA_HEADS
GLA_VAL_DIM = GLA_VAL_WIDTH // GLA_HEADS
GLA_GATE_LORA = 16
GLA_GATE_NORMALIZER = 16.0
GLA_CHUNK = 64
HEAD_NORM_EPS = 1e-5
RMS_EPS = 1e-6

RWKV_COLS = 3 * RWKV_WIDTH + RWKV_DECAY_LORA + RWKV_ICLR_LORA + RWKV_GATE_LORA
SB_COLS = 3 * SB_WIDTH
GLA_COLS = 2 * GLA_KEY_WIDTH + 2 * GLA_VAL_WIDTH + GLA_GATE_LORA
IN_COLS = RWKV_COLS + SB_COLS + GLA_COLS

N_EXPERTS = 16
N_EXPERT_GROUPS = 4
EXPERTS_PER_GROUP = N_EXPERTS // N_EXPERT_GROUPS
TOP_K = 2
GROUP_SCORE_K = 2
D_EXPERT = D_MODEL // 2
MOE_BLOCK = 256

kernel_name = "hybrid_rwkv7_stickbreak_gla_grouped_moe_adaln"

F32 = jnp.float32


def rms_norm(x, g):
    xf = x.astype(F32)
    y = xf * lax.rsqrt(jnp.mean(xf * xf, axis=-1, keepdims=True) + RMS_EPS) * g.astype(F32)
    return y.astype(x.dtype)


def _head_rms(t):
    tf = t.astype(F32)
    return tf * lax.rsqrt(jnp.mean(tf * tf, axis=-1, keepdims=True) + HEAD_NORM_EPS)


def _split_heads(t, n_heads):
    b, s, w = t.shape
    return t.reshape(b, s, n_heads, w // n_heads).transpose(0, 2, 1, 3)


def _wkv7_scan(r, w, k, v, kk, iclr):
    bsz, _, h, n = r.shape

    def step(state, inp):
        r_t, w_t, k_t, v_t, kk_t, a_t = inp
        sa = jnp.einsum('bhvk,bhk->bhv', state, kk_t)
        state = (state * w_t[:, :, None, :]
                 - sa[..., None] * (kk_t * a_t)[:, :, None, :]
                 + v_t[..., None] * k_t[:, :, None, :])
        return state, jnp.einsum('bhvk,bhk->bhv', state, r_t)

    xs = tuple(jnp.moveaxis(t, 1, 0) for t in (r, w, k, v, kk, iclr))
    state0 = jnp.zeros((bsz, h, n, n), F32)
    _, ys = lax.scan(step, state0, xs)
    return jnp.moveaxis(ys, 0, 1)


def rwkv7_group(p, mu, w0, w2, a0, a2, g2, k_k, k_a, r_k, lnx_w, lnx_b):
    bsz, s, _ = p.shape
    pf = p.astype(F32)
    prev = jnp.pad(pf, ((0, 0), (1, 0), (0, 0)))[:, :s]
    xm = pf + (prev - pf) * mu
    wd = RWKV_WIDTH
    o1 = 3 * wd
    o2 = o1 + RWKV_DECAY_LORA
    o3 = o2 + RWKV_ICLR_LORA
    r, k, v = xm[..., :wd], xm[..., wd:2 * wd], xm[..., 2 * wd:o1]
    xw, xa, xg = xm[..., o1:o2], xm[..., o2:o3], xm[..., o3:]
    w_log = -jax.nn.softplus(-(w0 + jnp.tanh(xw) @ w2)) - 0.5
    decay = jnp.exp(-jnp.exp(w_log))
    iclr = jax.nn.sigmoid(a0 + xa @ a2)
    g = jax.nn.sigmoid(xg) @ g2

    def hs(t):
        return t.reshape(bsz, s, RWKV_HEADS, HEAD_DIM)

    kk = hs(k * k_k)
    kk = kk / jnp.maximum(jnp.linalg.norm(kk, axis=-1, keepdims=True), 1e-12)
    k = k * (1.0 + (iclr - 1.0) * k_a)
    rh, kh, vh = hs(r), hs(k), hs(v)
    y = _wkv7_scan(rh, hs(decay), kh, vh, kk, hs(iclr))
    mean = jnp.mean(y, axis=-1, keepdims=True)
    var = jnp.mean(jnp.square(y - mean), axis=-1, keepdims=True)
    y = ((y - mean) * lax.rsqrt(var + RWKV_GN_EPS)).reshape(bsz, s, wd) * lnx_w + lnx_b
    bonus = jnp.sum(rh * kh * r_k, axis=-1, keepdims=True) * vh
    return ((y + bonus.reshape(bsz, s, wd)) * g).astype(p.dtype)


def _stick_breaking(q, k, v):
    _, _, s, dh = q.shape
    scale = dh ** -0.5
    outs = []
    for i in range(s // SB_BLOCK):
        q0 = i * SB_BLOCK
        kv_len = q0 + SB_BLOCK
        qb = q[:, :, q0:kv_len].astype(F32)
        kb = k[:, :, :kv_len].astype(F32)
        vb = v[:, :, :kv_len].astype(F32)
        z = jnp.einsum('bhqd,bhkd->bhqk', qb, kb) * scale
        t_pos = q0 + jnp.arange(SB_BLOCK)
        s_pos = jnp.arange(kv_len)
        causal = s_pos[None, :] < t_pos[:, None]
        log_keep = jnp.where(causal, -jax.nn.softplus(z), 0.0)
        later = lax.cumsum(log_keep, axis=3, reverse=True) - log_keep
        weights = jnp.where(causal, jnp.exp(jax.nn.log_sigmoid(z) + later), 0.0)
        outs.append(jnp.einsum('bhqk,bhkd->bhqd', weights, vb))
    return jnp.concatenate(outs, axis=2)


def sb_group(p, norm_g):
    bsz, s, _ = p.shape
    q = _split_heads(p[..., :SB_WIDTH], SB_HEADS)
    k = _split_heads(p[..., SB_WIDTH:2 * SB_WIDTH], SB_HEADS)
    v = _split_heads(p[..., 2 * SB_WIDTH:], SB_HEADS)
    o = _stick_breaking(q, k, v).transpose(0, 2, 1, 3)
    o = _head_rms(o).reshape(bsz, s, SB_WIDTH) * norm_g
    return o.astype(p.dtype)


def _gla_chunked(q, k, v, log_a):
    bsz, h, s, dk = q.shape
    dv = v.shape[-1]
    n = s // GLA_CHUNK

    def to_chunks(t):
        return jnp.moveaxis(t.astype(F32).reshape(bsz, h, n, GLA_CHUNK, t.shape[-1]), 2, 0)

    tri = jnp.tril(jnp.ones((GLA_CHUNK, GLA_CHUNK), bool))

    def step(state, inp):
        qc, kc, vc, gc = inp
        b = jnp.cumsum(gc, axis=2)
        diff = b[:, :, :, None, :] - b[:, :, None, :, :]
        decay = jnp.exp(jnp.where(tri[:, :, None], diff, -jnp.inf))
        attn = jnp.einsum('bhtd,bhsd,bhtsd->bhts', qc, kc, decay)
        o = jnp.einsum('bhts,bhsv->bhtv', attn, vc) + jnp.einsum('bhtd,bhdv->bhtv', qc * jnp.exp(b), state)
        b_last = b[:, :, -1:, :]
        state = (state * jnp.exp(b_last)[:, :, 0, :, None]
                 + jnp.einsum('bhsd,bhsv->bhdv', kc * jnp.exp(b_last - b), vc))
        return state, o

    state0 = jnp.zeros((bsz, h, dk, dv), F32)
    _, os_ = lax.scan(step, state0, (to_chunks(q), to_chunks(k), to_chunks(v), to_chunks(log_a)))
    return jnp.moveaxis(os_, 0, 2).reshape(bsz, h, s, dv)


def gla_group(p, gk_up, gk_b, norm_g):
    bsz, s, _ = p.shape
    pf = p.astype(F32)
    kw, vw = GLA_KEY_WIDTH, GLA_VAL_WIDTH
    o1 = 2 * kw + vw
    o2 = o1 + vw
    q = pf[..., :kw] * GLA_KEY_DIM ** -0.5
    k = pf[..., kw:2 * kw]
    v = pf[..., 2 * kw:o1]
    g = pf[..., o1:o2]
    gk_low = pf[..., o2:]
    log_a = jax.nn.log_sigmoid(gk_low @ gk_up + gk_b) / GLA_GATE_NORMALIZER
    o = _gla_chunked(_split_heads(q, GLA_HEADS), _split_heads(k, GLA_HEADS),
                     _split_heads(v, GLA_HEADS), _split_heads(log_a, GLA_HEADS))
    o = _head_rms(o.transpose(0, 2, 1, 3)) * norm_g
    o = o * jax.nn.silu(g.reshape(bsz, s, GLA_HEADS, GLA_VAL_DIM))
    return o.reshape(bsz, s, vw).astype(p.dtype)


def grouped_moe(h, w_router, router_bias, w_gate, w_up, w_down):
    bsz, s, d = h.shape
    n = bsz * s
    hf = h.reshape(n, d)
    aff = jax.nn.sigmoid((hf @ w_router).astype(F32))
    sel = aff + router_bias.astype(F32)
    sel_g = sel.reshape(n, N_EXPERT_GROUPS, EXPERTS_PER_GROUP)
    grp_score = lax.top_k(sel_g, GROUP_SCORE_K)[0].sum(-1)
    grp = jnp.argmax(grp_score, axis=-1)
    sel_in = jnp.take_along_axis(sel_g, grp[:, None, None], axis=1)[:, 0]
    _, loc = lax.top_k(sel_in, TOP_K)
    eidx = grp[:, None] * EXPERTS_PER_GROUP + loc
    wts = jnp.take_along_axis(aff, eidx, axis=1)
    wts = wts / jnp.sum(wts, axis=-1, keepdims=True)

    m = n * TOP_K
    e_flat = eidx.reshape(m)
    tok_flat = jnp.repeat(jnp.arange(n, dtype=jnp.int32), TOP_K)
    w_flat = wts.reshape(m)
    order = jnp.argsort(e_flat)
    e_sorted = e_flat[order]
    counts = jnp.bincount(e_flat, length=N_EXPERTS)
    start = jnp.cumsum(counts) - counts
    padded = (counts + MOE_BLOCK - 1) // MOE_BLOCK * MOE_BLOCK
    pad_end = jnp.cumsum(padded)
    pad_start = pad_end - padded
    pos = pad_start[e_sorted] + (jnp.arange(m) - start[e_sorted])
    n_blocks = -(-m // MOE_BLOCK) + N_EXPERTS
    m_pad = n_blocks * MOE_BLOCK
    tok_pad = jnp.zeros((m_pad,), jnp.int32).at[pos].set(tok_flat[order])
    w_pad = jnp.zeros((m_pad,), hf.dtype).at[pos].set(w_flat[order].astype(hf.dtype))
    blk_exp = jnp.minimum(jnp.searchsorted(pad_end, jnp.arange(n_blocks) * MOE_BLOCK, side='right'),
                          N_EXPERTS - 1)
    xb = hf[tok_pad].reshape(n_blocks, MOE_BLOCK, d)

    def expert_block(args):
        xblk, e = args
        return (jax.nn.silu(xblk @ w_gate[e]) * (xblk @ w_up[e])) @ w_down[e]

    yb = lax.map(expert_block, (xb, blk_exp)).reshape(m_pad, d) * w_pad[:, None]
    out = jax.ops.segment_sum(yb, tok_pad, num_segments=n)
    return out.reshape(bsz, s, d).astype(h.dtype)


def setup_inputs(seed: int = 0) -> dict:
    key = jax.random.key(seed)
    ks = iter(jax.random.split(key, 40))
    D, L = D_MODEL, DEPTH

    def nrm(shape, scale):
        return scale * jax.random.normal(next(ks), shape, F32)

    return {
        "x": nrm((BATCH, SEQ, D), 1.0),
        "c": nrm((BATCH, D), 1.0),
        "rms_mix_g": 1.0 + nrm((L, D), 0.02),
        "rms_ffn_g": 1.0 + nrm((L, D), 0.02),
        "w_mod": nrm((L, D, 6 * D), 0.5 * D ** -0.5),
        "b_mod": nrm((L, 6 * D), 0.02),
        "w_in": nrm((L, D, IN_COLS), D ** -0.5),
        "w_out": nrm((L, D, D), D ** -0.5),
        "rwkv_mu": jax.random.uniform(next(ks), (L, RWKV_COLS), F32),
        "rwkv_w0": jnp.linspace(-6.0, -1.0, RWKV_WIDTH, dtype=F32)[None, :] + nrm((L, RWKV_WIDTH), 0.1),
        "rwkv_w2": nrm((L, RWKV_DECAY_LORA, RWKV_WIDTH), 0.1),
        "rwkv_a0": nrm((L, RWKV_WIDTH), 0.1),
        "rwkv_a2": nrm((L, RWKV_ICLR_LORA, RWKV_WIDTH), 0.1),
        "rwkv_g2": nrm((L, RWKV_GATE_LORA, RWKV_WIDTH), RWKV_GATE_LORA ** -0.5),
        "rwkv_k_k": 0.85 + nrm((L, RWKV_WIDTH), 0.02),
        "rwkv_k_a": 1.0 + nrm((L, RWKV_WIDTH), 0.02),
        "rwkv_r_k": nrm((L, RWKV_HEADS, HEAD_DIM), 0.1),
        "rwkv_lnx_w": 1.0 + nrm((L, RWKV_WIDTH), 0.02),
        "rwkv_lnx_b": nrm((L, RWKV_WIDTH), 0.02),
        "sb_norm_g": 1.0 + nrm((L, SB_WIDTH), 0.02),
        "gla_gk_up": nrm((L, GLA_GATE_LORA, GLA_KEY_WIDTH), GLA_GATE_LORA ** -0.5),
        "gla_gk_b": nrm((L, GLA_KEY_WIDTH), 0.1),
        "gla_norm_g": 1.0 + nrm((L, GLA_VAL_DIM), 0.02),
        "w_router": nrm((D, N_EXPERTS), D ** -0.5),
        "router_bias": nrm((N_EXPERTS,), 0.01),
        "w_gate": nrm((L, N_EXPERTS, D, D_EXPERT), D ** -0.5),
        "w_up": nrm((L, N_EXPERTS, D, D_EXPERT), D ** -0.5),
        "w_down": nrm((L, N_EXPERTS, D_EXPERT, D), D_EXPERT ** -0.5),
        "final_g": 1.0 + nrm((D,), 0.02),
    }


def reference(x, c, rms_mix_g, rms_ffn_g, w_mod, b_mod, w_in, w_out, rwkv_mu, rwkv_w0, rwkv_w2,
              rwkv_a0, rwkv_a2, rwkv_g2, rwkv_k_k, rwkv_k_a, rwkv_r_k, rwkv_lnx_w, rwkv_lnx_b,
              sb_norm_g, gla_gk_up, gla_gk_b, gla_norm_g, w_router, router_bias, w_gate, w_up,
              w_down, final_g):
    c_act = jax.nn.silu(c)
    for l in range(DEPTH):
        mod = c_act @ w_mod[l] + b_mod[l]
        sh_a, sc_a, gt_a, sh_f, sc_f, gt_f = [m[:, None, :] for m in jnp.split(mod, 6, axis=-1)]

        h = rms_norm(x, rms_mix_g[l]) * (1.0 + sc_a) + sh_a
        p = h @ w_in[l]
        y_r = rwkv7_group(p[..., :RWKV_COLS], rwkv_mu[l], rwkv_w0[l], rwkv_w2[l], rwkv_a0[l],
                          rwkv_a2[l], rwkv_g2[l], rwkv_k_k[l], rwkv_k_a[l], rwkv_r_k[l],
                          rwkv_lnx_w[l], rwkv_lnx_b[l])
        y_s = sb_group(p[..., RWKV_COLS:RWKV_COLS + SB_COLS], sb_norm_g[l])
        y_g = gla_group(p[..., RWKV_COLS + SB_COLS:], gla_gk_up[l], gla_gk_b[l], gla_norm_g[l])
        x = x + gt_a * (jnp.concatenate([y_r, y_s, y_g], axis=-1) @ w_out[l])

        h = rms_norm(x, rms_ffn_g[l]) * (1.0 + sc_f) + sh_f
        x = x + gt_f * grouped_moe(h, w_router, router_bias, w_gate[l], w_up[l], w_down[l])
    return rms_norm(x, final_g)
```

```python
import functools

import jax
import jax.numpy as jnp
from jax import lax
from jax.experimental import pallas as pl
from jax.experimental.pallas import tpu as pltpu

F32 = jnp.float32
BF16 = jnp.bfloat16

LANES = 128
HEAD_DIM = 64
RWKV_W = 512
RWKV_PAIRS = RWKV_W // LANES
RWKV_COLS_PAD = 3 * RWKV_W + 3 * LANES
RWKV_CHUNK = 64
RWKV_GN_EPS = 64e-5
SB_W = 256
SB_COLS = 3 * SB_W
SB_TQ = 128
SB_TK = 128
GLA_VW = 256
GLA_KW = 128
GLA_HEADS = 4
GLA_KD = 32
GLA_LORA = 16
GLA_COLS_PAD = 2 * GLA_KW + 2 * GLA_VW + LANES
GLA_CHUNK = 64
GLA_SUB = 16
GLA_NORMALIZER = 16.0
IN_COLS_PAD = RWKV_COLS_PAD + SB_COLS + GLA_COLS_PAD
HEAD_NORM_EPS = 1e-5
RMS_EPS = 1e-6
N_EXPERTS = 16
EXPERTS_PER_GROUP = 4
N_GROUPS = 4
MOE_TM = 1024
MOE_R = 128
VMEM_LIMIT = 48 * 1024 * 1024


def _dot(a, b):
    return jnp.dot(a, b, preferred_element_type=F32)


def _dot_nt(a, b):
    return lax.dot_general(a, b, (((1,), (1,)), ((), ())), preferred_element_type=F32)


def _split2(x):
    hi = x.astype(BF16)
    lo = (x - hi.astype(F32)).astype(BF16)
    return hi, lo


def _split3(x):
    hi = x.astype(BF16)
    r = x - hi.astype(F32)
    mid = r.astype(BF16)
    lo = (r - mid.astype(F32)).astype(BF16)
    return hi, mid, lo


def _dot_exact_rhs(x, m):
    hi, mid, lo = _split3(x)
    return _dot(hi, m) + _dot(mid, m) + _dot(lo, m)


def _dot_exact_lhs(m, x):
    hi, mid, lo = _split3(x)
    return _dot(m, hi) + _dot(m, mid) + _dot(m, lo)


def _dot_f32(a, b):
    ah, al = _split2(a)
    bh, bl = _split2(b)
    return _dot(ah, bh) + _dot(ah, bl) + _dot(al, bh)


def _dot_nt_f32(a, b):
    ah, al = _split2(a)
    bh, bl = _split2(b)
    return _dot_nt(ah, bh) + _dot_nt(ah, bl) + _dot_nt(al, bh)


def _softplus(x):
    return jnp.maximum(x, 0.0) + jnp.log(1.0 + jnp.exp(-jnp.abs(x)))


def _log_sigmoid(x):
    return jnp.minimum(x, 0.0) - jnp.log(1.0 + jnp.exp(-jnp.abs(x)))


def _sigmoid(x):
    return 1.0 / (1.0 + jnp.exp(-x))


def _iota(shape, dim):
    return lax.broadcasted_iota(jnp.int32, shape, dim)


def _block_diag_const(n, blk, val):
    i = jnp.arange(n)
    return jnp.where((i[:, None] // blk) == (i[None, :] // blk), val, 0.0).astype(BF16)


def _cparams(sem):
    return pltpu.CompilerParams(dimension_semantics=sem, vmem_limit_bytes=VMEM_LIMIT)


def _mod_kernel(c_ref, w_ref, b_ref, o_ref):
    c = c_ref[...]
    ca = c * _sigmoid(c)
    o_ref[...] = _dot_f32(ca, w_ref[...]) + b_ref[...]


def _modulation(c, w_mod, b_mod):
    depth, d, six_d = w_mod.shape
    bsz = c.shape[0]
    tn = 1536
    return pl.pallas_call(
        _mod_kernel,
        grid=(depth, six_d // tn),
        in_specs=[
            pl.BlockSpec((bsz, d), lambda l, j: (0, 0)),
            pl.BlockSpec((None, d, tn), lambda l, j: (l, 0, j)),
            pl.BlockSpec((None, 1, tn), lambda l, j: (l, 0, j)),
        ],
        out_specs=pl.BlockSpec((None, bsz, tn), lambda l, j: (l, 0, j)),
        out_shape=jax.ShapeDtypeStruct((depth, bsz, six_d), F32),
        compiler_params=_cparams(("arbitrary", "arbitrary")),
        name="adaln_mod",
    )(c, w_mod, b_mod.reshape(depth, 1, six_d))


def _rms_mod(x, g, scale, shift):
    ms = jnp.mean(x * x, axis=-1, keepdims=True)
    return x * lax.rsqrt(ms + RMS_EPS) * g * (1.0 + scale) + shift


def _inproj_kernel(x_ref, mod_ref, g_ref, w_ref, pr_ref, ps_ref, pg_ref):
    mod = mod_ref[...]
    h = _rms_mod(x_ref[...], g_ref[...], mod[1:2], mod[0:1]).astype(BF16)
    o1 = RWKV_COLS_PAD
    o2 = o1 + SB_COLS
    pr_ref[...] = _dot(h, w_ref[:, 0:o1])
    ps_ref[...] = _dot(h, w_ref[:, o1:o2]).astype(BF16)
    pg_ref[...] = _dot(h, w_ref[:, o2:IN_COLS_PAD])


def _inproj(x, mod_l, g, w):
    bsz, s, d = x.shape
    tm = min(512, s)
    return pl.pallas_call(
        _inproj_kernel,
        grid=(bsz, s // tm),
        in_specs=[
            pl.BlockSpec((None, tm, d), lambda b, i: (b, i, 0)),
            pl.BlockSpec((None, 6, d), lambda b, i: (b, 0, 0)),
            pl.BlockSpec((1, d), lambda b, i: (0, 0)),
            pl.BlockSpec((d, IN_COLS_PAD), lambda b, i: (0, 0)),
        ],
        out_specs=[
            pl.BlockSpec((None, tm, RWKV_COLS_PAD), lambda b, i: (b, i, 0)),
            pl.BlockSpec((None, tm, SB_COLS), lambda b, i: (b, i, 0)),
            pl.BlockSpec((None, tm, GLA_COLS_PAD), lambda b, i: (b, i, 0)),
        ],
        out_shape=[
            jax.ShapeDtypeStruct((bsz, s, RWKV_COLS_PAD), F32),
            jax.ShapeDtypeStruct((bsz, s, SB_COLS), BF16),
            jax.ShapeDtypeStruct((bsz, s, GLA_COLS_PAD), F32),
        ],
        compiler_params=_cparams(("arbitrary", "arbitrary")),
        name="inproj",
    )(x, mod_l, g, w)


def _rwkv_kernel(p_ref, mu_ref, w0_ref, w2_ref, a0_ref, a2_ref, g2_ref, kk_ref, ka_ref, rk_ref,
                 lnw_ref, lnb_ref, bd_ref, o_ref, ht_scr, prev_scr):
    t = RWKV_CHUNK

    @pl.when(pl.program_id(1) == 0)
    def _():
        ht_scr[...] = jnp.zeros_like(ht_scr)
        prev_scr[...] = jnp.zeros_like(prev_scr)

    p = p_ref[...]
    row = _iota((t, 1), 0)
    prev = jnp.where(row == 0, prev_scr[0:1, :], pltpu.roll(p, 1, axis=0))
    prev_scr[0:1, :] = p[t - 1:t, :]
    xm = p + (prev - p) * mu_ref[...]
    w = RWKV_W
    r = xm[:, 0:w]
    k = xm[:, w:2 * w]
    v = xm[:, 2 * w:3 * w]
    xw = xm[:, 3 * w:3 * w + LANES]
    xa = xm[:, 3 * w + LANES:3 * w + 2 * LANES]
    xg = xm[:, 3 * w + 2 * LANES:3 * w + 3 * LANES]
    w_log = -_softplus(-(w0_ref[...] + _dot_f32(jnp.tanh(xw), w2_ref[...]))) - 0.5
    lw = -jnp.exp(w_log)
    iclr = _sigmoid(a0_ref[...] + _dot_f32(xa, a2_ref[...]))
    g = _dot_f32(_sigmoid(xg), g2_ref[...])
    bd = bd_ref[...]
    kkr = k * kk_ref[...]
    ss = _dot_exact_rhs(kkr * kkr, bd)
    kk = kkr * lax.rsqrt(jnp.maximum(ss, 1e-24))
    k2 = k * (1.0 + (iclr - 1.0) * ka_ref[...])
    bonus = _dot_exact_rhs(r * k2 * rk_ref[...], bd) * v

    ti = _iota((t, t), 0)
    tj = _iota((t, t), 1)
    tri_incl = jnp.where(ti >= tj, 1.0, 0.0).astype(BF16)
    beta = _dot_exact_lhs(tri_incl, lw)
    gam = jnp.exp(beta)
    gam_inv = jnp.exp(-beta)
    a_t = -kk * jnp.exp(beta - lw)
    r_t = r * gam
    b_t = kk * iclr * gam_inv
    k_t = k2 * gam_inv
    gam_last = gam[t - 1:t, :]

    lane = _iota((1, LANES), 1)
    m0 = jnp.where(lane < HEAD_DIM, 1.0, 0.0)
    m1 = 1.0 - m0
    n2 = 2 * t
    ii = _iota((n2, n2), 0)
    jj = _iota((n2, n2), 1)
    it = ii & (t - 1)
    jt = jj & (t - 1)
    same64 = (ii >> 6) == (jj >> 6)
    strict = same64 & (it > jt)
    incl = same64 & (it >= jt)
    blk16 = (ii >> 4) == (jj >> 4)
    blk32 = (ii >> 5) == (jj >> 5)
    eye = jnp.where(ii == jj, 1.0, 0.0)

    ys = []
    for pr in range(RWKV_PAIRS):
        sl = slice(pr * LANES, (pr + 1) * LANES)
        a_p, r_p, b_p, k_p, v_p = a_t[:, sl], r_t[:, sl], b_t[:, sl], k_t[:, sl], v[:, sl]
        lhs = jnp.concatenate([a_p * m0, a_p * m1, r_p * m0, r_p * m1], axis=0).astype(BF16)
        rhs = jnp.concatenate([b_p * m0, b_p * m1, k_p * m0, k_p * m1], axis=0).astype(BF16)
        vst = jnp.concatenate([v_p * m0, v_p * m1], axis=0)
        gb = _dot_nt(lhs, rhs)
        a_ab = jnp.where(strict, gb[0:n2, 0:n2], 0.0)
        a_ak = jnp.where(strict, gb[0:n2, n2:2 * n2], 0.0)
        a_rb = jnp.where(incl, gb[n2:2 * n2, 0:n2], 0.0)
        a_rk = jnp.where(incl, gb[n2:2 * n2, n2:2 * n2], 0.0)
        ht = ht_scr[pr]
        p0 = _dot_nt(lhs, ht.astype(BF16))
        d1 = jnp.where(blk16, a_ab, 0.0)
        d1b = d1.astype(BF16)
        x = eye + d1
        d2 = _dot(d1b, d1b)
        d2b = d2.astype(BF16)
        x = x + _dot(x.astype(BF16), d2b)
        d4 = _dot(d2b, d2b)
        d4b = d4.astype(BF16)
        x = x + _dot(x.astype(BF16), d4b)
        d8 = _dot(d4b, d4b)
        x = x + _dot(x.astype(BF16), d8.astype(BF16))
        e32 = jnp.where(blk32 & jnp.logical_not(blk16), a_ab, 0.0).astype(BF16)
        xb = x.astype(BF16)
        x = x + _dot(xb, _dot(e32, xb).astype(BF16))
        e64 = jnp.where(jnp.logical_not(blk32), a_ab, 0.0).astype(BF16)
        xb = x.astype(BF16)
        x = x + _dot(xb, _dot(e64, xb).astype(BF16))
        vstb = vst.astype(BF16)
        rhs_u = p0[0:n2] + _dot(a_ak.astype(BF16), vstb)
        ust = _dot(x.astype(BF16), rhs_u.astype(BF16))
        uv = jnp.concatenate([ust, vst], axis=0)
        uvb = uv.astype(BF16)
        yst = p0[n2:2 * n2] + _dot(jnp.concatenate([a_rb, a_rk], axis=1).astype(BF16), uvb)
        ys.append(yst[0:t] + yst[t:n2])
        ht_scr[pr] = (ht + _dot(uv.T.astype(BF16), rhs)) * gam_last[:, sl]
    y = jnp.concatenate(ys, axis=1)
    inv_n = 1.0 / HEAD_DIM
    mean = _dot_exact_rhs(y, bd) * inv_n
    yc = y - mean
    var = _dot_exact_rhs(yc * yc, bd) * inv_n
    yn = yc * lax.rsqrt(var + RWKV_GN_EPS) * lnw_ref[...] + lnb_ref[...]
    o_ref[...] = ((yn + bonus) * g).astype(o_ref.dtype)


def _rwkv(p_r, mu, w0, w2, a0, a2, g2, k_k, k_a, r_k, lnw, lnb, bd):
    bsz, s, _ = p_r.shape
    t = RWKV_CHUNK
    vec = lambda n: pl.BlockSpec((1, n), lambda b, i: (0, 0))
    mat = lambda m, n: pl.BlockSpec((m, n), lambda b, i: (0, 0))
    return pl.pallas_call(
        _rwkv_kernel,
        grid=(bsz, s // t),
        in_specs=[
            pl.BlockSpec((None, t, RWKV_COLS_PAD), lambda b, i: (b, i, 0)),
            vec(RWKV_COLS_PAD), vec(RWKV_W), mat(LANES, RWKV_W), vec(RWKV_W), mat(LANES, RWKV_W),
            mat(LANES, RWKV_W), vec(RWKV_W), vec(RWKV_W), vec(RWKV_W), vec(RWKV_W), vec(RWKV_W),
            mat(RWKV_W, RWKV_W),
        ],
        out_specs=pl.BlockSpec((None, t, RWKV_W), lambda b, i: (b, i, 0)),
        out_shape=jax.ShapeDtypeStruct((bsz, s, RWKV_W), BF16),
        scratch_shapes=[pltpu.VMEM((RWKV_PAIRS, LANES, LANES), F32), pltpu.VMEM((8, RWKV_COLS_PAD), F32)],
        compiler_params=_cparams(("arbitrary", "arbitrary")),
        name="rwkv7",
    )(p_r, mu, w0, w2, a0, a2, g2, k_k, k_a, r_k, lnw, lnb, bd)


def _sb_kernel(q_ref, k_ref, v_ref, g_ref, bd_ref, o_ref, acc_scr, carry_scr):
    tq, tk = SB_TQ, SB_TK
    qi = pl.program_id(2)
    lane = _iota((1, LANES), 1)
    m0 = jnp.where(lane < HEAD_DIM, 1.0, 0.0).astype(BF16)
    m1 = (1.0 - m0.astype(F32)).astype(BF16)
    q = q_ref[...] * jnp.asarray(HEAD_DIM ** -0.5, BF16)
    qh = (q * m0, q * m1)
    si = _iota((tk, 2 * tk), 0)
    sj = _iota((tk, 2 * tk), 1)
    suffix = jnp.where((si > sj) | (sj >= tk), 1.0, 0.0).astype(BF16)
    acc_scr[...] = jnp.zeros_like(acc_scr)
    carry_scr[...] = jnp.zeros_like(carry_scr)

    def block(j, diagonal):
        off = pl.multiple_of(j * tk, tk)
        kb = k_ref[pl.ds(off, tk), :]
        vb = v_ref[pl.ds(off, tk), :]
        ws = []
        for h in range(2):
            z = _dot_nt(qh[h], kb)
            l1p = jnp.log(1.0 + jnp.exp(-jnp.abs(z)))
            lk = -(jnp.maximum(z, 0.0) + l1p)
            lsig = jnp.minimum(z, 0.0) - l1p
            if diagonal:
                causal = _iota((tq, tk), 1) < _iota((tq, tk), 0)
                lk = jnp.where(causal, lk, 0.0)
            hi, lo = _split2(lk)
            cs = _dot(hi, suffix) + _dot(lo, suffix)
            carry = carry_scr[h]
            wgt = jnp.exp(lsig + cs[:, 0:tk] + carry)
            if diagonal:
                wgt = jnp.where(causal, wgt, 0.0)
            carry_scr[h] = carry + cs[:, tk:2 * tk]
            ws.append(wgt.astype(BF16))
        wcat = jnp.concatenate(ws, axis=1)
        vcat = jnp.concatenate([vb * m0, vb * m1], axis=0)
        acc_scr[...] += _dot(wcat, vcat)

    block(qi, True)

    def body(jj, carry):
        block(qi - 1 - jj, False)
        return carry

    lax.fori_loop(0, qi, body, 0)
    o = acc_scr[...]
    ms = _dot_exact_rhs(o * o, bd_ref[...]) * (1.0 / HEAD_DIM)
    o_ref[...] = (o * lax.rsqrt(ms + HEAD_NORM_EPS) * g_ref[...]).astype(o_ref.dtype)


def _sb(p_s, norm_g, bd):
    bsz, s, _ = p_s.shape
    npair = SB_W // LANES
    return pl.pallas_call(
        _sb_kernel,
        grid=(bsz, npair, s // SB_TQ),
        in_specs=[
            pl.BlockSpec((None, SB_TQ, LANES), lambda b, h, i: (b, i, h)),
            pl.BlockSpec((None, s, LANES), lambda b, h, i: (b, 0, npair + h)),
            pl.BlockSpec((None, s, LANES), lambda b, h, i: (b, 0, 2 * npair + h)),
            pl.BlockSpec((1, LANES), lambda b, h, i: (0, h)),
            pl.BlockSpec((LANES, LANES), lambda b, h, i: (0, 0)),
        ],
        out_specs=pl.BlockSpec((None, SB_TQ, LANES), lambda b, h, i: (b, i, h)),
        out_shape=jax.ShapeDtypeStruct((bsz, s, SB_W), BF16),
        scratch_shapes=[pltpu.VMEM((SB_TQ, LANES), F32), pltpu.VMEM((2, SB_TQ, LANES), F32)],
        compiler_params=_cparams(("arbitrary", "arbitrary", "arbitrary")),
        name="stickbreak",
    )(p_s, p_s, p_s, norm_g, bd)


def _gla_kernel(p_ref, up_ref, gkb_ref, ng_ref, bdv_ref, exp_ref, o_ref, ht_scr):
    t = GLA_CHUNK
    kw, vw = GLA_KW, GLA_VW
    nh = GLA_HEADS

    @pl.when(pl.program_id(1) == 0)
    def _():
        ht_scr[...] = jnp.zeros_like(ht_scr)

    p = p_ref[...]
    q = p[:, 0:kw] * (GLA_KD ** -0.5)
    k = p[:, kw:2 * kw]
    v = p[:, 2 * kw:2 * kw + vw]
    g = p[:, 2 * kw + vw:2 * kw + 2 * vw]
    gk_low = p[:, 2 * kw + 2 * vw:2 * kw + 2 * vw + LANES]
    log_a = _log_sigmoid(_dot_f32(gk_low, up_ref[...]) + gkb_ref[...]) * (1.0 / GLA_NORMALIZER)
    ti = _iota((t, t), 0)
    tj = _iota((t, t), 1)
    tri_incl = jnp.where(ti >= tj, 1.0, 0.0).astype(BF16)
    beta = _dot_exact_lhs(tri_incl, log_a)
    beta_last = beta[t - 1:t, :]

    ht = ht_scr[...]
    o = _dot_nt((q * jnp.exp(beta)).astype(BF16), ht.astype(BF16))

    row = _iota((t, 1), 0)
    sub = row >> 4
    ref_rows = [beta[GLA_SUB * i - 1:GLA_SUB * i, :] for i in range(1, t // GLA_SUB)]
    beta_ref = jnp.zeros_like(beta)
    for i, rr in enumerate(ref_rows):
        beta_ref = jnp.where(sub == i + 1, rr, beta_ref)
    q_hat = q * jnp.exp(jnp.minimum(beta - beta_ref, 0.0))
    lane_k = _iota((1, kw), 1)
    lane_v = _iota((1, vw), 1)
    mk = [jnp.where((lane_k >> 5) == h, 1.0, 0.0) for h in range(nh)]
    mv = [jnp.where((lane_v >> 6) == h, 1.0, 0.0) for h in range(nh)]
    q_st = jnp.concatenate([q_hat * mk[h] for h in range(nh)], axis=0).astype(BF16)
    v_st = jnp.concatenate([v * mv[h] for h in range(nh)], axis=0).astype(BF16)
    n4 = nh * t
    ri = _iota((n4, n4), 0)
    ci = _iota((n4, n4), 1)
    rsub = (ri & (t - 1)) >> 4
    ct = ci & (t - 1)
    attn = jnp.zeros((n4, n4), F32)
    for i, rr in enumerate(ref_rows):
        k_hat = k * jnp.exp(jnp.minimum(rr - beta, 0.0))
        k_st = jnp.concatenate([k_hat * mk[h] for h in range(nh)], axis=0).astype(BF16)
        gi = _dot_nt(q_st, k_st)
        attn = jnp.where((rsub == i + 1) & (ct < GLA_SUB * (i + 1)), gi, attn)
    o_st = _dot(attn.astype(BF16), v_st)
    for h in range(nh):
        o = o + o_st[h * t:(h + 1) * t]

    expand = exp_ref[...]
    tsub = row & (GLA_SUB - 1)
    for d in range(GLA_SUB):
        if d == 0:
            kd, bd_, vd = k, beta, v
        else:
            kd = pltpu.roll(k, d, axis=0)
            bd_ = pltpu.roll(beta, d, axis=0)
            vd = pltpu.roll(v, d, axis=0)
        term = jnp.where(tsub >= d, q * kd * jnp.exp(jnp.minimum(beta - bd_, 0.0)), 0.0)
        hi, lo = _split2(term)
        o = o + (_dot(hi, expand) + _dot(lo, expand)) * vd

    k_end = (k * jnp.exp(beta_last - beta)).astype(BF16)
    upd = _dot(v.T.astype(BF16), k_end)
    hv = _iota((vw, kw), 0) >> 6
    hk = _iota((vw, kw), 1) >> 5
    ht_scr[...] = ht * jnp.exp(beta_last) + jnp.where(hv == hk, upd, 0.0)

    ms = _dot_exact_rhs(o * o, bdv_ref[...]) * (1.0 / HEAD_DIM)
    on = o * lax.rsqrt(ms + HEAD_NORM_EPS) * ng_ref[...]
    o_ref[...] = (on * (g * _sigmoid(g))).astype(o_ref.dtype)


def _gla(p_g, gk_up, gk_b, norm_g, bdv, expand):
    bsz, s, _ = p_g.shape
    t = GLA_CHUNK
    const = lambda m, n: pl.BlockSpec((m, n), lambda b, i: (0, 0))
    return pl.pallas_call(
        _gla_kernel,
        grid=(bsz, s // t),
        in_specs=[
            pl.BlockSpec((None, t, GLA_COLS_PAD), lambda b, i: (b, i, 0)),
            const(LANES, GLA_KW), const(1, GLA_KW), const(1, GLA_VW), const(GLA_VW, GLA_VW),
            const(GLA_KW, GLA_VW),
        ],
        out_specs=pl.BlockSpec((None, t, GLA_VW), lambda b, i: (b, i, 0)),
        out_shape=jax.ShapeDtypeStruct((bsz, s, GLA_VW), BF16),
        scratch_shapes=[pltpu.VMEM((GLA_VW, GLA_KW), F32)],
        compiler_params=_cparams(("arbitrary", "arbitrary")),
        name="gla",
    )(p_g, gk_up, gk_b, norm_g, bdv, expand)


def _outproj_kernel(yr_ref, ys_ref, yg_ref, x_ref, mod_ref, g_ref, w_ref, wr_ref, x1_ref, h_ref, lg_ref):
    mod = mod_ref[...]
    o1 = RWKV_W
    o2 = o1 + SB_W
    mix = (_dot(yr_ref[...], w_ref[0:o1, :]) + _dot(ys_ref[...], w_ref[o1:o2, :])
           + _dot(yg_ref[...], w_ref[o2:, :]))
    x1 = x_ref[...] + mod[2:3] * mix
    x1_ref[...] = x1
    h = _rms_mod(x1, g_ref[...], mod[4:5], mod[3:4])
    h_ref[...] = h.astype(BF16)
    lg_ref[...] = _dot_nt_f32(wr_ref[...], h)


def _outproj(y_r, y_s, y_g, x, mod_l, g, w_out, w_router_t):
    bsz, s, d = x.shape
    tm = min(512, s)
    nt = s // tm
    return pl.pallas_call(
        _outproj_kernel,
        grid=(bsz, nt),
        in_specs=[
            pl.BlockSpec((None, tm, RWKV_W), lambda b, i: (b, i, 0)),
            pl.BlockSpec((None, tm, SB_W), lambda b, i: (b, i, 0)),
            pl.BlockSpec((None, tm, GLA_VW), lambda b, i: (b, i, 0)),
            pl.BlockSpec((None, tm, d), lambda b, i: (b, i, 0)),
            pl.BlockSpec((None, 6, d), lambda b, i: (b, 0, 0)),
            pl.BlockSpec((1, d), lambda b, i: (0, 0)),
            pl.BlockSpec((d, d), lambda b, i: (0, 0)),
            pl.BlockSpec((N_EXPERTS, d), lambda b, i: (0, 0)),
        ],
        out_specs=[
            pl.BlockSpec((None, tm, d), lambda b, i: (b, i, 0)),
            pl.BlockSpec((None, tm, d), lambda b, i: (b, i, 0)),
            pl.BlockSpec((N_EXPERTS, tm), lambda b, i: (0, b * nt + i)),
        ],
        out_shape=[
            jax.ShapeDtypeStruct((bsz, s, d), F32),
            jax.ShapeDtypeStruct((bsz, s, d), BF16),
            jax.ShapeDtypeStruct((N_EXPERTS, bsz * s), F32),
        ],
        compiler_params=_cparams(("arbitrary", "arbitrary")),
        name="outproj",
    )(y_r, y_s, y_g, x, mod_l, g, w_out, w_router_t)


def _route_kernel(lg_ref, bias_ref, tri_ref, w_ref, rank_ref, cnt_ref):
    aff = _sigmoid(lg_ref[...])
    sel = aff + bias_ref[...]
    e = EXPERTS_PER_GROUP
    rows = [sel[i:i + 1, :] for i in range(N_EXPERTS)]
    arow = [aff[i:i + 1, :] for i in range(N_EXPERTS)]
    scores = []
    for gi in range(N_GROUPS):
        a, b, c, d = rows[e * gi:e * gi + e]
        scores.append(jnp.maximum(jnp.maximum(jnp.maximum(a + b, a + c), jnp.maximum(a + d, b + c)),
                                  jnp.maximum(b + d, c + d)))
    grp = jnp.zeros_like(scores[0]).astype(jnp.int32)
    best = scores[0]
    for gi in range(1, N_GROUPS):
        better = scores[gi] > best
        grp = jnp.where(better, gi, grp)
        best = jnp.where(better, scores[gi], best)
    sin, ain = [], []
    for j in range(e):
        sv, av = rows[j], arow[j]
        for gi in range(1, N_GROUPS):
            sv = jnp.where(grp == gi, rows[e * gi + j], sv)
            av = jnp.where(grp == gi, arow[e * gi + j], av)
        sin.append(sv)
        ain.append(av)
    loc1 = jnp.zeros_like(grp)
    b1 = sin[0]
    for j in range(1, e):
        better = sin[j] > b1
        loc1 = jnp.where(better, j, loc1)
        b1 = jnp.where(better, sin[j], b1)
    neg = jnp.full_like(b1, -jnp.inf)
    loc2 = jnp.zeros_like(grp)
    b2 = neg
    for j in range(e):
        cand = jnp.where(loc1 == j, neg, sin[j])
        better = cand > b2
        loc2 = jnp.where(better, j, loc2)
        b2 = jnp.where(better, cand, b2)
    a1 = ain[0]
    a2 = ain[0]
    for j in range(1, e):
        a1 = jnp.where(loc1 == j, ain[j], a1)
        a2 = jnp.where(loc2 == j, ain[j], a2)
    den = a1 + a2
    e1 = grp * e + loc1
    e2 = grp * e + loc2
    eid = _iota(aff.shape, 0)
    is1 = eid == e1
    is2 = eid == e2
    w_ref[...] = jnp.where(is1, a1 / den, jnp.where(is2, a2 / den, 0.0))
    selected = is1 | is2
    self = jnp.where(selected, 1.0, 0.0)
    excl = _dot(self.astype(BF16), tri_ref[...])
    rank_ref[...] = jnp.where(selected, excl, -1.0)
    cnt = jnp.sum(self, axis=1, keepdims=True)
    cnt_ref[...] = jnp.broadcast_to(cnt, cnt_ref.shape)


def _route(logits_t, bias, tri):
    ne, n = logits_t.shape
    tm = tri.shape[0]
    nt = n // tm
    return pl.pallas_call(
        _route_kernel,
        grid=(nt,),
        in_specs=[
            pl.BlockSpec((ne, tm), lambda i: (0, i)),
            pl.BlockSpec((ne, 1), lambda i: (0, 0)),
            pl.BlockSpec((tm, tm), lambda i: (0, 0)),
        ],
        out_specs=[
            pl.BlockSpec((ne, tm), lambda i: (0, i)),
            pl.BlockSpec((ne, tm), lambda i: (0, i)),
            pl.BlockSpec((None, ne, LANES), lambda i: (i, 0, 0)),
        ],
        out_shape=[
            jax.ShapeDtypeStruct((ne, n), F32),
            jax.ShapeDtypeStruct((ne, n), F32),
            jax.ShapeDtypeStruct((nt, ne, LANES), F32),
        ],
        compiler_params=_cparams(("arbitrary",)),
        name="route",
    )(logits_t, bias, tri)


def _moe_kernel(cnt_ref, h_ref, rank_ref, w_ref, wg_ref, wu_ref, wd_ref, x1_ref, mod_ref, o_ref, acc_scr):
    i = pl.program_id(0)
    e = pl.program_id(1)
    r = MOE_R

    @pl.when(e == 0)
    def _():
        acc_scr[...] = jnp.zeros_like(acc_scr)

    cnt = cnt_ref[i * N_EXPERTS + e]
    rank = rank_ref[...].astype(jnp.int32)
    wrow = w_ref[...]
    w_hi, w_lo = _split2(wrow)
    tmn = rank.shape[1]
    arow = _iota((LANES, tmn), 0)
    waux = jnp.where(arow == 0, w_hi.astype(F32),
                     jnp.where(arow == 1, w_lo.astype(F32), 0.0)).astype(BF16)
    riota = _iota((r, 1), 0)

    def chunk(ch, carry):
        rel = rank - ch * r
        onehot = jnp.where(rel == riota, 1.0, 0.0).astype(BF16)
        xr = _dot(onehot, h_ref[...]).astype(BF16)
        gate = _dot(xr, wg_ref[...])
        up = _dot(xr, wu_ref[...])
        act = (gate * _sigmoid(gate) * up).astype(BF16)
        y = _dot(act, wd_ref[...])
        wr = _dot_nt(onehot, waux)
        wcol = wr[:, 0:1] + wr[:, 1:2]
        yw = (y * wcol).astype(BF16)
        acc_scr[...] += lax.dot_general(onehot, yw, (((0,), (0,)), ((), ())), preferred_element_type=F32)
        return carry

    lax.fori_loop(0, (cnt + (r - 1)) >> 7, chunk, 0)

    @pl.when(e == N_EXPERTS - 1)
    def _():
        o_ref[...] = x1_ref[...] + mod_ref[5:6, :] * acc_scr[...]


def _moe(cnt, h, rank_t, w_t, w_gate, w_up, w_down, x1, mod_l, tm, tiles_per_batch):
    n, d = h.shape
    de = w_gate.shape[-1]
    nt = n // tm
    grid_spec = pltpu.PrefetchScalarGridSpec(
        num_scalar_prefetch=1,
        grid=(nt, N_EXPERTS),
        in_specs=[
            pl.BlockSpec((tm, d), lambda i, e, c: (i, 0)),
            pl.BlockSpec((None, 1, tm), lambda i, e, c: (e, 0, i)),
            pl.BlockSpec((None, 1, tm), lambda i, e, c: (e, 0, i)),
            pl.BlockSpec((None, d, de), lambda i, e, c: (e, 0, 0)),
            pl.BlockSpec((None, d, de), lambda i, e, c: (e, 0, 0)),
            pl.BlockSpec((None, de, d), lambda i, e, c: (e, 0, 0)),
            pl.BlockSpec((tm, d), lambda i, e, c: (i, 0)),
            pl.BlockSpec((None, 6, d), lambda i, e, c: (i // tiles_per_batch, 0, 0)),
        ],
        out_specs=pl.BlockSpec((tm, d), lambda i, e, c: (i, 0)),
        scratch_shapes=[pltpu.VMEM((tm, d), F32)],
    )
    return pl.pallas_call(
        _moe_kernel,
        grid_spec=grid_spec,
        out_shape=jax.ShapeDtypeStruct((n, d), F32),
        compiler_params=_cparams(("arbitrary", "arbitrary")),
        name="moe",
    )(cnt, h, rank_t.reshape(N_EXPERTS, 1, n), w_t.reshape(N_EXPERTS, 1, n), w_gate, w_up, w_down, x1, mod_l)


def _final_kernel(x_ref, g_ref, o_ref):
    x = x_ref[...]
    ms = jnp.mean(x * x, axis=-1, keepdims=True)
    o_ref[...] = x * lax.rsqrt(ms + RMS_EPS) * g_ref[...]


def _final_norm(x, g):
    n, d = x.shape
    tm = min(1024, n)
    return pl.pallas_call(
        _final_kernel,
        grid=(n // tm,),
        in_specs=[pl.BlockSpec((tm, d), lambda i: (i, 0)), pl.BlockSpec((1, d), lambda i: (0, 0))],
        out_specs=pl.BlockSpec((tm, d), lambda i: (i, 0)),
        out_shape=jax.ShapeDtypeStruct((n, d), F32),
        compiler_params=_cparams(("arbitrary",)),
        name="final_norm",
    )(x, g)


def _pad_cols(w, n):
    return jnp.pad(w, [(0, 0)] * (w.ndim - 1) + [(0, n - w.shape[-1])])


def _pad_rows(w, n):
    return jnp.pad(w, [(0, 0)] * (w.ndim - 2) + [(0, n - w.shape[-2]), (0, 0)])


def _layout_in_cols(w):
    rw = RWKV_W
    o = 3 * rw
    parts = [w[..., 0:o], _pad_cols(w[..., o:o + 64], LANES), _pad_cols(w[..., o + 64:o + 128], LANES),
             w[..., o + 128:o + 256]]
    o += 256
    parts.append(w[..., o:o + SB_COLS])
    o += SB_COLS
    parts.append(_pad_cols(w[..., o:], GLA_COLS_PAD))
    return jnp.concatenate(parts, axis=-1)


def kernel(x, c, rms_mix_g, rms_ffn_g, w_mod, b_mod, w_in, w_out, rwkv_mu, rwkv_w0, rwkv_w2, rwkv_a0, rwkv_a2, rwkv_g2, rwkv_k_k, rwkv_k_a, rwkv_r_k, rwkv_lnx_w, rwkv_lnx_b, sb_norm_g, gla_gk_up, gla_gk_b, gla_norm_g, w_router, router_bias, w_gate, w_up, w_down, final_g):
    bsz, s, d = x.shape
    depth = w_in.shape[0]
    n = bsz * s

    mod = _modulation(c, w_mod, b_mod).reshape(depth, bsz, 6, d)
    w_in_l = _layout_in_cols(w_in).astype(BF16)
    w_out_b = w_out.astype(BF16)
    mu_l = _layout_in_cols(jnp.pad(rwkv_mu, ((0, 0), (0, w_in.shape[-1] - rwkv_mu.shape[-1]))))[:, :RWKV_COLS_PAD]
    w2_p = _pad_rows(rwkv_w2, LANES)
    a2_p = _pad_rows(rwkv_a2, LANES)
    up_p = _pad_rows(gla_gk_up, LANES)
    gla_ng = jnp.tile(gla_norm_g, (1, GLA_HEADS))
    bd_rwkv = _block_diag_const(RWKV_W, HEAD_DIM, 1.0)
    bd_sb = _block_diag_const(LANES, HEAD_DIM, 1.0)
    bd_gla = _block_diag_const(GLA_VW, HEAD_DIM, 1.0)
    expand = (jnp.arange(GLA_KW)[:, None] // GLA_KD == jnp.arange(GLA_VW)[None, :] // HEAD_DIM).astype(BF16)
    tm_moe = min(MOE_TM, s)
    tri = (jnp.arange(tm_moe)[:, None] < jnp.arange(tm_moe)[None, :]).astype(BF16)
    w_router_t = w_router.T
    bias_col = router_bias.reshape(N_EXPERTS, 1)
    wg_b, wu_b, wd_b = w_gate.astype(BF16), w_up.astype(BF16), w_down.astype(BF16)
    row = lambda a: a.reshape(1, -1)

    for l in range(depth):
        p_r, p_s, p_g = _inproj(x, mod[l], row(rms_mix_g[l]), w_in_l[l])
        y_r = _rwkv(p_r, row(mu_l[l]), row(rwkv_w0[l]), w2_p[l], row(rwkv_a0[l]), a2_p[l], rwkv_g2[l],
                    row(rwkv_k_k[l]), row(rwkv_k_a[l]), row(rwkv_r_k[l]), row(rwkv_lnx_w[l]),
                    row(rwkv_lnx_b[l]), bd_rwkv)
        y_s = _sb(p_s, row(sb_norm_g[l]), bd_sb)
        y_g = _gla(p_g, up_p[l], row(gla_gk_b[l]), row(gla_ng[l]), bd_gla, expand)
        x1, h2, logits_t = _outproj(y_r, y_s, y_g, x, mod[l], row(rms_ffn_g[l]), w_out_b[l], w_router_t)
        w_t, rank_t, cnt = _route(logits_t, bias_col, tri)
        cnt_i = cnt[:, :, 0].astype(jnp.int32).reshape(-1)
        x = _moe(cnt_i, h2.reshape(n, d), rank_t, w_t, wg_b[l], wu_b[l], wd_b[l], x1.reshape(n, d),
                 mod[l], tm_moe, s // tm_moe).reshape(bsz, s, d)
    return _final_norm(x.reshape(n, d), row(final_g)).reshape(bsz, s, d)
```

```python
import functools

import jax
import jax.numpy as jnp
from jax import lax
from jax.experimental import pallas as pl
from jax.experimental.pallas import tpu as pltpu

F32 = jnp.float32
BF16 = jnp.bfloat16

LANES = 128
HEAD_DIM = 64
RWKV_W = 512
RWKV_PAIRS = RWKV_W // LANES
RWKV_COLS_PAD = 3 * RWKV_W + 3 * LANES
RWKV_CHUNK = 64
RWKV_G = 4
RWKV_GN_EPS = 64e-5
SB_W = 256
SB_COLS = 3 * SB_W
SB_TQ = 1024
SB_TK = 128
GLA_VW = 256
GLA_KW = 128
GLA_HEADS = 4
GLA_KD = 32
GLA_LORA = 16
GLA_COLS_PAD = 2 * GLA_KW + 2 * GLA_VW + LANES
GLA_CHUNK = 64
GLA_SUB = 16
GLA_NORMALIZER = 16.0
IN_COLS_PAD = RWKV_COLS_PAD + SB_COLS + GLA_COLS_PAD
HEAD_NORM_EPS = 1e-5
RMS_EPS = 1e-6
N_EXPERTS = 16
EXPERTS_PER_GROUP = 4
N_GROUPS = 4
MOE_TM = 1024
MOE_R = 128
VMEM_LIMIT = 48 * 1024 * 1024


def _dot(a, b):
    return jnp.dot(a, b, preferred_element_type=F32)


def _dot_nt(a, b):
    return lax.dot_general(a, b, (((1,), (1,)), ((), ())), preferred_element_type=F32)


def _split2(x):
    hi = x.astype(BF16)
    lo = (x - hi.astype(F32)).astype(BF16)
    return hi, lo


def _split3(x):
    hi = x.astype(BF16)
    r = x - hi.astype(F32)
    mid = r.astype(BF16)
    lo = (r - mid.astype(F32)).astype(BF16)
    return hi, mid, lo


def _dot_exact_rhs(x, m):
    hi, mid, lo = _split3(x)
    return _dot(hi, m) + _dot(mid, m) + _dot(lo, m)


def _dot_exact_lhs(m, x):
    hi, mid, lo = _split3(x)
    return _dot(m, hi) + _dot(m, mid) + _dot(m, lo)


def _dot_f32(a, b):
    ah, al = _split2(a)
    bh, bl = _split2(b)
    return _dot(ah, bh) + _dot(ah, bl) + _dot(al, bh)


def _dot_nt_f32(a, b):
    ah, al = _split2(a)
    bh, bl = _split2(b)
    return _dot_nt(ah, bh) + _dot_nt(ah, bl) + _dot_nt(al, bh)


def _softplus(x):
    return jnp.maximum(x, 0.0) + jnp.log(1.0 + jnp.exp(-jnp.abs(x)))


def _log_sigmoid(x):
    return jnp.minimum(x, 0.0) - jnp.log(1.0 + jnp.exp(-jnp.abs(x)))


def _sigmoid(x):
    return 1.0 / (1.0 + jnp.exp(-x))


def _iota(shape, dim):
    return lax.broadcasted_iota(jnp.int32, shape, dim)


def _block_diag_const(n, blk, val):
    i = jnp.arange(n)
    return jnp.where((i[:, None] // blk) == (i[None, :] // blk), val, 0.0).astype(BF16)


def _cparams(sem):
    return pltpu.CompilerParams(dimension_semantics=sem, vmem_limit_bytes=VMEM_LIMIT)


def _mod_kernel(c_ref, w_ref, b_ref, o_ref):
    c = c_ref[...]
    ca = c * _sigmoid(c)
    o_ref[...] = _dot_f32(ca, w_ref[...]) + b_ref[...]


def _modulation(c, w_mod, b_mod):
    depth, d, six_d = w_mod.shape
    bsz = c.shape[0]
    tn = 1536
    return pl.pallas_call(
        _mod_kernel,
        grid=(depth, six_d // tn),
        in_specs=[
            pl.BlockSpec((bsz, d), lambda l, j: (0, 0)),
            pl.BlockSpec((None, d, tn), lambda l, j: (l, 0, j)),
            pl.BlockSpec((None, 1, tn), lambda l, j: (l, 0, j)),
        ],
        out_specs=pl.BlockSpec((None, bsz, tn), lambda l, j: (l, 0, j)),
        out_shape=jax.ShapeDtypeStruct((depth, bsz, six_d), F32),
        compiler_params=_cparams(("arbitrary", "arbitrary")),
        name="adaln_mod",
    )(c, w_mod, b_mod.reshape(depth, 1, six_d))


def _rms_mod(x, g, scale, shift):
    ms = jnp.mean(x * x, axis=-1, keepdims=True)
    return x * lax.rsqrt(ms + RMS_EPS) * g * (1.0 + scale) + shift


def _inproj_kernel(x_ref, mod_ref, g_ref, w_ref, pr_ref, ps_ref, pg_ref):
    mod = mod_ref[...]
    h = _rms_mod(x_ref[...], g_ref[...], mod[1:2], mod[0:1]).astype(BF16)
    o1 = RWKV_COLS_PAD
    o2 = o1 + SB_COLS
    pr_ref[...] = _dot(h, w_ref[:, 0:o1])
    ps_ref[...] = _dot(h, w_ref[:, o1:o2]).astype(BF16)
    pg_ref[...] = _dot(h, w_ref[:, o2:IN_COLS_PAD])


def _inproj(x, mod_l, g, w):
    bsz, s, d = x.shape
    tm = min(512, s)
    return pl.pallas_call(
        _inproj_kernel,
        grid=(bsz, s // tm),
        in_specs=[
            pl.BlockSpec((None, tm, d), lambda b, i: (b, i, 0)),
            pl.BlockSpec((None, 6, d), lambda b, i: (b, 0, 0)),
            pl.BlockSpec((1, d), lambda b, i: (0, 0)),
            pl.BlockSpec((d, IN_COLS_PAD), lambda b, i: (0, 0)),
        ],
        out_specs=[
            pl.BlockSpec((None, tm, RWKV_COLS_PAD), lambda b, i: (b, i, 0)),
            pl.BlockSpec((None, tm, SB_COLS), lambda b, i: (b, i, 0)),
            pl.BlockSpec((None, tm, GLA_COLS_PAD), lambda b, i: (b, i, 0)),
        ],
        out_shape=[
            jax.ShapeDtypeStruct((bsz, s, RWKV_COLS_PAD), F32),
            jax.ShapeDtypeStruct((bsz, s, SB_COLS), BF16),
            jax.ShapeDtypeStruct((bsz, s, GLA_COLS_PAD), F32),
        ],
        compiler_params=_cparams(("arbitrary", "arbitrary")),
        name="inproj",
    )(x, mod_l, g, w)


def _rwkv_kernel(p_ref, mu_ref, w0_ref, w2_ref, a0_ref, a2_ref, g2_ref, kk_ref, ka_ref, rk_ref,
                 lnw_ref, lnb_ref, bd_ref, o_ref, ht_scr, prev_scr, *, ng):
    t = RWKV_CHUNK
    rows = ng * t

    @pl.when(pl.program_id(1) == 0)
    def _():
        ht_scr[...] = jnp.zeros_like(ht_scr)
        prev_scr[...] = jnp.zeros_like(prev_scr)

    p = p_ref[...].reshape(rows, RWKV_COLS_PAD)
    row = _iota((rows, 1), 0)
    prev = pltpu.roll(p, 1, axis=0)
    for gi in range(ng):
        prev = jnp.where(row == gi * t, prev_scr[gi, 0:1, :], prev)
        prev_scr[gi, 0:1, :] = p[gi * t + t - 1:gi * t + t, :]
    xm = p + (prev - p) * mu_ref[...]
    w = RWKV_W
    r = xm[:, 0:w]
    k = xm[:, w:2 * w]
    v = xm[:, 2 * w:3 * w]
    xw = xm[:, 3 * w:3 * w + LANES]
    xa = xm[:, 3 * w + LANES:3 * w + 2 * LANES]
    xg = xm[:, 3 * w + 2 * LANES:3 * w + 3 * LANES]
    w_log = -_softplus(-(w0_ref[...] + _dot_f32(jnp.tanh(xw), w2_ref[...]))) - 0.5
    lw = -jnp.exp(w_log)
    iclr = _sigmoid(a0_ref[...] + _dot_f32(xa, a2_ref[...]))
    g = _dot_f32(_sigmoid(xg), g2_ref[...])
    bd = bd_ref[...]
    kkr = k * kk_ref[...]
    ss = _dot_exact_rhs(kkr * kkr, bd)
    kk = kkr * lax.rsqrt(jnp.maximum(ss, 1e-24))
    k2 = k * (1.0 + (iclr - 1.0) * ka_ref[...])
    bonus = _dot_exact_rhs(r * k2 * rk_ref[...], bd) * v

    ti = _iota((rows, rows), 0)
    tj = _iota((rows, rows), 1)
    tri_incl = jnp.where(((ti >> 6) == (tj >> 6)) & (ti >= tj), 1.0, 0.0).astype(BF16)
    beta = _dot_exact_lhs(tri_incl, lw)
    gam = jnp.exp(beta)
    gam_inv = jnp.exp(-beta)
    a_t = -kk * jnp.exp(beta - lw)
    r_t = r * gam
    b_t = kk * iclr * gam_inv
    k_t = k2 * gam_inv

    lane = _iota((1, LANES), 1)
    m0 = jnp.where(lane < HEAD_DIM, 1.0, 0.0)
    m1 = 1.0 - m0
    n2 = 2 * t
    ii = _iota((n2, n2), 0)
    jj = _iota((n2, n2), 1)
    it = ii & (t - 1)
    jt = jj & (t - 1)
    same64 = (ii >> 6) == (jj >> 6)
    strict = same64 & (it > jt)
    incl = same64 & (it >= jt)
    blk16 = (ii >> 4) == (jj >> 4)
    blk32 = (ii >> 5) == (jj >> 5)
    not16 = jnp.logical_not(blk16)
    not32 = jnp.logical_not(blk32)
    eye = jnp.where(ii == jj, 1.0, 0.0)

    units = [(gi, pr) for gi in range(ng) for pr in range(RWKV_PAIRS)]
    U = range(len(units))

    def cut(arr, gi, pr):
        return arr[gi * t:(gi + 1) * t, pr * LANES:(pr + 1) * LANES]

    def stack(xa_, xb_, gi, pr):
        ca, cb = cut(xa_, gi, pr), cut(xb_, gi, pr)
        return jnp.concatenate([ca * m0, ca * m1, cb * m0, cb * m1], axis=0).astype(BF16)

    lhs = [stack(a_t, r_t, gi, pr) for gi, pr in units]
    rhs = [stack(b_t, k_t, gi, pr) for gi, pr in units]
    vst = [jnp.concatenate([cut(v, gi, pr) * m0, cut(v, gi, pr) * m1], axis=0) for gi, pr in units]
    vstb = [u.astype(BF16) for u in vst]
    gb = [_dot_nt(lhs[i], rhs[i]) for i in U]
    hts = [ht_scr[i] for i in U]
    p0 = [_dot_nt(lhs[i], hts[i].astype(BF16)) for i in U]
    a_ab = [jnp.where(strict, g_[0:n2, 0:n2], 0.0) for g_ in gb]
    a_ak = [jnp.where(strict, g_[0:n2, n2:2 * n2], 0.0).astype(BF16) for g_ in gb]
    a_r = [jnp.concatenate([jnp.where(incl, g_[n2:2 * n2, 0:n2], 0.0),
                            jnp.where(incl, g_[n2:2 * n2, n2:2 * n2], 0.0)], axis=1).astype(BF16) for g_ in gb]
    d1 = [jnp.where(blk16, a, 0.0) for a in a_ab]
    d1b = [d.astype(BF16) for d in d1]
    x = [eye + d for d in d1]
    d2b = [_dot(d, d).astype(BF16) for d in d1b]
    x = [x[i] + _dot(x[i].astype(BF16), d2b[i]) for i in U]
    d4b = [_dot(d, d).astype(BF16) for d in d2b]
    x = [x[i] + _dot(x[i].astype(BF16), d4b[i]) for i in U]
    d8b = [_dot(d, d).astype(BF16) for d in d4b]
    x = [x[i] + _dot(x[i].astype(BF16), d8b[i]) for i in U]
    e32 = [jnp.where(blk32 & not16, a, 0.0).astype(BF16) for a in a_ab]
    xb = [u.astype(BF16) for u in x]
    t1 = [_dot(e32[i], xb[i]).astype(BF16) for i in U]
    x = [x[i] + _dot(xb[i], t1[i]) for i in U]
    e64 = [jnp.where(not32, a, 0.0).astype(BF16) for a in a_ab]
    xb = [u.astype(BF16) for u in x]
    t2 = [_dot(e64[i], xb[i]).astype(BF16) for i in U]
    x = [x[i] + _dot(xb[i], t2[i]) for i in U]
    rhs_u = [p0[i][0:n2] + _dot(a_ak[i], vstb[i]) for i in U]
    ust = [_dot(x[i].astype(BF16), rhs_u[i].astype(BF16)) for i in U]
    uv = [jnp.concatenate([ust[i], vst[i]], axis=0) for i in U]
    yst = [p0[i][n2:2 * n2] + _dot(a_r[i], uv[i].astype(BF16)) for i in U]
    ys = [u[0:t] + u[t:n2] for u in yst]
    upd = [_dot(uv[i].T.astype(BF16), rhs[i]) for i in U]
    for i, (gi, pr) in enumerate(units):
        last = gi * t + t - 1
        ht_scr[i] = (hts[i] + upd[i]) * gam[last:last + 1, pr * LANES:(pr + 1) * LANES]
    y = jnp.concatenate([jnp.concatenate(ys[gi * RWKV_PAIRS:(gi + 1) * RWKV_PAIRS], axis=1)
                         for gi in range(ng)], axis=0)
    inv_n = 1.0 / HEAD_DIM
    mean = _dot_exact_rhs(y, bd) * inv_n
    yc = y - mean
    var = _dot_exact_rhs(yc * yc, bd) * inv_n
    yn = yc * lax.rsqrt(var + RWKV_GN_EPS) * lnw_ref[...] + lnb_ref[...]
    o_ref[...] = ((yn + bonus) * g).astype(o_ref.dtype).reshape(ng, t, RWKV_W)


def _rwkv(p_r, mu, w0, w2, a0, a2, g2, k_k, k_a, r_k, lnw, lnb, bd):
    bsz, s, _ = p_r.shape
    t = RWKV_CHUNK
    ng = RWKV_G if bsz % RWKV_G == 0 else 1
    vec = lambda n: pl.BlockSpec((1, n), lambda b, i: (0, 0))
    mat = lambda m, n: pl.BlockSpec((m, n), lambda b, i: (0, 0))
    return pl.pallas_call(
        functools.partial(_rwkv_kernel, ng=ng),
        grid=(bsz // ng, s // t),
        in_specs=[
            pl.BlockSpec((ng, t, RWKV_COLS_PAD), lambda b, i: (b, i, 0)),
            vec(RWKV_COLS_PAD), vec(RWKV_W), mat(LANES, RWKV_W), vec(RWKV_W), mat(LANES, RWKV_W),
            mat(LANES, RWKV_W), vec(RWKV_W), vec(RWKV_W), vec(RWKV_W), vec(RWKV_W), vec(RWKV_W),
            mat(RWKV_W, RWKV_W),
        ],
        out_specs=pl.BlockSpec((ng, t, RWKV_W), lambda b, i: (b, i, 0)),
        out_shape=jax.ShapeDtypeStruct((bsz, s, RWKV_W), BF16),
        scratch_shapes=[pltpu.VMEM((ng * RWKV_PAIRS, LANES, LANES), F32),
                        pltpu.VMEM((ng, 8, RWKV_COLS_PAD), F32)],
        compiler_params=_cparams(("arbitrary", "arbitrary")),
        name="rwkv7",
    )(p_r, mu, w0, w2, a0, a2, g2, k_k, k_a, r_k, lnw, lnb, bd)


def _sb_kernel(q_ref, k_ref, v_ref, g_ref, bd_ref, sfx_ref, o_ref, acc_scr, carry_scr, *, tq):
    tk = SB_TK
    nsub = tq // tk
    qi = pl.program_id(2)
    lane = _iota((1, LANES), 1)
    m0 = jnp.where(lane < HEAD_DIM, 1.0, 0.0).astype(BF16)
    m1 = (1.0 - m0.astype(F32)).astype(BF16)
    q = q_ref[...] * jnp.asarray(HEAD_DIM ** -0.5, BF16)
    qh = (q * m0, q * m1)
    suffix = sfx_ref[...]
    acc_scr[...] = jnp.zeros_like(acc_scr)
    carry_scr[...] = jnp.zeros_like(carry_scr)
    heads = range(2)

    def block(j, r0, diagonal):
        off = pl.multiple_of(j * tk, tk)
        kb = k_ref[pl.ds(off, tk), :]
        vb = v_ref[pl.ds(off, tk), :]
        vcat = jnp.concatenate([vb * m0, vb * m1], axis=0)
        z = [_dot_nt(qh[h][r0:tq], kb) for h in heads]
        l1p = [jnp.log(1.0 + jnp.exp(-jnp.abs(u))) for u in z]
        lk = [-(jnp.maximum(z[h], 0.0) + l1p[h]) for h in heads]
        lsig = [jnp.minimum(z[h], 0.0) - l1p[h] for h in heads]
        if diagonal:
            rows = tq - r0
            causal = _iota((rows, tk), 1) < _iota((rows, tk), 0)
            lk = [jnp.where(causal, u, 0.0) for u in lk]
        sp = [_split2(u) for u in lk]
        cs = [_dot(sp[h][0], suffix) + _dot(sp[h][1], suffix) for h in heads]
        carry = [carry_scr[h, r0:tq, :] for h in heads]
        wgt = [jnp.exp(lsig[h] + cs[h][:, 0:tk] + carry[h]) for h in heads]
        if diagonal:
            wgt = [jnp.where(causal, u, 0.0) for u in wgt]
        for h in heads:
            carry_scr[h, r0:tq, :] = carry[h] + cs[h][:, tk:2 * tk]
        wcat = jnp.concatenate([u.astype(BF16) for u in wgt], axis=1)
        acc_scr[r0:tq, :] += _dot(wcat, vcat)

    for jd in reversed(range(nsub)):
        block(qi * nsub + jd, jd * tk, True)

    def body(jj, c):
        block(qi * nsub - 1 - jj, 0, False)
        return c

    lax.fori_loop(0, qi * nsub, body, 0)
    o = acc_scr[...]
    ms = _dot_exact_rhs(o * o, bd_ref[...]) * (1.0 / HEAD_DIM)
    o_ref[...] = (o * lax.rsqrt(ms + HEAD_NORM_EPS) * g_ref[...]).astype(o_ref.dtype)


def _sb(p_s, norm_g, bd):
    bsz, s, _ = p_s.shape
    npair = SB_W // LANES
    tq = min(SB_TQ, s)
    si = jnp.arange(SB_TK)[:, None]
    sj = jnp.arange(2 * SB_TK)[None, :]
    sfx = ((si > sj) | (sj >= SB_TK)).astype(BF16)
    return pl.pallas_call(
        functools.partial(_sb_kernel, tq=tq),
        grid=(bsz, npair, s // tq),
        in_specs=[
            pl.BlockSpec((None, tq, LANES), lambda b, h, i: (b, i, h)),
            pl.BlockSpec((None, s, LANES), lambda b, h, i: (b, 0, npair + h)),
            pl.BlockSpec((None, s, LANES), lambda b, h, i: (b, 0, 2 * npair + h)),
            pl.BlockSpec((1, LANES), lambda b, h, i: (0, h)),
            pl.BlockSpec((LANES, LANES), lambda b, h, i: (0, 0)),
            pl.BlockSpec((SB_TK, 2 * SB_TK), lambda b, h, i: (0, 0)),
        ],
        out_specs=pl.BlockSpec((None, tq, LANES), lambda b, h, i: (b, i, h)),
        out_shape=jax.ShapeDtypeStruct((bsz, s, SB_W), BF16),
        scratch_shapes=[pltpu.VMEM((tq, LANES), F32), pltpu.VMEM((2, tq, LANES), F32)],
        compiler_params=_cparams(("arbitrary", "arbitrary", "arbitrary")),
        name="stickbreak",
    )(p_s, p_s, p_s, norm_g, bd, sfx)


def _gla_kernel(p_ref, up_ref, gkb_ref, ng_ref, bdv_ref, exp_ref, o_ref, ht_scr):
    t = GLA_CHUNK
    kw, vw = GLA_KW, GLA_VW
    nh = GLA_HEADS

    @pl.when(pl.program_id(1) == 0)
    def _():
        ht_scr[...] = jnp.zeros_like(ht_scr)

    p = p_ref[...]
    q = p[:, 0:kw] * (GLA_KD ** -0.5)
    k = p[:, kw:2 * kw]
    v = p[:, 2 * kw:2 * kw + vw]
    g = p[:, 2 * kw + vw:2 * kw + 2 * vw]
    gk_low = p[:, 2 * kw + 2 * vw:2 * kw + 2 * vw + LANES]
    log_a = _log_sigmoid(_dot_f32(gk_low, up_ref[...]) + gkb_ref[...]) * (1.0 / GLA_NORMALIZER)
    ti = _iota((t, t), 0)
    tj = _iota((t, t), 1)
    tri_incl = jnp.where(ti >= tj, 1.0, 0.0).astype(BF16)
    beta = _dot_exact_lhs(tri_incl, log_a)
    beta_last = beta[t - 1:t, :]

    ht = ht_scr[...]
    o = _dot_nt((q * jnp.exp(beta)).astype(BF16), ht.astype(BF16))

    row = _iota((t, 1), 0)
    sub = row >> 4
    ref_rows = [beta[GLA_SUB * i - 1:GLA_SUB * i, :] for i in range(1, t // GLA_SUB)]
    beta_ref = jnp.zeros_like(beta)
    for i, rr in enumerate(ref_rows):
        beta_ref = jnp.where(sub == i + 1, rr, beta_ref)
    q_hat = q * jnp.exp(jnp.minimum(beta - beta_ref, 0.0))
    lane_k = _iota((1, kw), 1)
    lane_v = _iota((1, vw), 1)
    mk = [jnp.where((lane_k >> 5) == h, 1.0, 0.0) for h in range(nh)]
    mv = [jnp.where((lane_v >> 6) == h, 1.0, 0.0) for h in range(nh)]
    q_st = jnp.concatenate([q_hat * mk[h] for h in range(nh)], axis=0).astype(BF16)
    v_st = jnp.concatenate([v * mv[h] for h in range(nh)], axis=0).astype(BF16)
    n4 = nh * t
    ri = _iota((n4, n4), 0)
    ci = _iota((n4, n4), 1)
    rsub = (ri & (t - 1)) >> 4
    ct = ci & (t - 1)
    attn = jnp.zeros((n4, n4), F32)
    for i, rr in enumerate(ref_rows):
        k_hat = k * jnp.exp(jnp.minimum(rr - beta, 0.0))
        k_st = jnp.concatenate([k_hat * mk[h] for h in range(nh)], axis=0).astype(BF16)
        gi = _dot_nt(q_st, k_st)
        attn = jnp.where((rsub == i + 1) & (ct < GLA_SUB * (i + 1)), gi, attn)
    o_st = _dot(attn.astype(BF16), v_st)
    for h in range(nh):
        o = o + o_st[h * t:(h + 1) * t]

    expand = exp_ref[...]
    tsub = row & (GLA_SUB - 1)
    for d in range(GLA_SUB):
        if d == 0:
            kd, bd_, vd = k, beta, v
        else:
            kd = pltpu.roll(k, d, axis=0)
            bd_ = pltpu.roll(beta, d, axis=0)
            vd = pltpu.roll(v, d, axis=0)
        term = jnp.where(tsub >= d, q * kd * jnp.exp(jnp.minimum(beta - bd_, 0.0)), 0.0)
        hi, lo = _split2(term)
        o = o + (_dot(hi, expand) + _dot(lo, expand)) * vd

    k_end = (k * jnp.exp(beta_last - beta)).astype(BF16)
    upd = _dot(v.T.astype(BF16), k_end)
    hv = _iota((vw, kw), 0) >> 6
    hk = _iota((vw, kw), 1) >> 5
    ht_scr[...] = ht * jnp.exp(beta_last) + jnp.where(hv == hk, upd, 0.0)

    ms = _dot_exact_rhs(o * o, bdv_ref[...]) * (1.0 / HEAD_DIM)
    on = o * lax.rsqrt(ms + HEAD_NORM_EPS) * ng_ref[...]
    o_ref[...] = (on * (g * _sigmoid(g))).astype(o_ref.dtype)


def _gla(p_g, gk_up, gk_b, norm_g, bdv, expand):
    bsz, s, _ = p_g.shape
    t = GLA_CHUNK
    const = lambda m, n: pl.BlockSpec((m, n), lambda b, i: (0, 0))
    return pl.pallas_call(
        _gla_kernel,
        grid=(bsz, s // t),
        in_specs=[
            pl.BlockSpec((None, t, GLA_COLS_PAD), lambda b, i: (b, i, 0)),
            const(LANES, GLA_KW), const(1, GLA_KW), const(1, GLA_VW), const(GLA_VW, GLA_VW),
            const(GLA_KW, GLA_VW),
        ],
        out_specs=pl.BlockSpec((None, t, GLA_VW), lambda b, i: (b, i, 0)),
        out_shape=jax.ShapeDtypeStruct((bsz, s, GLA_VW), BF16),
        scratch_shapes=[pltpu.VMEM((GLA_VW, GLA_KW), F32)],
        compiler_params=_cparams(("arbitrary", "arbitrary")),
        name="gla",
    )(p_g, gk_up, gk_b, norm_g, bdv, expand)


def _outproj_kernel(yr_ref, ys_ref, yg_ref, x_ref, mod_ref, g_ref, w_ref, wr_ref, x1_ref, h_ref, lg_ref):
    mod = mod_ref[...]
    o1 = RWKV_W
    o2 = o1 + SB_W
    mix = (_dot(yr_ref[...], w_ref[0:o1, :]) + _dot(ys_ref[...], w_ref[o1:o2, :])
           + _dot(yg_ref[...], w_ref[o2:, :]))
    x1 = x_ref[...] + mod[2:3] * mix
    x1_ref[...] = x1
    h = _rms_mod(x1, g_ref[...], mod[4:5], mod[3:4])
    h_ref[...] = h.astype(BF16)
    lg_ref[...] = _dot_nt_f32(wr_ref[...], h)


def _outproj(y_r, y_s, y_g, x, mod_l, g, w_out, w_router_t):
    bsz, s, d = x.shape
    tm = min(512, s)
    nt = s // tm
    return pl.pallas_call(
        _outproj_kernel,
        grid=(bsz, nt),
        in_specs=[
            pl.BlockSpec((None, tm, RWKV_W), lambda b, i: (b, i, 0)),
            pl.BlockSpec((None, tm, SB_W), lambda b, i: (b, i, 0)),
            pl.BlockSpec((None, tm, GLA_VW), lambda b, i: (b, i, 0)),
            pl.BlockSpec((None, tm, d), lambda b, i: (b, i, 0)),
            pl.BlockSpec((None, 6, d), lambda b, i: (b, 0, 0)),
            pl.BlockSpec((1, d), lambda b, i: (0, 0)),
            pl.BlockSpec((d, d), lambda b, i: (0, 0)),
            pl.BlockSpec((N_EXPERTS, d), lambda b, i: (0, 0)),
        ],
        out_specs=[
            pl.BlockSpec((None, tm, d), lambda b, i: (b, i, 0)),
            pl.BlockSpec((None, tm, d), lambda b, i: (b, i, 0)),
            pl.BlockSpec((N_EXPERTS, tm), lambda b, i: (0, b * nt + i)),
        ],
        out_shape=[
            jax.ShapeDtypeStruct((bsz, s, d), F32),
            jax.ShapeDtypeStruct((bsz, s, d), BF16),
            jax.ShapeDtypeStruct((N_EXPERTS, bsz * s), F32),
        ],
        compiler_params=_cparams(("arbitrary", "arbitrary")),
        name="outproj",
    )(y_r, y_s, y_g, x, mod_l, g, w_out, w_router_t)


def _route_kernel(lg_ref, bias_ref, tri_ref, w_ref, rank_ref, cnt_ref):
    aff = _sigmoid(lg_ref[...])
    sel = aff + bias_ref[...]
    e = EXPERTS_PER_GROUP
    rows = [sel[i:i + 1, :] for i in range(N_EXPERTS)]
    arow = [aff[i:i + 1, :] for i in range(N_EXPERTS)]
    scores = []
    for gi in range(N_GROUPS):
        a, b, c, d = rows[e * gi:e * gi + e]
        scores.append(jnp.maximum(jnp.maximum(jnp.maximum(a + b, a + c), jnp.maximum(a + d, b + c)),
                                  jnp.maximum(b + d, c + d)))
    grp = jnp.zeros_like(scores[0]).astype(jnp.int32)
    best = scores[0]
    for gi in range(1, N_GROUPS):
        better = scores[gi] > best
        grp = jnp.where(better, gi, grp)
        best = jnp.where(better, scores[gi], best)
    sin, ain = [], []
    for j in range(e):
        sv, av = rows[j], arow[j]
        for gi in range(1, N_GROUPS):
            sv = jnp.where(grp == gi, rows[e * gi + j], sv)
            av = jnp.where(grp == gi, arow[e * gi + j], av)
        sin.append(sv)
        ain.append(av)
    loc1 = jnp.zeros_like(grp)
    b1 = sin[0]
    for j in range(1, e):
        better = sin[j] > b1
        loc1 = jnp.where(better, j, loc1)
        b1 = jnp.where(better, sin[j], b1)
    neg = jnp.full_like(b1, -jnp.inf)
    loc2 = jnp.zeros_like(grp)
    b2 = neg
    for j in range(e):
        cand = jnp.where(loc1 == j, neg, sin[j])
        better = cand > b2
        loc2 = jnp.where(better, j, loc2)
        b2 = jnp.where(better, cand, b2)
    a1 = ain[0]
    a2 = ain[0]
    for j in range(1, e):
        a1 = jnp.where(loc1 == j, ain[j], a1)
        a2 = jnp.where(loc2 == j, ain[j], a2)
    den = a1 + a2
    e1 = grp * e + loc1
    e2 = grp * e + loc2
    eid = _iota(aff.shape, 0)
    is1 = eid == e1
    is2 = eid == e2
    w_ref[...] = jnp.where(is1, a1 / den, jnp.where(is2, a2 / den, 0.0))
    selected = is1 | is2
    self = jnp.where(selected, 1.0, 0.0)
    excl = _dot(self.astype(BF16), tri_ref[...])
    rank_ref[...] = jnp.where(selected, excl, -1.0)
    cnt = jnp.sum(self, axis=1, keepdims=True)
    cnt_ref[...] = jnp.broadcast_to(cnt, cnt_ref.shape)


def _route(logits_t, bias, tri):
    ne, n = logits_t.shape
    tm = tri.shape[0]
    nt = n // tm
    return pl.pallas_call(
        _route_kernel,
        grid=(nt,),
        in_specs=[
            pl.BlockSpec((ne, tm), lambda i: (0, i)),
            pl.BlockSpec((ne, 1), lambda i: (0, 0)),
            pl.BlockSpec((tm, tm), lambda i: (0, 0)),
        ],
        out_specs=[
            pl.BlockSpec((ne, tm), lambda i: (0, i)),
            pl.BlockSpec((ne, tm), lambda i: (0, i)),
            pl.BlockSpec((None, ne, LANES), lambda i: (i, 0, 0)),
        ],
        out_shape=[
            jax.ShapeDtypeStruct((ne, n), F32),
            jax.ShapeDtypeStruct((ne, n), F32),
            jax.ShapeDtypeStruct((nt, ne, LANES), F32),
        ],
        compiler_params=_cparams(("arbitrary",)),
        name="route",
    )(logits_t, bias, tri)


def _moe_kernel(cnt_ref, h_ref, rank_ref, w_ref, wg_ref, wu_ref, wd_ref, x1_ref, mod_ref, o_ref, acc_scr):
    i = pl.program_id(0)
    e = pl.program_id(1)
    r = MOE_R

    @pl.when(e == 0)
    def _():
        acc_scr[...] = jnp.zeros_like(acc_scr)

    cnt = cnt_ref[i * N_EXPERTS + e]
    rank = rank_ref[...].astype(jnp.int32)
    wrow = w_ref[...]
    w_hi, w_lo = _split2(wrow)
    tmn = rank.shape[1]
    arow = _iota((LANES, tmn), 0)
    waux = jnp.where(arow == 0, w_hi.astype(F32),
                     jnp.where(arow == 1, w_lo.astype(F32), 0.0)).astype(BF16)
    riota = _iota((r, 1), 0)

    def chunk(ch, carry):
        rel = rank - ch * r
        onehot = jnp.where(rel == riota, 1.0, 0.0).astype(BF16)
        xr = _dot(onehot, h_ref[...]).astype(BF16)
        gate = _dot(xr, wg_ref[...])
        up = _dot(xr, wu_ref[...])
        act = (gate * _sigmoid(gate) * up).astype(BF16)
        y = _dot(act, wd_ref[...])
        wr = _dot_nt(onehot, waux)
        wcol = wr[:, 0:1] + wr[:, 1:2]
        yw = (y * wcol).astype(BF16)
        acc_scr[...] += lax.dot_general(onehot, yw, (((0,), (0,)), ((), ())), preferred_element_type=F32)
        return carry

    lax.fori_loop(0, (cnt + (r - 1)) >> (r.bit_length() - 1), chunk, 0)

    @pl.when(e == N_EXPERTS - 1)
    def _():
        o_ref[...] = x1_ref[...] + mod_ref[5:6, :] * acc_scr[...]


def _moe(cnt, h, rank_t, w_t, w_gate, w_up, w_down, x1, mod_l, tm, tiles_per_batch):
    n, d = h.shape
    de = w_gate.shape[-1]
    nt = n // tm
    grid_spec = pltpu.PrefetchScalarGridSpec(
        num_scalar_prefetch=1,
        grid=(nt, N_EXPERTS),
        in_specs=[
            pl.BlockSpec((tm, d), lambda i, e, c: (i, 0)),
            pl.BlockSpec((None, 1, tm), lambda i, e, c: (e, 0, i)),
            pl.BlockSpec((None, 1, tm), lambda i, e, c: (e, 0, i)),
            pl.BlockSpec((None, d, de), lambda i, e, c: (e, 0, 0)),
            pl.BlockSpec((None, d, de), lambda i, e, c: (e, 0, 0)),
            pl.BlockSpec((None, de, d), lambda i, e, c: (e, 0, 0)),
            pl.BlockSpec((tm, d), lambda i, e, c: (i, 0)),
            pl.BlockSpec((None, 6, d), lambda i, e, c: (i // tiles_per_batch, 0, 0)),
        ],
        out_specs=pl.BlockSpec((tm, d), lambda i, e, c: (i, 0)),
        scratch_shapes=[pltpu.VMEM((tm, d), F32)],
    )
    return pl.pallas_call(
        _moe_kernel,
        grid_spec=grid_spec,
        out_shape=jax.ShapeDtypeStruct((n, d), F32),
        compiler_params=_cparams(("arbitrary", "arbitrary")),
        name="moe",
    )(cnt, h, rank_t.reshape(N_EXPERTS, 1, n), w_t.reshape(N_EXPERTS, 1, n), w_gate, w_up, w_down, x1, mod_l)


def _final_kernel(x_ref, g_ref, o_ref):
    x = x_ref[...]
    ms = jnp.mean(x * x, axis=-1, keepdims=True)
    o_ref[...] = x * lax.rsqrt(ms + RMS_EPS) * g_ref[...]


def _final_norm(x, g):
    n, d = x.shape
    tm = min(1024, n)
    return pl.pallas_call(
        _final_kernel,
        grid=(n // tm,),
        in_specs=[pl.BlockSpec((tm, d), lambda i: (i, 0)), pl.BlockSpec((1, d), lambda i: (0, 0))],
        out_specs=pl.BlockSpec((tm, d), lambda i: (i, 0)),
        out_shape=jax.ShapeDtypeStruct((n, d), F32),
        compiler_params=_cparams(("arbitrary",)),
        name="final_norm",
    )(x, g)


def _pad_cols(w, n):
    return jnp.pad(w, [(0, 0)] * (w.ndim - 1) + [(0, n - w.shape[-1])])


def _pad_rows(w, n):
    return jnp.pad(w, [(0, 0)] * (w.ndim - 2) + [(0, n - w.shape[-2]), (0, 0)])


def _layout_in_cols(w):
    rw = RWKV_W
    o = 3 * rw
    parts = [w[..., 0:o], _pad_cols(w[..., o:o + 64], LANES), _pad_cols(w[..., o + 64:o + 128], LANES),
             w[..., o + 128:o + 256]]
    o += 256
    parts.append(w[..., o:o + SB_COLS])
    o += SB_COLS
    parts.append(_pad_cols(w[..., o:], GLA_COLS_PAD))
    return jnp.concatenate(parts, axis=-1)


def kernel(x, c, rms_mix_g, rms_ffn_g, w_mod, b_mod, w_in, w_out, rwkv_mu, rwkv_w0, rwkv_w2, rwkv_a0, rwkv_a2, rwkv_g2, rwkv_k_k, rwkv_k_a, rwkv_r_k, rwkv_lnx_w, rwkv_lnx_b, sb_norm_g, gla_gk_up, gla_gk_b, gla_norm_g, w_router, router_bias, w_gate, w_up, w_down, final_g):
    bsz, s, d = x.shape
    depth = w_in.shape[0]
    n = bsz * s

    mod = _modulation(c, w_mod, b_mod).reshape(depth, bsz, 6, d)
    w_in_l = _layout_in_cols(w_in).astype(BF16)
    w_out_b = w_out.astype(BF16)
    mu_l = _layout_in_cols(jnp.pad(rwkv_mu, ((0, 0), (0, w_in.shape[-1] - rwkv_mu.shape[-1]))))[:, :RWKV_COLS_PAD]
    w2_p = _pad_rows(rwkv_w2, LANES)
    a2_p = _pad_rows(rwkv_a2, LANES)
    up_p = _pad_rows(gla_gk_up, LANES)
    gla_ng = jnp.tile(gla_norm_g, (1, GLA_HEADS))
    bd_rwkv = _block_diag_const(RWKV_W, HEAD_DIM, 1.0)
    bd_sb = _block_diag_const(LANES, HEAD_DIM, 1.0)
    bd_gla = _block_diag_const(GLA_VW, HEAD_DIM, 1.0)
    expand = (jnp.arange(GLA_KW)[:, None] // GLA_KD == jnp.arange(GLA_VW)[None, :] // HEAD_DIM).astype(BF16)
    tm_moe = min(MOE_TM, s)
    tri = (jnp.arange(tm_moe)[:, None] < jnp.arange(tm_moe)[None, :]).astype(BF16)
    w_router_t = w_router.T
    bias_col = router_bias.reshape(N_EXPERTS, 1)
    wg_b, wu_b, wd_b = w_gate.astype(BF16), w_up.astype(BF16), w_down.astype(BF16)
    row = lambda a: a.reshape(1, -1)

    for l in range(depth):
        p_r, p_s, p_g = _inproj(x, mod[l], row(rms_mix_g[l]), w_in_l[l])
        y_r = _rwkv(p_r, row(mu_l[l]), row(rwkv_w0[l]), w2_p[l], row(rwkv_a0[l]), a2_p[l], rwkv_g2[l],
                    row(rwkv_k_k[l]), row(rwkv_k_a[l]), row(rwkv_r_k[l]), row(rwkv_lnx_w[l]),
                    row(rwkv_lnx_b[l]), bd_rwkv)
        y_s = _sb(p_s, row(sb_norm_g[l]), bd_sb)
        y_g = _gla(p_g, up_p[l], row(gla_gk_b[l]), row(gla_ng[l]), bd_gla, expand)
        x1, h2, logits_t = _outproj(y_r, y_s, y_g, x, mod[l], row(rms_ffn_g[l]), w_out_b[l], w_router_t)
        w_t, rank_t, cnt = _route(logits_t, bias_col, tri)
        cnt_i = cnt[:, :, 0].astype(jnp.int32).reshape(-1)
        x = _moe(cnt_i, h2.reshape(n, d), rank_t, w_t, wg_b[l], wu_b[l], wd_b[l], x1.reshape(n, d),
                 mod[l], tm_moe, s // tm_moe).reshape(bsz, s, d)
    return _final_norm(x.reshape(n, d), row(final_g)).reshape(bsz, s, d)
```

```python
import functools

import jax
import jax.numpy as jnp
from jax import lax
from jax.experimental import pallas as pl
from jax.experimental.pallas import tpu as pltpu

F32 = jnp.float32
BF16 = jnp.bfloat16

LANES = 128
HEAD_DIM = 64
RWKV_W = 512
RWKV_PAIRS = RWKV_W // LANES
RWKV_COLS_PAD = 3 * RWKV_W + 3 * LANES
RWKV_CHUNK = 64
RWKV_G = 4
RWKV_GN_EPS = 64e-5
SB_W = 256
SB_COLS = 3 * SB_W
SB_TQ = 1024
SB_TK = 128
GLA_VW = 256
GLA_KW = 128
GLA_HEADS = 4
GLA_KD = 32
GLA_LORA = 16
GLA_COLS_PAD = 2 * GLA_KW + 2 * GLA_VW + LANES
GLA_CHUNK = 64
GLA_SUB = 16
GLA_G = 4
GLA_NORMALIZER = 16.0
IN_COLS_PAD = RWKV_COLS_PAD + SB_COLS + GLA_COLS_PAD
HEAD_NORM_EPS = 1e-5
RMS_EPS = 1e-6
N_EXPERTS = 16
EXPERTS_PER_GROUP = 4
N_GROUPS = 4
MOE_TM = 1024
MOE_R = 128
VMEM_LIMIT = 48 * 1024 * 1024


def _dot(a, b):
    return jnp.dot(a, b, preferred_element_type=F32)


def _dot_nt(a, b):
    return lax.dot_general(a, b, (((1,), (1,)), ((), ())), preferred_element_type=F32)


def _split2(x):
    hi = x.astype(BF16)
    lo = (x - hi.astype(F32)).astype(BF16)
    return hi, lo


def _split3(x):
    hi = x.astype(BF16)
    r = x - hi.astype(F32)
    mid = r.astype(BF16)
    lo = (r - mid.astype(F32)).astype(BF16)
    return hi, mid, lo


def _dot_exact_rhs(x, m):
    hi, mid, lo = _split3(x)
    return _dot(hi, m) + _dot(mid, m) + _dot(lo, m)


def _dot_exact_lhs(m, x):
    hi, mid, lo = _split3(x)
    return _dot(m, hi) + _dot(m, mid) + _dot(m, lo)


def _dot_f32(a, b):
    ah, al = _split2(a)
    bh, bl = _split2(b)
    return _dot(ah, bh) + _dot(ah, bl) + _dot(al, bh)


def _dot_nt_f32(a, b):
    ah, al = _split2(a)
    bh, bl = _split2(b)
    return _dot_nt(ah, bh) + _dot_nt(ah, bl) + _dot_nt(al, bh)


def _softplus(x):
    return jnp.maximum(x, 0.0) + jnp.log(1.0 + jnp.exp(-jnp.abs(x)))


def _log_sigmoid(x):
    return jnp.minimum(x, 0.0) - jnp.log(1.0 + jnp.exp(-jnp.abs(x)))


def _sigmoid(x):
    return 1.0 / (1.0 + jnp.exp(-x))


def _iota(shape, dim):
    return lax.broadcasted_iota(jnp.int32, shape, dim)


def _block_diag_const(n, blk, val):
    i = jnp.arange(n)
    return jnp.where((i[:, None] // blk) == (i[None, :] // blk), val, 0.0).astype(BF16)


def _cparams(sem):
    return pltpu.CompilerParams(dimension_semantics=sem, vmem_limit_bytes=VMEM_LIMIT)


def _mod_kernel(c_ref, w_ref, b_ref, o_ref):
    c = c_ref[...]
    ca = c * _sigmoid(c)
    o_ref[...] = _dot_f32(ca, w_ref[...]) + b_ref[...]


def _modulation(c, w_mod, b_mod):
    depth, d, six_d = w_mod.shape
    bsz = c.shape[0]
    tn = 1536
    return pl.pallas_call(
        _mod_kernel,
        grid=(depth, six_d // tn),
        in_specs=[
            pl.BlockSpec((bsz, d), lambda l, j: (0, 0)),
            pl.BlockSpec((None, d, tn), lambda l, j: (l, 0, j)),
            pl.BlockSpec((None, 1, tn), lambda l, j: (l, 0, j)),
        ],
        out_specs=pl.BlockSpec((None, bsz, tn), lambda l, j: (l, 0, j)),
        out_shape=jax.ShapeDtypeStruct((depth, bsz, six_d), F32),
        compiler_params=_cparams(("arbitrary", "arbitrary")),
        name="adaln_mod",
    )(c, w_mod, b_mod.reshape(depth, 1, six_d))


def _rms_mod(x, g, scale, shift):
    ms = jnp.mean(x * x, axis=-1, keepdims=True)
    return x * lax.rsqrt(ms + RMS_EPS) * g * (1.0 + scale) + shift


def _inproj_kernel(x_ref, mod_ref, g_ref, w_ref, pr_ref, ps_ref, pg_ref):
    mod = mod_ref[...]
    h = _rms_mod(x_ref[...], g_ref[...], mod[1:2], mod[0:1]).astype(BF16)
    o1 = RWKV_COLS_PAD
    o2 = o1 + SB_COLS
    pr_ref[...] = _dot(h, w_ref[:, 0:o1])
    ps_ref[...] = _dot(h, w_ref[:, o1:o2]).astype(BF16)
    pg_ref[...] = _dot(h, w_ref[:, o2:IN_COLS_PAD])


def _inproj(x, mod_l, g, w):
    bsz, s, d = x.shape
    tm = min(512, s)
    return pl.pallas_call(
        _inproj_kernel,
        grid=(bsz, s // tm),
        in_specs=[
            pl.BlockSpec((None, tm, d), lambda b, i: (b, i, 0)),
            pl.BlockSpec((None, 6, d), lambda b, i: (b, 0, 0)),
            pl.BlockSpec((1, d), lambda b, i: (0, 0)),
            pl.BlockSpec((d, IN_COLS_PAD), lambda b, i: (0, 0)),
        ],
        out_specs=[
            pl.BlockSpec((None, tm, RWKV_COLS_PAD), lambda b, i: (b, i, 0)),
            pl.BlockSpec((None, tm, SB_COLS), lambda b, i: (b, i, 0)),
            pl.BlockSpec((None, tm, GLA_COLS_PAD), lambda b, i: (b, i, 0)),
        ],
        out_shape=[
            jax.ShapeDtypeStruct((bsz, s, RWKV_COLS_PAD), F32),
            jax.ShapeDtypeStruct((bsz, s, SB_COLS), BF16),
            jax.ShapeDtypeStruct((bsz, s, GLA_COLS_PAD), F32),
        ],
        compiler_params=_cparams(("arbitrary", "arbitrary")),
        name="inproj",
    )(x, mod_l, g, w)


def _rwkv_kernel(p_ref, mu_ref, w0_ref, w2_ref, a0_ref, a2_ref, g2_ref, kk_ref, ka_ref, rk_ref,
                 lnw_ref, lnb_ref, bd_ref, o_ref, ht_scr, prev_scr, *, ng):
    t = RWKV_CHUNK
    rows = ng * t

    @pl.when(pl.program_id(1) == 0)
    def _():
        ht_scr[...] = jnp.zeros_like(ht_scr)
        prev_scr[...] = jnp.zeros_like(prev_scr)

    p = p_ref[...].reshape(rows, RWKV_COLS_PAD)
    row = _iota((rows, 1), 0)
    prev = pltpu.roll(p, 1, axis=0)
    for gi in range(ng):
        prev = jnp.where(row == gi * t, prev_scr[gi, 0:1, :], prev)
        prev_scr[gi, 0:1, :] = p[gi * t + t - 1:gi * t + t, :]
    xm = p + (prev - p) * mu_ref[...]
    w = RWKV_W
    r = xm[:, 0:w]
    k = xm[:, w:2 * w]
    v = xm[:, 2 * w:3 * w]
    xw = xm[:, 3 * w:3 * w + LANES]
    xa = xm[:, 3 * w + LANES:3 * w + 2 * LANES]
    xg = xm[:, 3 * w + 2 * LANES:3 * w + 3 * LANES]
    w_log = -_softplus(-(w0_ref[...] + _dot_f32(jnp.tanh(xw), w2_ref[...]))) - 0.5
    lw = -jnp.exp(w_log)
    iclr = _sigmoid(a0_ref[...] + _dot_f32(xa, a2_ref[...]))
    g = _dot_f32(_sigmoid(xg), g2_ref[...])
    bd = bd_ref[...]
    kkr = k * kk_ref[...]
    ss = _dot_exact_rhs(kkr * kkr, bd)
    kk = kkr * lax.rsqrt(jnp.maximum(ss, 1e-24))
    k2 = k * (1.0 + (iclr - 1.0) * ka_ref[...])
    bonus = _dot_exact_rhs(r * k2 * rk_ref[...], bd) * v

    ti = _iota((rows, rows), 0)
    tj = _iota((rows, rows), 1)
    tri_incl = jnp.where(((ti >> 6) == (tj >> 6)) & (ti >= tj), 1.0, 0.0).astype(BF16)
    beta = _dot_exact_lhs(tri_incl, lw)
    gam = jnp.exp(beta)
    gam_inv = jnp.exp(-beta)
    a_t = -kk * jnp.exp(beta - lw)
    r_t = r * gam
    b_t = kk * iclr * gam_inv
    k_t = k2 * gam_inv

    lane = _iota((1, LANES), 1)
    m0 = jnp.where(lane < HEAD_DIM, 1.0, 0.0)
    m1 = 1.0 - m0
    n2 = 2 * t
    ii = _iota((n2, n2), 0)
    jj = _iota((n2, n2), 1)
    it = ii & (t - 1)
    jt = jj & (t - 1)
    same64 = (ii >> 6) == (jj >> 6)
    strict = same64 & (it > jt)
    incl = same64 & (it >= jt)
    blk16 = (ii >> 4) == (jj >> 4)
    blk32 = (ii >> 5) == (jj >> 5)
    not16 = jnp.logical_not(blk16)
    not32 = jnp.logical_not(blk32)
    eye = jnp.where(ii == jj, 1.0, 0.0)

    units = [(gi, pr) for gi in range(ng) for pr in range(RWKV_PAIRS)]
    U = range(len(units))

    def cut(arr, gi, pr):
        return arr[gi * t:(gi + 1) * t, pr * LANES:(pr + 1) * LANES]

    def stack(xa_, xb_, gi, pr):
        ca, cb = cut(xa_, gi, pr), cut(xb_, gi, pr)
        return jnp.concatenate([ca * m0, ca * m1, cb * m0, cb * m1], axis=0).astype(BF16)

    lhs = [stack(a_t, r_t, gi, pr) for gi, pr in units]
    rhs = [stack(b_t, k_t, gi, pr) for gi, pr in units]
    vst = [jnp.concatenate([cut(v, gi, pr) * m0, cut(v, gi, pr) * m1], axis=0) for gi, pr in units]
    vstb = [u.astype(BF16) for u in vst]
    gb = [_dot_nt(lhs[i], rhs[i]) for i in U]
    hts = [ht_scr[i] for i in U]
    p0 = [_dot_nt(lhs[i], hts[i].astype(BF16)) for i in U]
    a_ab = [jnp.where(strict, g_[0:n2, 0:n2], 0.0) for g_ in gb]
    a_ak = [jnp.where(strict, g_[0:n2, n2:2 * n2], 0.0).astype(BF16) for g_ in gb]
    a_r = [jnp.concatenate([jnp.where(incl, g_[n2:2 * n2, 0:n2], 0.0),
                            jnp.where(incl, g_[n2:2 * n2, n2:2 * n2], 0.0)], axis=1).astype(BF16) for g_ in gb]
    d1 = [jnp.where(blk16, a, 0.0) for a in a_ab]
    d1b = [d.astype(BF16) for d in d1]
    x = [eye + d for d in d1]
    d2b = [_dot(d, d).astype(BF16) for d in d1b]
    x = [x[i] + _dot(x[i].astype(BF16), d2b[i]) for i in U]
    d4b = [_dot(d, d).astype(BF16) for d in d2b]
    x = [x[i] + _dot(x[i].astype(BF16), d4b[i]) for i in U]
    d8b = [_dot(d, d).astype(BF16) for d in d4b]
    x = [x[i] + _dot(x[i].astype(BF16), d8b[i]) for i in U]
    e32 = [jnp.where(blk32 & not16, a, 0.0).astype(BF16) for a in a_ab]
    xb = [u.astype(BF16) for u in x]
    t1 = [_dot(e32[i], xb[i]).astype(BF16) for i in U]
    x = [x[i] + _dot(xb[i], t1[i]) for i in U]
    e64 = [jnp.where(not32, a, 0.0).astype(BF16) for a in a_ab]
    xb = [u.astype(BF16) for u in x]
    t2 = [_dot(e64[i], xb[i]).astype(BF16) for i in U]
    x = [x[i] + _dot(xb[i], t2[i]) for i in U]
    rhs_u = [p0[i][0:n2] + _dot(a_ak[i], vstb[i]) for i in U]
    ust = [_dot(x[i].astype(BF16), rhs_u[i].astype(BF16)) for i in U]
    uv = [jnp.concatenate([ust[i], vst[i]], axis=0) for i in U]
    yst = [p0[i][n2:2 * n2] + _dot(a_r[i], uv[i].astype(BF16)) for i in U]
    ys = [u[0:t] + u[t:n2] for u in yst]
    upd = [_dot(uv[i].T.astype(BF16), rhs[i]) for i in U]
    for i, (gi, pr) in enumerate(units):
        last = gi * t + t - 1
        ht_scr[i] = (hts[i] + upd[i]) * gam[last:last + 1, pr * LANES:(pr + 1) * LANES]
    y = jnp.concatenate([jnp.concatenate(ys[gi * RWKV_PAIRS:(gi + 1) * RWKV_PAIRS], axis=1)
                         for gi in range(ng)], axis=0)
    inv_n = 1.0 / HEAD_DIM
    mean = _dot_exact_rhs(y, bd) * inv_n
    yc = y - mean
    var = _dot_exact_rhs(yc * yc, bd) * inv_n
    yn = yc * lax.rsqrt(var + RWKV_GN_EPS) * lnw_ref[...] + lnb_ref[...]
    o_ref[...] = ((yn + bonus) * g).astype(o_ref.dtype).reshape(ng, t, RWKV_W)


def _rwkv(p_r, mu, w0, w2, a0, a2, g2, k_k, k_a, r_k, lnw, lnb, bd):
    bsz, s, _ = p_r.shape
    t = RWKV_CHUNK
    ng = RWKV_G if bsz % RWKV_G == 0 else 1
    vec = lambda n: pl.BlockSpec((1, n), lambda b, i: (0, 0))
    mat = lambda m, n: pl.BlockSpec((m, n), lambda b, i: (0, 0))
    return pl.pallas_call(
        functools.partial(_rwkv_kernel, ng=ng),
        grid=(bsz // ng, s // t),
        in_specs=[
            pl.BlockSpec((ng, t, RWKV_COLS_PAD), lambda b, i: (b, i, 0)),
            vec(RWKV_COLS_PAD), vec(RWKV_W), mat(LANES, RWKV_W), vec(RWKV_W), mat(LANES, RWKV_W),
            mat(LANES, RWKV_W), vec(RWKV_W), vec(RWKV_W), vec(RWKV_W), vec(RWKV_W), vec(RWKV_W),
            mat(RWKV_W, RWKV_W),
        ],
        out_specs=pl.BlockSpec((ng, t, RWKV_W), lambda b, i: (b, i, 0)),
        out_shape=jax.ShapeDtypeStruct((bsz, s, RWKV_W), BF16),
        scratch_shapes=[pltpu.VMEM((ng * RWKV_PAIRS, LANES, LANES), F32),
                        pltpu.VMEM((ng, 8, RWKV_COLS_PAD), F32)],
        compiler_params=_cparams(("arbitrary", "arbitrary")),
        name="rwkv7",
    )(p_r, mu, w0, w2, a0, a2, g2, k_k, k_a, r_k, lnw, lnb, bd)


def _sb_kernel(q_ref, k_ref, v_ref, g_ref, bd_ref, sfx_ref, o_ref, acc_scr, carry_scr, *, tq):
    tk = SB_TK
    nsub = tq // tk
    qi = pl.program_id(2)
    lane = _iota((1, LANES), 1)
    m0 = jnp.where(lane < HEAD_DIM, 1.0, 0.0).astype(BF16)
    m1 = (1.0 - m0.astype(F32)).astype(BF16)
    q = q_ref[...] * jnp.asarray(HEAD_DIM ** -0.5, BF16)
    qh = (q * m0, q * m1)
    suffix = sfx_ref[...]
    acc_scr[...] = jnp.zeros_like(acc_scr)
    carry_scr[...] = jnp.zeros_like(carry_scr)
    heads = range(2)

    def block(j, r0, diagonal):
        off = pl.multiple_of(j * tk, tk)
        kb = k_ref[pl.ds(off, tk), :]
        vb = v_ref[pl.ds(off, tk), :]
        vcat = jnp.concatenate([vb * m0, vb * m1], axis=0)
        z = [_dot_nt(qh[h][r0:tq], kb) for h in heads]
        l1p = [jnp.log(1.0 + jnp.exp(-jnp.abs(u))) for u in z]
        lk = [-(jnp.maximum(z[h], 0.0) + l1p[h]) for h in heads]
        lsig = [jnp.minimum(z[h], 0.0) - l1p[h] for h in heads]
        if diagonal:
            rows = tq - r0
            causal = _iota((rows, tk), 1) < _iota((rows, tk), 0)
            lk = [jnp.where(causal, u, 0.0) for u in lk]
        cs = [_dot(lk[h].astype(BF16), suffix) for h in heads]
        carry = [carry_scr[h, r0:tq, :] for h in heads]
        wgt = [jnp.exp(lsig[h] + cs[h][:, 0:tk] + carry[h]) for h in heads]
        if diagonal:
            wgt = [jnp.where(causal, u, 0.0) for u in wgt]
        for h in heads:
            carry_scr[h, r0:tq, :] = carry[h] + cs[h][:, tk:2 * tk]
        wcat = jnp.concatenate([u.astype(BF16) for u in wgt], axis=1)
        acc_scr[r0:tq, :] += _dot(wcat, vcat)

    for jd in reversed(range(nsub)):
        block(qi * nsub + jd, jd * tk, True)

    def body(jj, c):
        block(qi * nsub - 1 - jj, 0, False)
        return c

    lax.fori_loop(0, qi * nsub, body, 0)
    o = acc_scr[...]
    ms = _dot_exact_rhs(o * o, bd_ref[...]) * (1.0 / HEAD_DIM)
    o_ref[...] = (o * lax.rsqrt(ms + HEAD_NORM_EPS) * g_ref[...]).astype(o_ref.dtype)


def _sb(p_s, norm_g, bd):
    bsz, s, _ = p_s.shape
    npair = SB_W // LANES
    tq = min(SB_TQ, s)
    si = jnp.arange(SB_TK)[:, None]
    sj = jnp.arange(2 * SB_TK)[None, :]
    sfx = ((si > sj) | (sj >= SB_TK)).astype(BF16)
    return pl.pallas_call(
        functools.partial(_sb_kernel, tq=tq),
        grid=(bsz, npair, s // tq),
        in_specs=[
            pl.BlockSpec((None, tq, LANES), lambda b, h, i: (b, i, h)),
            pl.BlockSpec((None, s, LANES), lambda b, h, i: (b, 0, npair + h)),
            pl.BlockSpec((None, s, LANES), lambda b, h, i: (b, 0, 2 * npair + h)),
            pl.BlockSpec((1, LANES), lambda b, h, i: (0, h)),
            pl.BlockSpec((LANES, LANES), lambda b, h, i: (0, 0)),
            pl.BlockSpec((SB_TK, 2 * SB_TK), lambda b, h, i: (0, 0)),
        ],
        out_specs=pl.BlockSpec((None, tq, LANES), lambda b, h, i: (b, i, h)),
        out_shape=jax.ShapeDtypeStruct((bsz, s, SB_W), BF16),
        scratch_shapes=[pltpu.VMEM((tq, LANES), F32), pltpu.VMEM((2, tq, LANES), F32)],
        compiler_params=_cparams(("arbitrary", "arbitrary", "arbitrary")),
        name="stickbreak",
    )(p_s, p_s, p_s, norm_g, bd, sfx)


def _gla_kernel(p_ref, up_ref, gkb_ref, ng_ref, bdv_ref, exp_ref, o_ref, ht_scr, *, ns):
    t = GLA_CHUNK
    kw, vw = GLA_KW, GLA_VW
    nh = GLA_HEADS
    rows = ns * t
    nsub = t // GLA_SUB
    seqs = range(ns)

    @pl.when(pl.program_id(1) == 0)
    def _():
        ht_scr[...] = jnp.zeros_like(ht_scr)

    p = p_ref[...].reshape(rows, GLA_COLS_PAD)
    q = p[:, 0:kw] * (GLA_KD ** -0.5)
    k = p[:, kw:2 * kw]
    v = p[:, 2 * kw:2 * kw + vw]
    g = p[:, 2 * kw + vw:2 * kw + 2 * vw]
    gk_low = p[:, 2 * kw + 2 * vw:2 * kw + 2 * vw + LANES]
    log_a = _log_sigmoid(_dot_f32(gk_low, up_ref[...]) + gkb_ref[...]) * (1.0 / GLA_NORMALIZER)
    ti = _iota((rows, rows), 0)
    tj = _iota((rows, rows), 1)
    tri_incl = jnp.where(((ti >> 6) == (tj >> 6)) & (ti >= tj), 1.0, 0.0).astype(BF16)
    beta = _dot_exact_lhs(tri_incl, log_a)
    row = _iota((rows, 1), 0)
    seq = row >> 6
    sub = (row & (t - 1)) >> 4

    def srows(a, si):
        return a[si * t:(si + 1) * t]

    beta_last = [beta[si * t + t - 1:si * t + t, :] for si in seqs]
    hts = [ht_scr[si] for si in seqs]
    q_exp = (q * jnp.exp(beta)).astype(BF16)
    o_inter = [_dot_nt(srows(q_exp, si), hts[si].astype(BF16)) for si in seqs]

    ref_rows = [[beta[si * t + GLA_SUB * i - 1:si * t + GLA_SUB * i, :] for i in range(1, nsub)] for si in seqs]
    beta_ref = jnp.zeros_like(beta)
    for si in seqs:
        for i in range(1, nsub):
            beta_ref = jnp.where((seq == si) & (sub == i), ref_rows[si][i - 1], beta_ref)
    q_hat = q * jnp.exp(jnp.minimum(beta - beta_ref, 0.0))
    lane_k = _iota((1, kw), 1)
    lane_v = _iota((1, vw), 1)
    mk = [jnp.where((lane_k >> 5) == h, 1.0, 0.0) for h in range(nh)]
    mv = [jnp.where((lane_v >> 6) == h, 1.0, 0.0) for h in range(nh)]
    q_st = [jnp.concatenate([srows(q_hat, si) * mk[h] for h in range(nh)], axis=0).astype(BF16) for si in seqs]
    v_st = [jnp.concatenate([srows(v, si) * mv[h] for h in range(nh)], axis=0).astype(BF16) for si in seqs]
    n4 = nh * t
    ri = _iota((n4, n4), 0)
    ci = _iota((n4, n4), 1)
    rsub = (ri & (t - 1)) >> 4
    ct = ci & (t - 1)
    attn = [jnp.zeros((n4, n4), F32) for _ in seqs]
    for i in range(1, nsub):
        k_hat = [srows(k, si) * jnp.exp(jnp.minimum(ref_rows[si][i - 1] - srows(beta, si), 0.0)) for si in seqs]
        k_st = [jnp.concatenate([k_hat[si] * mk[h] for h in range(nh)], axis=0).astype(BF16) for si in seqs]
        gi = [_dot_nt(q_st[si], k_st[si]) for si in seqs]
        sel = (rsub == i) & (ct < GLA_SUB * i)
        attn = [jnp.where(sel, gi[si], attn[si]) for si in seqs]
    o_st = [_dot(attn[si].astype(BF16), v_st[si]) for si in seqs]
    o_seq = []
    for si in seqs:
        acc = o_inter[si]
        for h in range(nh):
            acc = acc + o_st[si][h * t:(h + 1) * t]
        o_seq.append(acc)
    o = jnp.concatenate(o_seq, axis=0)

    expand = exp_ref[...]
    tsub = row & (GLA_SUB - 1)
    for d in range(GLA_SUB):
        if d == 0:
            kd, bd_, vd = k, beta, v
        else:
            kd = pltpu.roll(k, d, axis=0)
            bd_ = pltpu.roll(beta, d, axis=0)
            vd = pltpu.roll(v, d, axis=0)
        term = jnp.where(tsub >= d, q * kd * jnp.exp(jnp.minimum(beta - bd_, 0.0)), 0.0)
        hi, lo = _split2(term)
        o = o + (_dot(hi, expand) + _dot(lo, expand)) * vd

    hv = _iota((vw, kw), 0) >> 6
    hk = _iota((vw, kw), 1) >> 5
    k_end = [(srows(k, si) * jnp.exp(beta_last[si] - srows(beta, si))).astype(BF16) for si in seqs]
    upd = [_dot(srows(v, si).T.astype(BF16), k_end[si]) for si in seqs]
    for si in seqs:
        ht_scr[si] = hts[si] * jnp.exp(beta_last[si]) + jnp.where(hv == hk, upd[si], 0.0)

    ms = _dot_exact_rhs(o * o, bdv_ref[...]) * (1.0 / HEAD_DIM)
    on = o * lax.rsqrt(ms + HEAD_NORM_EPS) * ng_ref[...]
    o_ref[...] = (on * (g * _sigmoid(g))).astype(o_ref.dtype).reshape(ns, t, vw)


def _gla(p_g, gk_up, gk_b, norm_g, bdv, expand):
    bsz, s, _ = p_g.shape
    t = GLA_CHUNK
    ns = GLA_G if bsz % GLA_G == 0 else 1
    const = lambda m, n: pl.BlockSpec((m, n), lambda b, i: (0, 0))
    return pl.pallas_call(
        functools.partial(_gla_kernel, ns=ns),
        grid=(bsz // ns, s // t),
        in_specs=[
            pl.BlockSpec((ns, t, GLA_COLS_PAD), lambda b, i: (b, i, 0)),
            const(LANES, GLA_KW), const(1, GLA_KW), const(1, GLA_VW), const(GLA_VW, GLA_VW),
            const(GLA_KW, GLA_VW),
        ],
        out_specs=pl.BlockSpec((ns, t, GLA_VW), lambda b, i: (b, i, 0)),
        out_shape=jax.ShapeDtypeStruct((bsz, s, GLA_VW), BF16),
        scratch_shapes=[pltpu.VMEM((ns, GLA_VW, GLA_KW), F32)],
        compiler_params=_cparams(("arbitrary", "arbitrary")),
        name="gla",
    )(p_g, gk_up, gk_b, norm_g, bdv, expand)


def _outproj_kernel(yr_ref, ys_ref, yg_ref, x_ref, mod_ref, g_ref, w_ref, wr_ref, x1_ref, h_ref, lg_ref):
    mod = mod_ref[...]
    o1 = RWKV_W
    o2 = o1 + SB_W
    mix = (_dot(yr_ref[...], w_ref[0:o1, :]) + _dot(ys_ref[...], w_ref[o1:o2, :])
           + _dot(yg_ref[...], w_ref[o2:, :]))
    x1 = x_ref[...] + mod[2:3] * mix
    x1_ref[...] = x1
    h = _rms_mod(x1, g_ref[...], mod[4:5], mod[3:4])
    h_ref[...] = h.astype(BF16)
    lg_ref[...] = _dot_nt_f32(wr_ref[...], h)


def _outproj(y_r, y_s, y_g, x, mod_l, g, w_out, w_router_t):
    bsz, s, d = x.shape
    tm = min(512, s)
    nt = s // tm
    return pl.pallas_call(
        _outproj_kernel,
        grid=(bsz, nt),
        in_specs=[
            pl.BlockSpec((None, tm, RWKV_W), lambda b, i: (b, i, 0)),
            pl.BlockSpec((None, tm, SB_W), lambda b, i: (b, i, 0)),
            pl.BlockSpec((None, tm, GLA_VW), lambda b, i: (b, i, 0)),
            pl.BlockSpec((None, tm, d), lambda b, i: (b, i, 0)),
            pl.BlockSpec((None, 6, d), lambda b, i: (b, 0, 0)),
            pl.BlockSpec((1, d), lambda b, i: (0, 0)),
            pl.BlockSpec((d, d), lambda b, i: (0, 0)),
            pl.BlockSpec((N_EXPERTS, d), lambda b, i: (0, 0)),
        ],
        out_specs=[
            pl.BlockSpec((None, tm, d), lambda b, i: (b, i, 0)),
            pl.BlockSpec((None, tm, d), lambda b, i: (b, i, 0)),
            pl.BlockSpec((N_EXPERTS, tm), lambda b, i: (0, b * nt + i)),
        ],
        out_shape=[
            jax.ShapeDtypeStruct((bsz, s, d), F32),
            jax.ShapeDtypeStruct((bsz, s, d), BF16),
            jax.ShapeDtypeStruct((N_EXPERTS, bsz * s), F32),
        ],
        compiler_params=_cparams(("arbitrary", "arbitrary")),
        name="outproj",
    )(y_r, y_s, y_g, x, mod_l, g, w_out, w_router_t)


def _route_kernel(lg_ref, bias_ref, tri_ref, w_ref, rank_ref, cnt_ref):
    aff = _sigmoid(lg_ref[...])
    sel = aff + bias_ref[...]
    e = EXPERTS_PER_GROUP
    rows = [sel[i:i + 1, :] for i in range(N_EXPERTS)]
    arow = [aff[i:i + 1, :] for i in range(N_EXPERTS)]
    scores = []
    for gi in range(N_GROUPS):
        a, b, c, d = rows[e * gi:e * gi + e]
        scores.append(jnp.maximum(jnp.maximum(jnp.maximum(a + b, a + c), jnp.maximum(a + d, b + c)),
                                  jnp.maximum(b + d, c + d)))
    grp = jnp.zeros_like(scores[0]).astype(jnp.int32)
    best = scores[0]
    for gi in range(1, N_GROUPS):
        better = scores[gi] > best
        grp = jnp.where(better, gi, grp)
        best = jnp.where(better, scores[gi], best)
    sin, ain = [], []
    for j in range(e):
        sv, av = rows[j], arow[j]
        for gi in range(1, N_GROUPS):
            sv = jnp.where(grp == gi, rows[e * gi + j], sv)
            av = jnp.where(grp == gi, arow[e * gi + j], av)
        sin.append(sv)
        ain.append(av)
    loc1 = jnp.zeros_like(grp)
    b1 = sin[0]
    for j in range(1, e):
        better = sin[j] > b1
        loc1 = jnp.where(better, j, loc1)
        b1 = jnp.where(better, sin[j], b1)
    neg = jnp.full_like(b1, -jnp.inf)
    loc2 = jnp.zeros_like(grp)
    b2 = neg
    for j in range(e):
        cand = jnp.where(loc1 == j, neg, sin[j])
        better = cand > b2
        loc2 = jnp.where(better, j, loc2)
        b2 = jnp.where(better, cand, b2)
    a1 = ain[0]
    a2 = ain[0]
    for j in range(1, e):
        a1 = jnp.where(loc1 == j, ain[j], a1)
        a2 = jnp.where(loc2 == j, ain[j], a2)
    den = a1 + a2
    e1 = grp * e + loc1
    e2 = grp * e + loc2
    eid = _iota(aff.shape, 0)
    is1 = eid == e1
    is2 = eid == e2
    w_ref[...] = jnp.where(is1, a1 / den, jnp.where(is2, a2 / den, 0.0))
    selected = is1 | is2
    self = jnp.where(selected, 1.0, 0.0)
    excl = _dot(self.astype(BF16), tri_ref[...])
    rank_ref[...] = jnp.where(selected, excl, -1.0)
    cnt = jnp.sum(self, axis=1, keepdims=True)
    cnt_ref[...] = jnp.broadcast_to(cnt, cnt_ref.shape)


def _route(logits_t, bias, tri):
    ne, n = logits_t.shape
    tm = tri.shape[0]
    nt = n // tm
    return pl.pallas_call(
        _route_kernel,
        grid=(nt,),
        in_specs=[
            pl.BlockSpec((ne, tm), lambda i: (0, i)),
            pl.BlockSpec((ne, 1), lambda i: (0, 0)),
            pl.BlockSpec((tm, tm), lambda i: (0, 0)),
        ],
        out_specs=[
            pl.BlockSpec((ne, tm), lambda i: (0, i)),
            pl.BlockSpec((ne, tm), lambda i: (0, i)),
            pl.BlockSpec((None, ne, LANES), lambda i: (i, 0, 0)),
        ],
        out_shape=[
            jax.ShapeDtypeStruct((ne, n), F32),
            jax.ShapeDtypeStruct((ne, n), F32),
            jax.ShapeDtypeStruct((nt, ne, LANES), F32),
        ],
        compiler_params=_cparams(("arbitrary",)),
        name="route",
    )(logits_t, bias, tri)


def _moe_kernel(cnt_ref, h_ref, rank_ref, w_ref, wg_ref, wu_ref, wd_ref, x1_ref, mod_ref, o_ref,
                acc_scr, obuf, ybuf, pend):
    i = pl.program_id(0)
    e = pl.program_id(1)
    r = MOE_R

    @pl.when(e == 0)
    def _():
        acc_scr[...] = jnp.zeros_like(acc_scr)
        pend[0] = 0

    def scatter(nrows):
        acc_scr[...] += lax.dot_general(obuf[0:nrows, :], ybuf[0:nrows, :], (((0,), (0,)), ((), ())),
                                        preferred_element_type=F32)

    cnt = cnt_ref[i * N_EXPERTS + e]
    rank = rank_ref[...].astype(jnp.int32)
    wrow = w_ref[...]
    w_hi, w_lo = _split2(wrow)
    tmn = rank.shape[1]
    arow = _iota((LANES, tmn), 0)
    waux = jnp.where(arow == 0, w_hi.astype(F32),
                     jnp.where(arow == 1, w_lo.astype(F32), 0.0)).astype(BF16)
    riota = _iota((r, 1), 0)

    def chunk(ch, carry):
        rel = rank - ch * r
        onehot = jnp.where(rel == riota, 1.0, 0.0).astype(BF16)
        xr = _dot(onehot, h_ref[...]).astype(BF16)
        gate = _dot(xr, wg_ref[...])
        up = _dot(xr, wu_ref[...])
        act = (gate * _sigmoid(gate) * up).astype(BF16)
        y = _dot(act, wd_ref[...])
        wr = _dot_nt(onehot, waux)
        wcol = wr[:, 0:1] + wr[:, 1:2]
        slot = pend[0]
        off = pl.multiple_of(slot * r, r)
        obuf[pl.ds(off, r), :] = onehot
        ybuf[pl.ds(off, r), :] = (y * wcol).astype(BF16)

        @pl.when(slot == 1)
        def _():
            scatter(2 * r)

        pend[0] = 1 - slot
        return carry

    lax.fori_loop(0, (cnt + (r - 1)) >> (r.bit_length() - 1), chunk, 0)

    @pl.when(e == N_EXPERTS - 1)
    def _():
        @pl.when(pend[0] == 1)
        def _():
            scatter(r)

        o_ref[...] = x1_ref[...] + mod_ref[5:6, :] * acc_scr[...]


def _moe(cnt, h, rank_t, w_t, w_gate, w_up, w_down, x1, mod_l, tm, tiles_per_batch):
    n, d = h.shape
    de = w_gate.shape[-1]
    nt = n // tm
    grid_spec = pltpu.PrefetchScalarGridSpec(
        num_scalar_prefetch=1,
        grid=(nt, N_EXPERTS),
        in_specs=[
            pl.BlockSpec((tm, d), lambda i, e, c: (i, 0)),
            pl.BlockSpec((None, 1, tm), lambda i, e, c: (e, 0, i)),
            pl.BlockSpec((None, 1, tm), lambda i, e, c: (e, 0, i)),
            pl.BlockSpec((None, d, de), lambda i, e, c: (e, 0, 0)),
            pl.BlockSpec((None, d, de), lambda i, e, c: (e, 0, 0)),
            pl.BlockSpec((None, de, d), lambda i, e, c: (e, 0, 0)),
            pl.BlockSpec((tm, d), lambda i, e, c: (i, 0)),
            pl.BlockSpec((None, 6, d), lambda i, e, c: (i // tiles_per_batch, 0, 0)),
        ],
        out_specs=pl.BlockSpec((tm, d), lambda i, e, c: (i, 0)),
        scratch_shapes=[pltpu.VMEM((tm, d), F32), pltpu.VMEM((2 * MOE_R, tm), BF16),
                        pltpu.VMEM((2 * MOE_R, d), BF16), pltpu.SMEM((1,), jnp.int32)],
    )
    return pl.pallas_call(
        _moe_kernel,
        grid_spec=grid_spec,
        out_shape=jax.ShapeDtypeStruct((n, d), F32),
        compiler_params=_cparams(("arbitrary", "arbitrary")),
        name="moe",
    )(cnt, h, rank_t.reshape(N_EXPERTS, 1, n), w_t.reshape(N_EXPERTS, 1, n), w_gate, w_up, w_down, x1, mod_l)


def _final_kernel(x_ref, g_ref, o_ref):
    x = x_ref[...]
    ms = jnp.mean(x * x, axis=-1, keepdims=True)
    o_ref[...] = x * lax.rsqrt(ms + RMS_EPS) * g_ref[...]


def _final_norm(x, g):
    n, d = x.shape
    tm = min(1024, n)
    return pl.pallas_call(
        _final_kernel,
        grid=(n // tm,),
        in_specs=[pl.BlockSpec((tm, d), lambda i: (i, 0)), pl.BlockSpec((1, d), lambda i: (0, 0))],
        out_specs=pl.BlockSpec((tm, d), lambda i: (i, 0)),
        out_shape=jax.ShapeDtypeStruct((n, d), F32),
        compiler_params=_cparams(("arbitrary",)),
        name="final_norm",
    )(x, g)


def _pad_cols(w, n):
    return jnp.pad(w, [(0, 0)] * (w.ndim - 1) + [(0, n - w.shape[-1])])


def _pad_rows(w, n):
    return jnp.pad(w, [(0, 0)] * (w.ndim - 2) + [(0, n - w.shape[-2]), (0, 0)])


def _layout_in_cols(w):
    rw = RWKV_W
    o = 3 * rw
    parts = [w[..., 0:o], _pad_cols(w[..., o:o + 64], LANES), _pad_cols(w[..., o + 64:o + 128], LANES),
             w[..., o + 128:o + 256]]
    o += 256
    parts.append(w[..., o:o + SB_COLS])
    o += SB_COLS
    parts.append(_pad_cols(w[..., o:], GLA_COLS_PAD))
    return jnp.concatenate(parts, axis=-1)


def kernel(x, c, rms_mix_g, rms_ffn_g, w_mod, b_mod, w_in, w_out, rwkv_mu, rwkv_w0, rwkv_w2, rwkv_a0, rwkv_a2, rwkv_g2, rwkv_k_k, rwkv_k_a, rwkv_r_k, rwkv_lnx_w, rwkv_lnx_b, sb_norm_g, gla_gk_up, gla_gk_b, gla_norm_g, w_router, router_bias, w_gate, w_up, w_down, final_g):
    bsz, s, d = x.shape
    depth = w_in.shape[0]
    n = bsz * s

    mod = _modulation(c, w_mod, b_mod).reshape(depth, bsz, 6, d)
    w_in_l = _layout_in_cols(w_in).astype(BF16)
    w_out_b = w_out.astype(BF16)
    mu_l = _layout_in_cols(jnp.pad(rwkv_mu, ((0, 0), (0, w_in.shape[-1] - rwkv_mu.shape[-1]))))[:, :RWKV_COLS_PAD]
    w2_p = _pad_rows(rwkv_w2, LANES)
    a2_p = _pad_rows(rwkv_a2, LANES)
    up_p = _pad_rows(gla_gk_up, LANES)
    gla_ng = jnp.tile(gla_norm_g, (1, GLA_HEADS))
    bd_rwkv = _block_diag_const(RWKV_W, HEAD_DIM, 1.0)
    bd_sb = _block_diag_const(LANES, HEAD_DIM, 1.0)
    bd_gla = _block_diag_const(GLA_VW, HEAD_DIM, 1.0)
    expand = (jnp.arange(GLA_KW)[:, None] // GLA_KD == jnp.arange(GLA_VW)[None, :] // HEAD_DIM).astype(BF16)
    tm_moe = min(MOE_TM, s)
    tri = (jnp.arange(tm_moe)[:, None] < jnp.arange(tm_moe)[None, :]).astype(BF16)
    w_router_t = w_router.T
    bias_col = router_bias.reshape(N_EXPERTS, 1)
    wg_b, wu_b, wd_b = w_gate.astype(BF16), w_up.astype(BF16), w_down.astype(BF16)
    row = lambda a: a.reshape(1, -1)

    for l in range(depth):
        p_r, p_s, p_g = _inproj(x, mod[l], row(rms_mix_g[l]), w_in_l[l])
        y_r = _rwkv(p_r, row(mu_l[l]), row(rwkv_w0[l]), w2_p[l], row(rwkv_a0[l]), a2_p[l], rwkv_g2[l],
                    row(rwkv_k_k[l]), row(rwkv_k_a[l]), row(rwkv_r_k[l]), row(rwkv_lnx_w[l]),
                    row(rwkv_lnx_b[l]), bd_rwkv)
        y_s = _sb(p_s, row(sb_norm_g[l]), bd_sb)
        y_g = _gla(p_g, up_p[l], row(gla_gk_b[l]), row(gla_ng[l]), bd_gla, expand)
        x1, h2, logits_t = _outproj(y_r, y_s, y_g, x, mod[l], row(rms_ffn_g[l]), w_out_b[l], w_router_t)
        w_t, rank_t, cnt = _route(logits_t, bias_col, tri)
        cnt_i = cnt[:, :, 0].astype(jnp.int32).reshape(-1)
        x = _moe(cnt_i, h2.reshape(n, d), rank_t, w_t, wg_b[l], wu_b[l], wd_b[l], x1.reshape(n, d),
                 mod[l], tm_moe, s // tm_moe).reshape(bsz, s, d)
    return _final_norm(x.reshape(n, d), row(final_g)).reshape(bsz, s, d)
```

```python
import functools

import jax
import jax.numpy as jnp
from jax import lax
from jax.experimental import pallas as pl
from jax.experimental.pallas import tpu as pltpu

F32 = jnp.float32
BF16 = jnp.bfloat16

LANES = 128
HEAD_DIM = 64
RWKV_W = 512
RWKV_PAIRS = RWKV_W // LANES
RWKV_COLS_PAD = 3 * RWKV_W + 3 * LANES
RWKV_CHUNK = 64
RWKV_G = 4
RWKV_GN_EPS = 64e-5
SB_W = 256
SB_COLS = 3 * SB_W
SB_TQ = 1024
SB_TK = 128
SB_DEAD = -104.0
GLA_VW = 256
GLA_KW = 128
GLA_HEADS = 4
GLA_KD = 32
GLA_LORA = 16
GLA_COLS_PAD = 2 * GLA_KW + 2 * GLA_VW + LANES
GLA_CHUNK = 64
GLA_SUB = 16
GLA_G = 4
GLA_NORMALIZER = 16.0
IN_COLS_PAD = RWKV_COLS_PAD + SB_COLS + GLA_COLS_PAD
HEAD_NORM_EPS = 1e-5
RMS_EPS = 1e-6
N_EXPERTS = 16
EXPERTS_PER_GROUP = 4
N_GROUPS = 4
MOE_TM = 1024
MOE_R = 128
VMEM_LIMIT = 48 * 1024 * 1024


def _dot(a, b):
    return jnp.dot(a, b, preferred_element_type=F32)


def _dot_nt(a, b):
    return lax.dot_general(a, b, (((1,), (1,)), ((), ())), preferred_element_type=F32)


def _split2(x):
    hi = x.astype(BF16)
    lo = (x - hi.astype(F32)).astype(BF16)
    return hi, lo


def _split3(x):
    hi = x.astype(BF16)
    r = x - hi.astype(F32)
    mid = r.astype(BF16)
    lo = (r - mid.astype(F32)).astype(BF16)
    return hi, mid, lo


def _dot_seg(x, m):
    hi, lo = _split2(x)
    return _dot(hi, m) + _dot(lo, m)


def _dot_exact_lhs(m, x):
    hi, mid, lo = _split3(x)
    return _dot(m, hi) + _dot(m, mid) + _dot(m, lo)


def _dot_f32(a, b):
    ah, al = _split2(a)
    bh, bl = _split2(b)
    return _dot(ah, bh) + _dot(ah, bl) + _dot(al, bh)


def _dot_nt_f32(a, b):
    ah, al = _split2(a)
    bh, bl = _split2(b)
    return _dot_nt(ah, bh) + _dot_nt(ah, bl) + _dot_nt(al, bh)


def _softplus(x):
    return jnp.maximum(x, 0.0) + jnp.log(1.0 + jnp.exp(-jnp.abs(x)))


def _log_sigmoid(x):
    return jnp.minimum(x, 0.0) - jnp.log(1.0 + jnp.exp(-jnp.abs(x)))


def _sigmoid(x):
    return 1.0 / (1.0 + jnp.exp(-x))


def _iota(shape, dim):
    return lax.broadcasted_iota(jnp.int32, shape, dim)


def _block_diag_const(n, blk, val):
    i = jnp.arange(n)
    return jnp.where((i[:, None] // blk) == (i[None, :] // blk), val, 0.0).astype(BF16)


def _cparams(sem):
    return pltpu.CompilerParams(dimension_semantics=sem, vmem_limit_bytes=VMEM_LIMIT)


def _mod_kernel(c_ref, w_ref, b_ref, o_ref):
    c = c_ref[...]
    ca = c * _sigmoid(c)
    o_ref[...] = _dot_f32(ca, w_ref[...]) + b_ref[...]


def _modulation(c, w_mod, b_mod):
    depth, d, six_d = w_mod.shape
    bsz = c.shape[0]
    tn = 1536
    return pl.pallas_call(
        _mod_kernel,
        grid=(depth, six_d // tn),
        in_specs=[
            pl.BlockSpec((bsz, d), lambda l, j: (0, 0)),
            pl.BlockSpec((None, d, tn), lambda l, j: (l, 0, j)),
            pl.BlockSpec((None, 1, tn), lambda l, j: (l, 0, j)),
        ],
        out_specs=pl.BlockSpec((None, bsz, tn), lambda l, j: (l, 0, j)),
        out_shape=jax.ShapeDtypeStruct((depth, bsz, six_d), F32),
        compiler_params=_cparams(("arbitrary", "arbitrary")),
        name="adaln_mod",
    )(c, w_mod, b_mod.reshape(depth, 1, six_d))


def _rms_mod(x, g, scale, shift):
    ms = jnp.mean(x * x, axis=-1, keepdims=True)
    return x * lax.rsqrt(ms + RMS_EPS) * g * (1.0 + scale) + shift


def _inproj_kernel(x_ref, mod_ref, g_ref, w_ref, pr_ref, ps_ref, pg_ref):
    mod = mod_ref[...]
    h = _rms_mod(x_ref[...], g_ref[...], mod[1:2], mod[0:1]).astype(BF16)
    o1 = RWKV_COLS_PAD
    o2 = o1 + SB_COLS
    pr_ref[...] = _dot(h, w_ref[:, 0:o1])
    ps_ref[...] = _dot(h, w_ref[:, o1:o2]).astype(BF16)
    pg_ref[...] = _dot(h, w_ref[:, o2:IN_COLS_PAD])


def _inproj(x, mod_l, g, w):
    bsz, s, d = x.shape
    tm = min(512, s)
    return pl.pallas_call(
        _inproj_kernel,
        grid=(bsz, s // tm),
        in_specs=[
            pl.BlockSpec((None, tm, d), lambda b, i: (b, i, 0)),
            pl.BlockSpec((None, 6, d), lambda b, i: (b, 0, 0)),
            pl.BlockSpec((1, d), lambda b, i: (0, 0)),
            pl.BlockSpec((d, IN_COLS_PAD), lambda b, i: (0, 0)),
        ],
        out_specs=[
            pl.BlockSpec((None, tm, RWKV_COLS_PAD), lambda b, i: (b, i, 0)),
            pl.BlockSpec((None, tm, SB_COLS), lambda b, i: (b, i, 0)),
            pl.BlockSpec((None, tm, GLA_COLS_PAD), lambda b, i: (b, i, 0)),
        ],
        out_shape=[
            jax.ShapeDtypeStruct((bsz, s, RWKV_COLS_PAD), F32),
            jax.ShapeDtypeStruct((bsz, s, SB_COLS), BF16),
            jax.ShapeDtypeStruct((bsz, s, GLA_COLS_PAD), F32),
        ],
        compiler_params=_cparams(("arbitrary", "arbitrary")),
        name="inproj",
    )(x, mod_l, g, w)


def _rwkv_kernel(p_ref, mu_ref, w0_ref, w2_ref, a0_ref, a2_ref, g2_ref, kk_ref, ka_ref, rk_ref,
                 lnw_ref, lnb_ref, bd_ref, o_ref, ht_scr, prev_scr, *, ng):
    t = RWKV_CHUNK
    rows = ng * t

    @pl.when(pl.program_id(1) == 0)
    def _():
        ht_scr[...] = jnp.zeros_like(ht_scr)
        prev_scr[...] = jnp.zeros_like(prev_scr)

    p = p_ref[...].reshape(rows, RWKV_COLS_PAD)
    row = _iota((rows, 1), 0)
    prev = pltpu.roll(p, 1, axis=0)
    for gi in range(ng):
        prev = jnp.where(row == gi * t, prev_scr[gi, 0:1, :], prev)
        prev_scr[gi, 0:1, :] = p[gi * t + t - 1:gi * t + t, :]
    xm = p + (prev - p) * mu_ref[...]
    w = RWKV_W
    r = xm[:, 0:w]
    k = xm[:, w:2 * w]
    v = xm[:, 2 * w:3 * w]
    xw = xm[:, 3 * w:3 * w + LANES]
    xa = xm[:, 3 * w + LANES:3 * w + 2 * LANES]
    xg = xm[:, 3 * w + 2 * LANES:3 * w + 3 * LANES]
    w_log = -_softplus(-(w0_ref[...] + _dot_f32(jnp.tanh(xw), w2_ref[...]))) - 0.5
    lw = -jnp.exp(w_log)
    iclr = _sigmoid(a0_ref[...] + _dot_f32(xa, a2_ref[...]))
    g = _dot_f32(_sigmoid(xg), g2_ref[...])
    bd = bd_ref[...]
    kkr = k * kk_ref[...]
    ss = _dot_seg(kkr * kkr, bd)
    kk = kkr * lax.rsqrt(jnp.maximum(ss, 1e-24))
    k2 = k * (1.0 + (iclr - 1.0) * ka_ref[...])
    bonus = _dot_seg(r * k2 * rk_ref[...], bd) * v

    ti = _iota((rows, rows), 0)
    tj = _iota((rows, rows), 1)
    tri_incl = jnp.where(((ti >> 6) == (tj >> 6)) & (ti >= tj), 1.0, 0.0).astype(BF16)
    beta = _dot_exact_lhs(tri_incl, lw)
    gam = jnp.exp(beta)
    gam_inv = jnp.exp(-beta)
    a_t = -kk * jnp.exp(beta - lw)
    r_t = r * gam
    b_t = kk * iclr * gam_inv
    k_t = k2 * gam_inv

    lane = _iota((1, LANES), 1)
    m0 = jnp.where(lane < HEAD_DIM, 1.0, 0.0)
    m1 = 1.0 - m0
    n2 = 2 * t
    ii = _iota((n2, n2), 0)
    jj = _iota((n2, n2), 1)
    it = ii & (t - 1)
    jt = jj & (t - 1)
    same64 = (ii >> 6) == (jj >> 6)
    strict = same64 & (it > jt)
    incl = same64 & (it >= jt)
    blk16 = (ii >> 4) == (jj >> 4)
    blk32 = (ii >> 5) == (jj >> 5)
    not16 = jnp.logical_not(blk16)
    not32 = jnp.logical_not(blk32)
    eye = jnp.where(ii == jj, 1.0, 0.0)

    units = [(gi, pr) for gi in range(ng) for pr in range(RWKV_PAIRS)]
    U = range(len(units))

    def cut(arr, gi, pr):
        return arr[gi * t:(gi + 1) * t, pr * LANES:(pr + 1) * LANES]

    def stack(xa_, xb_, gi, pr):
        ca, cb = cut(xa_, gi, pr), cut(xb_, gi, pr)
        return jnp.concatenate([ca * m0, ca * m1, cb * m0, cb * m1], axis=0).astype(BF16)

    lhs = [stack(a_t, r_t, gi, pr) for gi, pr in units]
    rhs = [stack(b_t, k_t, gi, pr) for gi, pr in units]
    vst = [jnp.concatenate([cut(v, gi, pr) * m0, cut(v, gi, pr) * m1], axis=0) for gi, pr in units]
    vstb = [u.astype(BF16) for u in vst]
    gb = [_dot_nt(lhs[i], rhs[i]) for i in U]
    hts = [ht_scr[i] for i in U]
    p0 = [_dot_nt(lhs[i], hts[i].astype(BF16)) for i in U]
    a_ab = [jnp.where(strict, g_[0:n2, 0:n2], 0.0) for g_ in gb]
    a_ak = [jnp.where(strict, g_[0:n2, n2:2 * n2], 0.0).astype(BF16) for g_ in gb]
    a_r = [jnp.concatenate([jnp.where(incl, g_[n2:2 * n2, 0:n2], 0.0),
                            jnp.where(incl, g_[n2:2 * n2, n2:2 * n2], 0.0)], axis=1).astype(BF16) for g_ in gb]
    d1 = [jnp.where(blk16, a, 0.0) for a in a_ab]
    d1b = [d.astype(BF16) for d in d1]
    x = [eye + d for d in d1]
    d2b = [_dot(d, d).astype(BF16) for d in d1b]
    x = [x[i] + _dot(x[i].astype(BF16), d2b[i]) for i in U]
    d4b = [_dot(d, d).astype(BF16) for d in d2b]
    x = [x[i] + _dot(x[i].astype(BF16), d4b[i]) for i in U]
    d8b = [_dot(d, d).astype(BF16) for d in d4b]
    x = [x[i] + _dot(x[i].astype(BF16), d8b[i]) for i in U]
    e32 = [jnp.where(blk32 & not16, a, 0.0).astype(BF16) for a in a_ab]
    xb = [u.astype(BF16) for u in x]
    t1 = [_dot(e32[i], xb[i]).astype(BF16) for i in U]
    x = [x[i] + _dot(xb[i], t1[i]) for i in U]
    e64 = [jnp.where(not32, a, 0.0).astype(BF16) for a in a_ab]
    xb = [u.astype(BF16) for u in x]
    t2 = [_dot(e64[i], xb[i]).astype(BF16) for i in U]
    x = [x[i] + _dot(xb[i], t2[i]) for i in U]
    rhs_u = [p0[i][0:n2] + _dot(a_ak[i], vstb[i]) for i in U]
    ust = [_dot(x[i].astype(BF16), rhs_u[i].astype(BF16)) for i in U]
    uv = [jnp.concatenate([ust[i], vst[i]], axis=0) for i in U]
    yst = [p0[i][n2:2 * n2] + _dot(a_r[i], uv[i].astype(BF16)) for i in U]
    ys = [u[0:t] + u[t:n2] for u in yst]
    upd = [_dot(uv[i].T.astype(BF16), rhs[i]) for i in U]
    for i, (gi, pr) in enumerate(units):
        last = gi * t + t - 1
        ht_scr[i] = (hts[i] + upd[i]) * gam[last:last + 1, pr * LANES:(pr + 1) * LANES]
    y = jnp.concatenate([jnp.concatenate(ys[gi * RWKV_PAIRS:(gi + 1) * RWKV_PAIRS], axis=1)
                         for gi in range(ng)], axis=0)
    inv_n = 1.0 / HEAD_DIM
    mean = _dot_seg(y, bd) * inv_n
    yc = y - mean
    var = _dot_seg(yc * yc, bd) * inv_n
    yn = yc * lax.rsqrt(var + RWKV_GN_EPS) * lnw_ref[...] + lnb_ref[...]
    o_ref[...] = ((yn + bonus) * g).astype(o_ref.dtype).reshape(ng, t, RWKV_W)


def _rwkv(p_r, mu, w0, w2, a0, a2, g2, k_k, k_a, r_k, lnw, lnb, bd):
    bsz, s, _ = p_r.shape
    t = RWKV_CHUNK
    ng = RWKV_G if bsz % RWKV_G == 0 else 1
    vec = lambda n: pl.BlockSpec((1, n), lambda b, i: (0, 0))
    mat = lambda m, n: pl.BlockSpec((m, n), lambda b, i: (0, 0))
    return pl.pallas_call(
        functools.partial(_rwkv_kernel, ng=ng),
        grid=(bsz // ng, s // t),
        in_specs=[
            pl.BlockSpec((ng, t, RWKV_COLS_PAD), lambda b, i: (b, i, 0)),
            vec(RWKV_COLS_PAD), vec(RWKV_W), mat(LANES, RWKV_W), vec(RWKV_W), mat(LANES, RWKV_W),
            mat(LANES, RWKV_W), vec(RWKV_W), vec(RWKV_W), vec(RWKV_W), vec(RWKV_W), vec(RWKV_W),
            mat(RWKV_W, RWKV_W),
        ],
        out_specs=pl.BlockSpec((ng, t, RWKV_W), lambda b, i: (b, i, 0)),
        out_shape=jax.ShapeDtypeStruct((bsz, s, RWKV_W), BF16),
        scratch_shapes=[pltpu.VMEM((ng * RWKV_PAIRS, LANES, LANES), F32),
                        pltpu.VMEM((ng, 8, RWKV_COLS_PAD), F32)],
        compiler_params=_cparams(("arbitrary", "arbitrary")),
        name="rwkv7",
    )(p_r, mu, w0, w2, a0, a2, g2, k_k, k_a, r_k, lnw, lnb, bd)


def _sb_kernel(q_ref, k_ref, v_ref, g_ref, bd_ref, sfx_ref, o_ref, acc_scr, carry_scr, *, tq):
    tk = SB_TK
    nsub = tq // tk
    qi = pl.program_id(2)
    lane = _iota((1, LANES), 1)
    m0 = jnp.where(lane < HEAD_DIM, 1.0, 0.0).astype(BF16)
    m1 = (1.0 - m0.astype(F32)).astype(BF16)
    q = q_ref[...] * jnp.asarray(HEAD_DIM ** -0.5, BF16)
    qh = (q * m0, q * m1)
    suffix = sfx_ref[...]
    acc_scr[...] = jnp.zeros_like(acc_scr)
    carry_scr[...] = jnp.zeros_like(carry_scr)
    heads = range(2)

    def block(j, r0, diagonal):
        off = pl.multiple_of(j * tk, tk)
        kb = k_ref[pl.ds(off, tk), :]
        vb = v_ref[pl.ds(off, tk), :]
        vcat = jnp.concatenate([vb * m0, vb * m1], axis=0)
        z = [_dot_nt(qh[h][r0:tq], kb) for h in heads]
        l1p = [jnp.log(1.0 + jnp.exp(-jnp.abs(u))) for u in z]
        lk = [-(jnp.maximum(z[h], 0.0) + l1p[h]) for h in heads]
        lsig = [jnp.minimum(z[h], 0.0) - l1p[h] for h in heads]
        if diagonal:
            rows = tq - r0
            causal = _iota((rows, tk), 1) < _iota((rows, tk), 0)
            lk = [jnp.where(causal, u, 0.0) for u in lk]
        cs = [_dot(lk[h].astype(BF16), suffix) for h in heads]
        carry = [carry_scr[h, r0:tq, :] for h in heads]
        wgt = [jnp.exp(lsig[h] + cs[h][:, 0:tk] + carry[h]) for h in heads]
        if diagonal:
            wgt = [jnp.where(causal, u, 0.0) for u in wgt]
        for h in heads:
            carry_scr[h, r0:tq, :] = carry[h] + cs[h][:, tk:2 * tk]
        wcat = jnp.concatenate([u.astype(BF16) for u in wgt], axis=1)
        acc_scr[r0:tq, :] += _dot(wcat, vcat)

    for jd in reversed(range(nsub)):
        block(qi * nsub + jd, jd * tk, True)

    def alive():
        return (jnp.max(jnp.maximum(carry_scr[0], carry_scr[1])) > SB_DEAD).astype(jnp.int32)

    def cond(c):
        return (c[0] < qi * nsub) & (c[1] > 0)

    def body(c):
        block(qi * nsub - 1 - c[0], 0, False)
        return c[0] + 1, alive()

    lax.while_loop(cond, body, (jnp.int32(0), alive()))
    o = acc_scr[...]
    ms = _dot_seg(o * o, bd_ref[...]) * (1.0 / HEAD_DIM)
    o_ref[...] = (o * lax.rsqrt(ms + HEAD_NORM_EPS) * g_ref[...]).astype(o_ref.dtype)


def _sb(p_s, norm_g, bd):
    bsz, s, _ = p_s.shape
    npair = SB_W // LANES
    tq = min(SB_TQ, s)
    si = jnp.arange(SB_TK)[:, None]
    sj = jnp.arange(2 * SB_TK)[None, :]
    sfx = ((si > sj) | (sj >= SB_TK)).astype(BF16)
    return pl.pallas_call(
        functools.partial(_sb_kernel, tq=tq),
        grid=(bsz, npair, s // tq),
        in_specs=[
            pl.BlockSpec((None, tq, LANES), lambda b, h, i: (b, i, h)),
            pl.BlockSpec((None, s, LANES), lambda b, h, i: (b, 0, npair + h)),
            pl.BlockSpec((None, s, LANES), lambda b, h, i: (b, 0, 2 * npair + h)),
            pl.BlockSpec((1, LANES), lambda b, h, i: (0, h)),
            pl.BlockSpec((LANES, LANES), lambda b, h, i: (0, 0)),
            pl.BlockSpec((SB_TK, 2 * SB_TK), lambda b, h, i: (0, 0)),
        ],
        out_specs=pl.BlockSpec((None, tq, LANES), lambda b, h, i: (b, i, h)),
        out_shape=jax.ShapeDtypeStruct((bsz, s, SB_W), BF16),
        scratch_shapes=[pltpu.VMEM((tq, LANES), F32), pltpu.VMEM((2, tq, LANES), F32)],
        compiler_params=_cparams(("arbitrary", "arbitrary", "arbitrary")),
        name="stickbreak",
    )(p_s, p_s, p_s, norm_g, bd, sfx)


def _gla_kernel(p_ref, up_ref, gkb_ref, ng_ref, bdv_ref, exp_ref, o_ref, ht_scr, *, ns):
    t = GLA_CHUNK
    kw, vw = GLA_KW, GLA_VW
    nh = GLA_HEADS
    rows = ns * t
    nsub = t // GLA_SUB
    seqs = range(ns)

    @pl.when(pl.program_id(1) == 0)
    def _():
        ht_scr[...] = jnp.zeros_like(ht_scr)

    p = p_ref[...].reshape(rows, GLA_COLS_PAD)
    q = p[:, 0:kw] * (GLA_KD ** -0.5)
    k = p[:, kw:2 * kw]
    v = p[:, 2 * kw:2 * kw + vw]
    g = p[:, 2 * kw + vw:2 * kw + 2 * vw]
    gk_low = p[:, 2 * kw + 2 * vw:2 * kw + 2 * vw + LANES]
    log_a = _log_sigmoid(_dot_f32(gk_low, up_ref[...]) + gkb_ref[...]) * (1.0 / GLA_NORMALIZER)
    ti = _iota((rows, rows), 0)
    tj = _iota((rows, rows), 1)
    tri_incl = jnp.where(((ti >> 6) == (tj >> 6)) & (ti >= tj), 1.0, 0.0).astype(BF16)
    beta = _dot_exact_lhs(tri_incl, log_a)
    row = _iota((rows, 1), 0)
    seq = row >> 6
    sub = (row & (t - 1)) >> 4

    def srows(a, si):
        return a[si * t:(si + 1) * t]

    beta_last = [beta[si * t + t - 1:si * t + t, :] for si in seqs]
    hts = [ht_scr[si] for si in seqs]
    q_exp = (q * jnp.exp(beta)).astype(BF16)
    o_inter = [_dot_nt(srows(q_exp, si), hts[si].astype(BF16)) for si in seqs]

    ref_rows = [[beta[si * t + GLA_SUB * i - 1:si * t + GLA_SUB * i, :] for i in range(1, nsub)] for si in seqs]
    beta_ref = jnp.zeros_like(beta)
    for si in seqs:
        for i in range(1, nsub):
            beta_ref = jnp.where((seq == si) & (sub == i), ref_rows[si][i - 1], beta_ref)
    q_hat = q * jnp.exp(jnp.minimum(beta - beta_ref, 0.0))
    lane_k = _iota((1, kw), 1)
    lane_v = _iota((1, vw), 1)
    mk = [jnp.where((lane_k >> 5) == h, 1.0, 0.0) for h in range(nh)]
    mv = [jnp.where((lane_v >> 6) == h, 1.0, 0.0) for h in range(nh)]
    q_st = [jnp.concatenate([srows(q_hat, si) * mk[h] for h in range(nh)], axis=0).astype(BF16) for si in seqs]
    v_st = [jnp.concatenate([srows(v, si) * mv[h] for h in range(nh)], axis=0).astype(BF16) for si in seqs]
    n4 = nh * t
    ri = _iota((n4, n4), 0)
    ci = _iota((n4, n4), 1)
    rsub = (ri & (t - 1)) >> 4
    ct = ci & (t - 1)
    attn = [jnp.zeros((n4, n4), F32) for _ in seqs]
    for i in range(1, nsub):
        k_hat = [srows(k, si) * jnp.exp(jnp.minimum(ref_rows[si][i - 1] - srows(beta, si), 0.0)) for si in seqs]
        k_st = [jnp.concatenate([k_hat[si] * mk[h] for h in range(nh)], axis=0).astype(BF16) for si in seqs]
        gi = [_dot_nt(q_st[si], k_st[si]) for si in seqs]
        sel = (rsub == i) & (ct < GLA_SUB * i)
        attn = [jnp.where(sel, gi[si], attn[si]) for si in seqs]
    o_st = [_dot(attn[si].astype(BF16), v_st[si]) for si in seqs]
    o_seq = []
    for si in seqs:
        acc = o_inter[si]
        for h in range(nh):
            acc = acc + o_st[si][h * t:(h + 1) * t]
        o_seq.append(acc)
    o = jnp.concatenate(o_seq, axis=0)

    expand = exp_ref[...]
    tsub = row & (GLA_SUB - 1)
    for d in range(GLA_SUB):
        if d == 0:
            kd, bd_, vd = k, beta, v
        else:
            kd = pltpu.roll(k, d, axis=0)
            bd_ = pltpu.roll(beta, d, axis=0)
            vd = pltpu.roll(v, d, axis=0)
        term = jnp.where(tsub >= d, q * kd * jnp.exp(jnp.minimum(beta - bd_, 0.0)), 0.0)
        o = o + _dot(term.astype(BF16), expand) * vd

    hv = _iota((vw, kw), 0) >> 6
    hk = _iota((vw, kw), 1) >> 5
    k_end = [(srows(k, si) * jnp.exp(beta_last[si] - srows(beta, si))).astype(BF16) for si in seqs]
    upd = [_dot(srows(v, si).T.astype(BF16), k_end[si]) for si in seqs]
    for si in seqs:
        ht_scr[si] = hts[si] * jnp.exp(beta_last[si]) + jnp.where(hv == hk, upd[si], 0.0)

    ms = _dot_seg(o * o, bdv_ref[...]) * (1.0 / HEAD_DIM)
    on = o * lax.rsqrt(ms + HEAD_NORM_EPS) * ng_ref[...]
    o_ref[...] = (on * (g * _sigmoid(g))).astype(o_ref.dtype).reshape(ns, t, vw)


def _gla(p_g, gk_up, gk_b, norm_g, bdv, expand):
    bsz, s, _ = p_g.shape
    t = GLA_CHUNK
    ns = GLA_G if bsz % GLA_G == 0 else 1
    const = lambda m, n: pl.BlockSpec((m, n), lambda b, i: (0, 0))
    return pl.pallas_call(
        functools.partial(_gla_kernel, ns=ns),
        grid=(bsz // ns, s // t),
        in_specs=[
            pl.BlockSpec((ns, t, GLA_COLS_PAD), lambda b, i: (b, i, 0)),
            const(LANES, GLA_KW), const(1, GLA_KW), const(1, GLA_VW), const(GLA_VW, GLA_VW),
            const(GLA_KW, GLA_VW),
        ],
        out_specs=pl.BlockSpec((ns, t, GLA_VW), lambda b, i: (b, i, 0)),
        out_shape=jax.ShapeDtypeStruct((bsz, s, GLA_VW), BF16),
        scratch_shapes=[pltpu.VMEM((ns, GLA_VW, GLA_KW), F32)],
        compiler_params=_cparams(("arbitrary", "arbitrary")),
        name="gla",
    )(p_g, gk_up, gk_b, norm_g, bdv, expand)


def _outproj_kernel(yr_ref, ys_ref, yg_ref, x_ref, mod_ref, g_ref, w_ref, wr_ref, x1_ref, h_ref, lg_ref):
    mod = mod_ref[...]
    o1 = RWKV_W
    o2 = o1 + SB_W
    mix = (_dot(yr_ref[...], w_ref[0:o1, :]) + _dot(ys_ref[...], w_ref[o1:o2, :])
           + _dot(yg_ref[...], w_ref[o2:, :]))
    x1 = x_ref[...] + mod[2:3] * mix
    x1_ref[...] = x1
    h = _rms_mod(x1, g_ref[...], mod[4:5], mod[3:4])
    h_ref[...] = h.astype(BF16)
    lg_ref[...] = _dot_nt_f32(wr_ref[...], h)


def _outproj(y_r, y_s, y_g, x, mod_l, g, w_out, w_router_t):
    bsz, s, d = x.shape
    tm = min(512, s)
    nt = s // tm
    return pl.pallas_call(
        _outproj_kernel,
        grid=(bsz, nt),
        in_specs=[
            pl.BlockSpec((None, tm, RWKV_W), lambda b, i: (b, i, 0)),
            pl.BlockSpec((None, tm, SB_W), lambda b, i: (b, i, 0)),
            pl.BlockSpec((None, tm, GLA_VW), lambda b, i: (b, i, 0)),
            pl.BlockSpec((None, tm, d), lambda b, i: (b, i, 0)),
            pl.BlockSpec((None, 6, d), lambda b, i: (b, 0, 0)),
            pl.BlockSpec((1, d), lambda b, i: (0, 0)),
            pl.BlockSpec((d, d), lambda b, i: (0, 0)),
            pl.BlockSpec((N_EXPERTS, d), lambda b, i: (0, 0)),
        ],
        out_specs=[
            pl.BlockSpec((None, tm, d), lambda b, i: (b, i, 0)),
            pl.BlockSpec((None, tm, d), lambda b, i: (b, i, 0)),
            pl.BlockSpec((N_EXPERTS, tm), lambda b, i: (0, b * nt + i)),
        ],
        out_shape=[
            jax.ShapeDtypeStruct((bsz, s, d), F32),
            jax.ShapeDtypeStruct((bsz, s, d), BF16),
            jax.ShapeDtypeStruct((N_EXPERTS, bsz * s), F32),
        ],
        compiler_params=_cparams(("arbitrary", "arbitrary")),
        name="outproj",
    )(y_r, y_s, y_g, x, mod_l, g, w_out, w_router_t)


def _route_kernel(lg_ref, bias_ref, tri_ref, w_ref, rank_ref, cnt_ref):
    aff = _sigmoid(lg_ref[...])
    sel = aff + bias_ref[...]
    e = EXPERTS_PER_GROUP
    rows = [sel[i:i + 1, :] for i in range(N_EXPERTS)]
    arow = [aff[i:i + 1, :] for i in range(N_EXPERTS)]
    scores = []
    for gi in range(N_GROUPS):
        a, b, c, d = rows[e * gi:e * gi + e]
        scores.append(jnp.maximum(jnp.maximum(jnp.maximum(a + b, a + c), jnp.maximum(a + d, b + c)),
                                  jnp.maximum(b + d, c + d)))
    grp = jnp.zeros_like(scores[0]).astype(jnp.int32)
    best = scores[0]
    for gi in range(1, N_GROUPS):
        better = scores[gi] > best
        grp = jnp.where(better, gi, grp)
        best = jnp.where(better, scores[gi], best)
    sin, ain = [], []
    for j in range(e):
        sv, av = rows[j], arow[j]
        for gi in range(1, N_GROUPS):
            sv = jnp.where(grp == gi, rows[e * gi + j], sv)
            av = jnp.where(grp == gi, arow[e * gi + j], av)
        sin.append(sv)
        ain.append(av)
    loc1 = jnp.zeros_like(grp)
    b1 = sin[0]
    for j in range(1, e):
        better = sin[j] > b1
        loc1 = jnp.where(better, j, loc1)
        b1 = jnp.where(better, sin[j], b1)
    neg = jnp.full_like(b1, -jnp.inf)
    loc2 = jnp.zeros_like(grp)
    b2 = neg
    for j in range(e):
        cand = jnp.where(loc1 == j, neg, sin[j])
        better = cand > b2
        loc2 = jnp.where(better, j, loc2)
        b2 = jnp.where(better, cand, b2)
    a1 = ain[0]
    a2 = ain[0]
    for j in range(1, e):
        a1 = jnp.where(loc1 == j, ain[j], a1)
        a2 = jnp.where(loc2 == j, ain[j], a2)
    den = a1 + a2
    e1 = grp * e + loc1
    e2 = grp * e + loc2
    eid = _iota(aff.shape, 0)
    is1 = eid == e1
    is2 = eid == e2
    w_ref[...] = jnp.where(is1, a1 / den, jnp.where(is2, a2 / den, 0.0))
    selected = is1 | is2
    self = jnp.where(selected, 1.0, 0.0)
    excl = _dot(self.astype(BF16), tri_ref[...])
    rank_ref[...] = jnp.where(selected, excl, -1.0)
    cnt = jnp.sum(self, axis=1, keepdims=True)
    cnt_ref[...] = jnp.broadcast_to(cnt, cnt_ref.shape)


def _route(logits_t, bias, tri):
    ne, n = logits_t.shape
    tm = tri.shape[0]
    nt = n // tm
    return pl.pallas_call(
        _route_kernel,
        grid=(nt,),
        in_specs=[
            pl.BlockSpec((ne, tm), lambda i: (0, i)),
            pl.BlockSpec((ne, 1), lambda i: (0, 0)),
            pl.BlockSpec((tm, tm), lambda i: (0, 0)),
        ],
        out_specs=[
            pl.BlockSpec((ne, tm), lambda i: (0, i)),
            pl.BlockSpec((ne, tm), lambda i: (0, i)),
            pl.BlockSpec((None, ne, LANES), lambda i: (i, 0, 0)),
        ],
        out_shape=[
            jax.ShapeDtypeStruct((ne, n), F32),
            jax.ShapeDtypeStruct((ne, n), F32),
            jax.ShapeDtypeStruct((nt, ne, LANES), F32),
        ],
        compiler_params=_cparams(("arbitrary",)),
        name="route",
    )(logits_t, bias, tri)


def _moe_kernel(cnt_ref, h_ref, rank_ref, w_ref, wg_ref, wu_ref, wd_ref, x1_ref, mod_ref, o_ref,
                acc_scr, obuf, ybuf, pend):
    i = pl.program_id(0)
    e = pl.program_id(1)
    r = MOE_R

    @pl.when(e == 0)
    def _():
        acc_scr[...] = jnp.zeros_like(acc_scr)
        pend[0] = 0

    def scatter(nrows):
        acc_scr[...] += lax.dot_general(obuf[0:nrows, :], ybuf[0:nrows, :], (((0,), (0,)), ((), ())),
                                        preferred_element_type=F32)

    cnt = cnt_ref[i * N_EXPERTS + e]
    rank = rank_ref[...].astype(jnp.int32)
    wrow = w_ref[...]
    w_hi, w_lo = _split2(wrow)
    tmn = rank.shape[1]
    arow = _iota((LANES, tmn), 0)
    waux = jnp.where(arow == 0, w_hi.astype(F32),
                     jnp.where(arow == 1, w_lo.astype(F32), 0.0)).astype(BF16)
    riota = _iota((r, 1), 0)

    def chunk(ch, carry):
        rel = rank - ch * r
        onehot = jnp.where(rel == riota, 1.0, 0.0).astype(BF16)
        xr = _dot(onehot, h_ref[...]).astype(BF16)
        gate = _dot(xr, wg_ref[...])
        up = _dot(xr, wu_ref[...])
        act = (gate * _sigmoid(gate) * up).astype(BF16)
        y = _dot(act, wd_ref[...])
        wr = _dot_nt(onehot, waux)
        wcol = wr[:, 0:1] + wr[:, 1:2]
        slot = pend[0]
        off = pl.multiple_of(slot * r, r)
        obuf[pl.ds(off, r), :] = onehot
        ybuf[pl.ds(off, r), :] = (y * wcol).astype(BF16)

        @pl.when(slot == 1)
        def _():
            scatter(2 * r)

        pend[0] = 1 - slot
        return carry

    lax.fori_loop(0, (cnt + (r - 1)) >> (r.bit_length() - 1), chunk, 0)

    @pl.when(e == N_EXPERTS - 1)
    def _():
        @pl.when(pend[0] == 1)
        def _():
            scatter(r)

        o_ref[...] = x1_ref[...] + mod_ref[5:6, :] * acc_scr[...]


def _moe(cnt, h, rank_t, w_t, w_gate, w_up, w_down, x1, mod_l, tm, tiles_per_batch):
    n, d = h.shape
    de = w_gate.shape[-1]
    nt = n // tm
    grid_spec = pltpu.PrefetchScalarGridSpec(
        num_scalar_prefetch=1,
        grid=(nt, N_EXPERTS),
        in_specs=[
            pl.BlockSpec((tm, d), lambda i, e, c: (i, 0)),
            pl.BlockSpec((None, 1, tm), lambda i, e, c: (e, 0, i)),
            pl.BlockSpec((None, 1, tm), lambda i, e, c: (e, 0, i)),
            pl.BlockSpec((None, d, de), lambda i, e, c: (e, 0, 0)),
            pl.BlockSpec((None, d, de), lambda i, e, c: (e, 0, 0)),
            pl.BlockSpec((None, de, d), lambda i, e, c: (e, 0, 0)),
            pl.BlockSpec((tm, d), lambda i, e, c: (i, 0)),
            pl.BlockSpec((None, 6, d), lambda i, e, c: (i // tiles_per_batch, 0, 0)),
        ],
        out_specs=pl.BlockSpec((tm, d), lambda i, e, c: (i, 0)),
        scratch_shapes=[pltpu.VMEM((tm, d), F32), pltpu.VMEM((2 * MOE_R, tm), BF16),
                        pltpu.VMEM((2 * MOE_R, d), BF16), pltpu.SMEM((1,), jnp.int32)],
    )
    return pl.pallas_call(
        _moe_kernel,
        grid_spec=grid_spec,
        out_shape=jax.ShapeDtypeStruct((n, d), F32),
        compiler_params=_cparams(("arbitrary", "arbitrary")),
        name="moe",
    )(cnt, h, rank_t.reshape(N_EXPERTS, 1, n), w_t.reshape(N_EXPERTS, 1, n), w_gate, w_up, w_down, x1, mod_l)


def _final_kernel(x_ref, g_ref, o_ref):
    x = x_ref[...]
    ms = jnp.mean(x * x, axis=-1, keepdims=True)
    o_ref[...] = x * lax.rsqrt(ms + RMS_EPS) * g_ref[...]


def _final_norm(x, g):
    n, d = x.shape
    tm = min(1024, n)
    return pl.pallas_call(
        _final_kernel,
        grid=(n // tm,),
        in_specs=[pl.BlockSpec((tm, d), lambda i: (i, 0)), pl.BlockSpec((1, d), lambda i: (0, 0))],
        out_specs=pl.BlockSpec((tm, d), lambda i: (i, 0)),
        out_shape=jax.ShapeDtypeStruct((n, d), F32),
        compiler_params=_cparams(("arbitrary",)),
        name="final_norm",
    )(x, g)


def _pad_cols(w, n):
    return jnp.pad(w, [(0, 0)] * (w.ndim - 1) + [(0, n - w.shape[-1])])


def _pad_rows(w, n):
    return jnp.pad(w, [(0, 0)] * (w.ndim - 2) + [(0, n - w.shape[-2]), (0, 0)])


def _layout_in_cols(w):
    rw = RWKV_W
    o = 3 * rw
    parts = [w[..., 0:o], _pad_cols(w[..., o:o + 64], LANES), _pad_cols(w[..., o + 64:o + 128], LANES),
             w[..., o + 128:o + 256]]
    o += 256
    parts.append(w[..., o:o + SB_COLS])
    o += SB_COLS
    parts.append(_pad_cols(w[..., o:], GLA_COLS_PAD))
    return jnp.concatenate(parts, axis=-1)


def kernel(x, c, rms_mix_g, rms_ffn_g, w_mod, b_mod, w_in, w_out, rwkv_mu, rwkv_w0, rwkv_w2, rwkv_a0, rwkv_a2, rwkv_g2, rwkv_k_k, rwkv_k_a, rwkv_r_k, rwkv_lnx_w, rwkv_lnx_b, sb_norm_g, gla_gk_up, gla_gk_b, gla_norm_g, w_router, router_bias, w_gate, w_up, w_down, final_g):
    bsz, s, d = x.shape
    depth = w_in.shape[0]
    n = bsz * s

    mod = _modulation(c, w_mod, b_mod).reshape(depth, bsz, 6, d)
    w_in_l = _layout_in_cols(w_in).astype(BF16)
    w_out_b = w_out.astype(BF16)
    mu_l = _layout_in_cols(jnp.pad(rwkv_mu, ((0, 0), (0, w_in.shape[-1] - rwkv_mu.shape[-1]))))[:, :RWKV_COLS_PAD]
    w2_p = _pad_rows(rwkv_w2, LANES)
    a2_p = _pad_rows(rwkv_a2, LANES)
    up_p = _pad_rows(gla_gk_up, LANES)
    gla_ng = jnp.tile(gla_norm_g, (1, GLA_HEADS))
    bd_rwkv = _block_diag_const(RWKV_W, HEAD_DIM, 1.0)
    bd_sb = _block_diag_const(LANES, HEAD_DIM, 1.0)
    bd_gla = _block_diag_const(GLA_VW, HEAD_DIM, 1.0)
    expand = (jnp.arange(GLA_KW)[:, None] // GLA_KD == jnp.arange(GLA_VW)[None, :] // HEAD_DIM).astype(BF16)
    tm_moe = min(MOE_TM, s)
    tri = (jnp.arange(tm_moe)[:, None] < jnp.arange(tm_moe)[None, :]).astype(BF16)
    w_router_t = w_router.T
    bias_col = router_bias.reshape(N_EXPERTS, 1)
    wg_b, wu_b, wd_b = w_gate.astype(BF16), w_up.astype(BF16), w_down.astype(BF16)
    row = lambda a: a.reshape(1, -1)

    for l in range(depth):
        p_r, p_s, p_g = _inproj(x, mod[l], row(rms_mix_g[l]), w_in_l[l])
        y_r = _rwkv(p_r, row(mu_l[l]), row(rwkv_w0[l]), w2_p[l], row(rwkv_a0[l]), a2_p[l], rwkv_g2[l],
                    row(rwkv_k_k[l]), row(rwkv_k_a[l]), row(rwkv_r_k[l]), row(rwkv_lnx_w[l]),
                    row(rwkv_lnx_b[l]), bd_rwkv)
        y_s = _sb(p_s, row(sb_norm_g[l]), bd_sb)
        y_g = _gla(p_g, up_p[l], row(gla_gk_b[l]), row(gla_ng[l]), bd_gla, expand)
        x1, h2, logits_t = _outproj(y_r, y_s, y_g, x, mod[l], row(rms_ffn_g[l]), w_out_b[l], w_router_t)
        w_t, rank_t, cnt = _route(logits_t, bias_col, tri)
        cnt_i = cnt[:, :, 0].astype(jnp.int32).reshape(-1)
        x = _moe(cnt_i, h2.reshape(n, d), rank_t, w_t, wg_b[l], wu_b[l], wd_b[l], x1.reshape(n, d),
                 mod[l], tm_moe, s // tm_moe).reshape(bsz, s, d)
    return _final_norm(x.reshape(n, d), row(final_g)).reshape(bsz, s, d)
```

```python
import functools

import jax
import jax.numpy as jnp
from jax import lax
from jax.experimental import pallas as pl
from jax.experimental.pallas import tpu as pltpu

F32 = jnp.float32
BF16 = jnp.bfloat16

LANES = 128
HEAD_DIM = 64
RWKV_W = 512
RWKV_PAIRS = RWKV_W // LANES
RWKV_COLS_PAD = 3 * RWKV_W + 3 * LANES
RWKV_CHUNK = 64
RWKV_G = 4
RWKV_GN_EPS = 64e-5
SB_W = 256
SB_COLS = 3 * SB_W
SB_TQ = 1024
SB_TK = 128
SB_DEAD = -104.0
GLA_VW = 256
GLA_KW = 128
GLA_HEADS = 4
GLA_KD = 32
GLA_LORA = 16
GLA_COLS_PAD = 2 * GLA_KW + 2 * GLA_VW + LANES
GLA_CHUNK = 64
GLA_SUB = 16
GLA_G = 4
GLA_NORMALIZER = 16.0
IN_COLS_PAD = RWKV_COLS_PAD + SB_COLS + GLA_COLS_PAD
HEAD_NORM_EPS = 1e-5
RMS_EPS = 1e-6
N_EXPERTS = 16
EXPERTS_PER_GROUP = 4
N_GROUPS = 4
MOE_TM = 1024
MOE_R = 128
VMEM_LIMIT = 48 * 1024 * 1024


def _dot(a, b):
    return jnp.dot(a, b, preferred_element_type=F32)


def _dot_nt(a, b):
    return lax.dot_general(a, b, (((1,), (1,)), ((), ())), preferred_element_type=F32)


def _split2(x):
    hi = x.astype(BF16)
    lo = (x - hi.astype(F32)).astype(BF16)
    return hi, lo


def _split3(x):
    hi = x.astype(BF16)
    r = x - hi.astype(F32)
    mid = r.astype(BF16)
    lo = (r - mid.astype(F32)).astype(BF16)
    return hi, mid, lo


def _dot_seg(x, m):
    hi, lo = _split2(x)
    return _dot(hi, m) + _dot(lo, m)


def _dot_exact_lhs(m, x):
    hi, mid, lo = _split3(x)
    return _dot(m, hi) + _dot(m, mid) + _dot(m, lo)


def _dot_f32(a, b):
    ah, al = _split2(a)
    bh, bl = _split2(b)
    return _dot(ah, bh) + _dot(ah, bl) + _dot(al, bh)


def _dot_nt_f32(a, b):
    ah, al = _split2(a)
    bh, bl = _split2(b)
    return _dot_nt(ah, bh) + _dot_nt(ah, bl) + _dot_nt(al, bh)


def _softplus(x):
    return jnp.maximum(x, 0.0) + jnp.log(1.0 + jnp.exp(-jnp.abs(x)))


def _log_sigmoid(x):
    return jnp.minimum(x, 0.0) - jnp.log(1.0 + jnp.exp(-jnp.abs(x)))


def _sigmoid(x):
    return 1.0 / (1.0 + jnp.exp(-x))


def _iota(shape, dim):
    return lax.broadcasted_iota(jnp.int32, shape, dim)


def _block_diag_const(n, blk, val):
    i = jnp.arange(n)
    return jnp.where((i[:, None] // blk) == (i[None, :] // blk), val, 0.0).astype(BF16)


def _cparams(sem):
    return pltpu.CompilerParams(dimension_semantics=sem, vmem_limit_bytes=VMEM_LIMIT)


def _mod_kernel(c_ref, w_ref, b_ref, o_ref):
    c = c_ref[...]
    ca = c * _sigmoid(c)
    o_ref[...] = _dot_f32(ca, w_ref[...]) + b_ref[...]


def _modulation(c, w_mod, b_mod):
    depth, d, six_d = w_mod.shape
    bsz = c.shape[0]
    tn = 1536
    return pl.pallas_call(
        _mod_kernel,
        grid=(depth, six_d // tn),
        in_specs=[
            pl.BlockSpec((bsz, d), lambda l, j: (0, 0)),
            pl.BlockSpec((None, d, tn), lambda l, j: (l, 0, j)),
            pl.BlockSpec((None, 1, tn), lambda l, j: (l, 0, j)),
        ],
        out_specs=pl.BlockSpec((None, bsz, tn), lambda l, j: (l, 0, j)),
        out_shape=jax.ShapeDtypeStruct((depth, bsz, six_d), F32),
        compiler_params=_cparams(("arbitrary", "arbitrary")),
        name="adaln_mod",
    )(c, w_mod, b_mod.reshape(depth, 1, six_d))


def _rms_mod(x, g, scale, shift):
    ms = jnp.mean(x * x, axis=-1, keepdims=True)
    return x * lax.rsqrt(ms + RMS_EPS) * g * (1.0 + scale) + shift


def _inproj_kernel(x_ref, mod_ref, g_ref, w_ref, pr_ref, ps_ref, pg_ref):
    mod = mod_ref[...]
    h = _rms_mod(x_ref[...], g_ref[...], mod[1:2], mod[0:1]).astype(BF16)
    o1 = RWKV_COLS_PAD
    o2 = o1 + SB_COLS
    pr_ref[...] = _dot(h, w_ref[:, 0:o1])
    ps_ref[...] = _dot(h, w_ref[:, o1:o2]).astype(BF16)
    pg_ref[...] = _dot(h, w_ref[:, o2:IN_COLS_PAD])


def _inproj(x, mod_l, g, w):
    bsz, s, d = x.shape
    tm = min(512, s)
    return pl.pallas_call(
        _inproj_kernel,
        grid=(bsz, s // tm),
        in_specs=[
            pl.BlockSpec((None, tm, d), lambda b, i: (b, i, 0)),
            pl.BlockSpec((None, 6, d), lambda b, i: (b, 0, 0)),
            pl.BlockSpec((1, d), lambda b, i: (0, 0)),
            pl.BlockSpec((d, IN_COLS_PAD), lambda b, i: (0, 0)),
        ],
        out_specs=[
            pl.BlockSpec((None, tm, RWKV_COLS_PAD), lambda b, i: (b, i, 0)),
            pl.BlockSpec((None, tm, SB_COLS), lambda b, i: (b, i, 0)),
            pl.BlockSpec((None, tm, GLA_COLS_PAD), lambda b, i: (b, i, 0)),
        ],
        out_shape=[
            jax.ShapeDtypeStruct((bsz, s, RWKV_COLS_PAD), F32),
            jax.ShapeDtypeStruct((bsz, s, SB_COLS), BF16),
            jax.ShapeDtypeStruct((bsz, s, GLA_COLS_PAD), F32),
        ],
        compiler_params=_cparams(("arbitrary", "arbitrary")),
        name="inproj",
    )(x, mod_l, g, w)


def _rwkv_kernel(p_ref, mu_ref, w0_ref, w2_ref, a0_ref, a2_ref, g2_ref, kk_ref, ka_ref, rk_ref,
                 lnw_ref, lnb_ref, bd_ref, o_ref, ht_scr, prev_scr, *, ng):
    t = RWKV_CHUNK
    rows = ng * t

    @pl.when(pl.program_id(1) == 0)
    def _():
        ht_scr[...] = jnp.zeros_like(ht_scr)
        prev_scr[...] = jnp.zeros_like(prev_scr)

    p = p_ref[...].reshape(rows, RWKV_COLS_PAD)
    row = _iota((rows, 1), 0)
    prev = pltpu.roll(p, 1, axis=0)
    for gi in range(ng):
        prev = jnp.where(row == gi * t, prev_scr[gi, 0:1, :], prev)
        prev_scr[gi, 0:1, :] = p[gi * t + t - 1:gi * t + t, :]
    xm = p + (prev - p) * mu_ref[...]
    w = RWKV_W
    r = xm[:, 0:w]
    k = xm[:, w:2 * w]
    v = xm[:, 2 * w:3 * w]
    xw = xm[:, 3 * w:3 * w + LANES]
    xa = xm[:, 3 * w + LANES:3 * w + 2 * LANES]
    xg = xm[:, 3 * w + 2 * LANES:3 * w + 3 * LANES]
    w_log = -_softplus(-(w0_ref[...] + _dot_f32(jnp.tanh(xw), w2_ref[...]))) - 0.5
    lw = -jnp.exp(w_log)
    iclr = _sigmoid(a0_ref[...] + _dot_f32(xa, a2_ref[...]))
    g = _dot_f32(_sigmoid(xg), g2_ref[...])
    bd = bd_ref[...]
    kkr = k * kk_ref[...]
    ss = _dot_seg(kkr * kkr, bd)
    kk = kkr * lax.rsqrt(jnp.maximum(ss, 1e-24))
    k2 = k * (1.0 + (iclr - 1.0) * ka_ref[...])
    bonus = _dot_seg(r * k2 * rk_ref[...], bd) * v

    ti = _iota((rows, rows), 0)
    tj = _iota((rows, rows), 1)
    tri_incl = jnp.where(((ti >> 6) == (tj >> 6)) & (ti >= tj), 1.0, 0.0).astype(BF16)
    beta = _dot_exact_lhs(tri_incl, lw)
    gam = jnp.exp(beta)
    gam_inv = jnp.exp(-beta)
    a_t = -kk * jnp.exp(beta - lw)
    r_t = r * gam
    b_t = kk * iclr * gam_inv
    k_t = k2 * gam_inv

    lane = _iota((1, LANES), 1)
    m0 = jnp.where(lane < HEAD_DIM, 1.0, 0.0)
    m1 = 1.0 - m0
    n2 = 2 * t
    ii = _iota((n2, n2), 0)
    jj = _iota((n2, n2), 1)
    it = ii & (t - 1)
    jt = jj & (t - 1)
    same64 = (ii >> 6) == (jj >> 6)
    strict = same64 & (it > jt)
    incl = same64 & (it >= jt)
    blk16 = (ii >> 4) == (jj >> 4)
    blk32 = (ii >> 5) == (jj >> 5)
    not16 = jnp.logical_not(blk16)
    not32 = jnp.logical_not(blk32)
    eye = jnp.where(ii == jj, 1.0, 0.0)

    units = [(gi, pr) for gi in range(ng) for pr in range(RWKV_PAIRS)]
    U = range(len(units))

    def cut(arr, gi, pr):
        return arr[gi * t:(gi + 1) * t, pr * LANES:(pr + 1) * LANES]

    def stack(xa_, xb_, gi, pr):
        ca, cb = cut(xa_, gi, pr), cut(xb_, gi, pr)
        return jnp.concatenate([ca * m0, ca * m1, cb * m0, cb * m1], axis=0).astype(BF16)

    lhs = [stack(a_t, r_t, gi, pr) for gi, pr in units]
    rhs = [stack(b_t, k_t, gi, pr) for gi, pr in units]
    vst = [jnp.concatenate([cut(v, gi, pr) * m0, cut(v, gi, pr) * m1], axis=0) for gi, pr in units]
    vstb = [u.astype(BF16) for u in vst]
    gb = [_dot_nt(lhs[i], rhs[i]) for i in U]
    hts = [ht_scr[i] for i in U]
    p0 = [_dot_nt(lhs[i], hts[i].astype(BF16)) for i in U]
    a_ab = [jnp.where(strict, g_[0:n2, 0:n2], 0.0) for g_ in gb]
    a_ak = [jnp.where(strict, g_[0:n2, n2:2 * n2], 0.0).astype(BF16) for g_ in gb]
    a_r = [jnp.concatenate([jnp.where(incl, g_[n2:2 * n2, 0:n2], 0.0),
                            jnp.where(incl, g_[n2:2 * n2, n2:2 * n2], 0.0)], axis=1).astype(BF16) for g_ in gb]
    d1 = [jnp.where(blk16, a, 0.0) for a in a_ab]
    d1b = [d.astype(BF16) for d in d1]
    x = [eye + d for d in d1]
    d2b = [_dot(d, d).astype(BF16) for d in d1b]
    x = [x[i] + _dot(x[i].astype(BF16), d2b[i]) for i in U]
    d4b = [_dot(d, d).astype(BF16) for d in d2b]
    x = [x[i] + _dot(x[i].astype(BF16), d4b[i]) for i in U]
    d8b = [_dot(d, d).astype(BF16) for d in d4b]
    x = [x[i] + _dot(x[i].astype(BF16), d8b[i]) for i in U]
    e32 = [jnp.where(blk32 & not16, a, 0.0).astype(BF16) for a in a_ab]
    xb = [u.astype(BF16) for u in x]
    t1 = [_dot(e32[i], xb[i]).astype(BF16) for i in U]
    x = [x[i] + _dot(xb[i], t1[i]) for i in U]
    e64 = [jnp.where(not32, a, 0.0).astype(BF16) for a in a_ab]
    xb = [u.astype(BF16) for u in x]
    t2 = [_dot(e64[i], xb[i]).astype(BF16) for i in U]
    x = [x[i] + _dot(xb[i], t2[i]) for i in U]
    rhs_u = [p0[i][0:n2] + _dot(a_ak[i], vstb[i]) for i in U]
    ust = [_dot(x[i].astype(BF16), rhs_u[i].astype(BF16)) for i in U]
    uv = [jnp.concatenate([ust[i], vst[i]], axis=0) for i in U]
    yst = [p0[i][n2:2 * n2] + _dot(a_r[i], uv[i].astype(BF16)) for i in U]
    ys = [u[0:t] + u[t:n2] for u in yst]
    upd = [_dot(uv[i].T.astype(BF16), rhs[i]) for i in U]
    for i, (gi, pr) in enumerate(units):
        last = gi * t + t - 1
        ht_scr[i] = (hts[i] + upd[i]) * gam[last:last + 1, pr * LANES:(pr + 1) * LANES]
    y = jnp.concatenate([jnp.concatenate(ys[gi * RWKV_PAIRS:(gi + 1) * RWKV_PAIRS], axis=1)
                         for gi in range(ng)], axis=0)
    inv_n = 1.0 / HEAD_DIM
    mean = _dot_seg(y, bd) * inv_n
    yc = y - mean
    var = _dot_seg(yc * yc, bd) * inv_n
    yn = yc * lax.rsqrt(var + RWKV_GN_EPS) * lnw_ref[...] + lnb_ref[...]
    o_ref[...] = ((yn + bonus) * g).astype(o_ref.dtype).reshape(ng, t, RWKV_W)


def _rwkv(p_r, mu, w0, w2, a0, a2, g2, k_k, k_a, r_k, lnw, lnb, bd):
    bsz, s, _ = p_r.shape
    t = RWKV_CHUNK
    ng = RWKV_G if bsz % RWKV_G == 0 else 1
    vec = lambda n: pl.BlockSpec((1, n), lambda b, i: (0, 0))
    mat = lambda m, n: pl.BlockSpec((m, n), lambda b, i: (0, 0))
    return pl.pallas_call(
        functools.partial(_rwkv_kernel, ng=ng),
        grid=(bsz // ng, s // t),
        in_specs=[
            pl.BlockSpec((ng, t, RWKV_COLS_PAD), lambda b, i: (b, i, 0)),
            vec(RWKV_COLS_PAD), vec(RWKV_W), mat(LANES, RWKV_W), vec(RWKV_W), mat(LANES, RWKV_W),
            mat(LANES, RWKV_W), vec(RWKV_W), vec(RWKV_W), vec(RWKV_W), vec(RWKV_W), vec(RWKV_W),
            mat(RWKV_W, RWKV_W),
        ],
        out_specs=pl.BlockSpec((ng, t, RWKV_W), lambda b, i: (b, i, 0)),
        out_shape=jax.ShapeDtypeStruct((bsz, s, RWKV_W), BF16),
        scratch_shapes=[pltpu.VMEM((ng * RWKV_PAIRS, LANES, LANES), F32),
                        pltpu.VMEM((ng, 8, RWKV_COLS_PAD), F32)],
        compiler_params=_cparams(("arbitrary", "arbitrary")),
        name="rwkv7",
    )(p_r, mu, w0, w2, a0, a2, g2, k_k, k_a, r_k, lnw, lnb, bd)


def _sb_kernel(q_ref, k_ref, v_ref, g_ref, bd_ref, sfx_ref, o_ref, acc_scr, carry_scr, *, tq):
    tk = SB_TK
    nsub = tq // tk
    qi = pl.program_id(2)
    lane = _iota((1, LANES), 1)
    m0 = jnp.where(lane < HEAD_DIM, 1.0, 0.0).astype(BF16)
    m1 = (1.0 - m0.astype(F32)).astype(BF16)
    q = q_ref[...] * jnp.asarray(HEAD_DIM ** -0.5, BF16)
    qh = (q * m0, q * m1)
    suffix = sfx_ref[...]
    acc_scr[...] = jnp.zeros_like(acc_scr)
    carry_scr[...] = jnp.zeros_like(carry_scr)
    heads = range(2)

    def block(j, r0, diagonal):
        off = pl.multiple_of(j * tk, tk)
        kb = k_ref[pl.ds(off, tk), :]
        vb = v_ref[pl.ds(off, tk), :]
        vcat = jnp.concatenate([vb * m0, vb * m1], axis=0)
        z = [_dot_nt(qh[h][r0:tq], kb) for h in heads]
        l1p = [jnp.log(1.0 + jnp.exp(-jnp.abs(u))) for u in z]
        lk = [-(jnp.maximum(z[h], 0.0) + l1p[h]) for h in heads]
        lsig = [jnp.minimum(z[h], 0.0) - l1p[h] for h in heads]
        if diagonal:
            rows = tq - r0
            causal = _iota((rows, tk), 1) < _iota((rows, tk), 0)
            lk = [jnp.where(causal, u, 0.0) for u in lk]
        cs = [_dot(lk[h].astype(BF16), suffix) for h in heads]
        carry = [carry_scr[h, r0:tq, :] for h in heads]
        wgt = [jnp.exp(lsig[h] + cs[h][:, 0:tk] + carry[h]) for h in heads]
        if diagonal:
            wgt = [jnp.where(causal, u, 0.0) for u in wgt]
        for h in heads:
            carry_scr[h, r0:tq, :] = carry[h] + cs[h][:, tk:2 * tk]
        wcat = jnp.concatenate([u.astype(BF16) for u in wgt], axis=1)
        acc_scr[r0:tq, :] += _dot(wcat, vcat)

    for jd in reversed(range(nsub)):
        block(qi * nsub + jd, jd * tk, True)

    def alive():
        return (jnp.max(jnp.maximum(carry_scr[0], carry_scr[1])) > SB_DEAD).astype(jnp.int32)

    def cond(c):
        return (c[0] < qi * nsub) & (c[1] > 0)

    def body(c):
        block(qi * nsub - 1 - c[0], 0, False)
        return c[0] + 1, alive()

    lax.while_loop(cond, body, (jnp.int32(0), alive()))
    o = acc_scr[...]
    ms = _dot_seg(o * o, bd_ref[...]) * (1.0 / HEAD_DIM)
    o_ref[...] = (o * lax.rsqrt(ms + HEAD_NORM_EPS) * g_ref[...]).astype(o_ref.dtype)


def _sb(p_s, norm_g, bd):
    bsz, s, _ = p_s.shape
    npair = SB_W // LANES
    tq = min(SB_TQ, s)
    si = jnp.arange(SB_TK)[:, None]
    sj = jnp.arange(2 * SB_TK)[None, :]
    sfx = ((si > sj) | (sj >= SB_TK)).astype(BF16)
    return pl.pallas_call(
        functools.partial(_sb_kernel, tq=tq),
        grid=(bsz, npair, s // tq),
        in_specs=[
            pl.BlockSpec((None, tq, LANES), lambda b, h, i: (b, i, h)),
            pl.BlockSpec((None, s, LANES), lambda b, h, i: (b, 0, npair + h)),
            pl.BlockSpec((None, s, LANES), lambda b, h, i: (b, 0, 2 * npair + h)),
            pl.BlockSpec((1, LANES), lambda b, h, i: (0, h)),
            pl.BlockSpec((LANES, LANES), lambda b, h, i: (0, 0)),
            pl.BlockSpec((SB_TK, 2 * SB_TK), lambda b, h, i: (0, 0)),
        ],
        out_specs=pl.BlockSpec((None, tq, LANES), lambda b, h, i: (b, i, h)),
        out_shape=jax.ShapeDtypeStruct((bsz, s, SB_W), BF16),
        scratch_shapes=[pltpu.VMEM((tq, LANES), F32), pltpu.VMEM((2, tq, LANES), F32)],
        compiler_params=_cparams(("arbitrary", "arbitrary", "arbitrary")),
        name="stickbreak",
    )(p_s, p_s, p_s, norm_g, bd, sfx)


def _gla_kernel(p_ref, up_ref, gkb_ref, ng_ref, bdv_ref, exp_ref, o_ref, ht_scr, *, ns):
    t = GLA_CHUNK
    kw, vw = GLA_KW, GLA_VW
    nh = GLA_HEADS
    rows = ns * t
    nsub = t // GLA_SUB
    seqs = range(ns)

    @pl.when(pl.program_id(1) == 0)
    def _():
        ht_scr[...] = jnp.zeros_like(ht_scr)

    p = p_ref[...].reshape(rows, GLA_COLS_PAD)
    q = p[:, 0:kw] * (GLA_KD ** -0.5)
    k = p[:, kw:2 * kw]
    v = p[:, 2 * kw:2 * kw + vw]
    g = p[:, 2 * kw + vw:2 * kw + 2 * vw]
    gk_low = p[:, 2 * kw + 2 * vw:2 * kw + 2 * vw + LANES]
    log_a = _log_sigmoid(_dot_f32(gk_low, up_ref[...]) + gkb_ref[...]) * (1.0 / GLA_NORMALIZER)
    ti = _iota((rows, rows), 0)
    tj = _iota((rows, rows), 1)
    tri_incl = jnp.where(((ti >> 6) == (tj >> 6)) & (ti >= tj), 1.0, 0.0).astype(BF16)
    beta = _dot_exact_lhs(tri_incl, log_a)
    row = _iota((rows, 1), 0)
    seq = row >> 6
    sub = (row & (t - 1)) >> 4

    def srows(a, si):
        return a[si * t:(si + 1) * t]

    beta_last = [beta[si * t + t - 1:si * t + t, :] for si in seqs]
    hts = [ht_scr[si] for si in seqs]
    q_exp = (q * jnp.exp(beta)).astype(BF16)
    o_inter = [_dot_nt(srows(q_exp, si), hts[si].astype(BF16)) for si in seqs]

    ref_rows = [[beta[si * t + GLA_SUB * i - 1:si * t + GLA_SUB * i, :] for i in range(1, nsub)] for si in seqs]
    beta_ref = jnp.zeros_like(beta)
    for si in seqs:
        for i in range(1, nsub):
            beta_ref = jnp.where((seq == si) & (sub == i), ref_rows[si][i - 1], beta_ref)
    q_hat = q * jnp.exp(jnp.minimum(beta - beta_ref, 0.0))
    lane_k = _iota((1, kw), 1)
    lane_v = _iota((1, vw), 1)
    mk = [jnp.where((lane_k >> 5) == h, 1.0, 0.0) for h in range(nh)]
    mv = [jnp.where((lane_v >> 6) == h, 1.0, 0.0) for h in range(nh)]
    q_st = [jnp.concatenate([srows(q_hat, si) * mk[h] for h in range(nh)], axis=0).astype(BF16) for si in seqs]
    v_st = [jnp.concatenate([srows(v, si) * mv[h] for h in range(nh)], axis=0).astype(BF16) for si in seqs]
    n4 = nh * t
    ri = _iota((n4, n4), 0)
    ci = _iota((n4, n4), 1)
    rsub = (ri & (t - 1)) >> 4
    ct = ci & (t - 1)
    attn = [jnp.zeros((n4, n4), F32) for _ in seqs]
    for i in range(1, nsub):
        k_hat = [srows(k, si) * jnp.exp(jnp.minimum(ref_rows[si][i - 1] - srows(beta, si), 0.0)) for si in seqs]
        k_st = [jnp.concatenate([k_hat[si] * mk[h] for h in range(nh)], axis=0).astype(BF16) for si in seqs]
        gi = [_dot_nt(q_st[si], k_st[si]) for si in seqs]
        sel = (rsub == i) & (ct < GLA_SUB * i)
        attn = [jnp.where(sel, gi[si], attn[si]) for si in seqs]
    o_st = [_dot(attn[si].astype(BF16), v_st[si]) for si in seqs]
    o_seq = []
    for si in seqs:
        acc = o_inter[si]
        for h in range(nh):
            acc = acc + o_st[si][h * t:(h + 1) * t]
        o_seq.append(acc)
    o = jnp.concatenate(o_seq, axis=0)

    expand = exp_ref[...]
    tsub = row & (GLA_SUB - 1)
    for d in range(GLA_SUB):
        if d == 0:
            kd, bd_, vd = k, beta, v
        else:
            kd = pltpu.roll(k, d, axis=0)
            bd_ = pltpu.roll(beta, d, axis=0)
            vd = pltpu.roll(v, d, axis=0)
        term = jnp.where(tsub >= d, q * kd * jnp.exp(jnp.minimum(beta - bd_, 0.0)), 0.0)
        o = o + _dot(term.astype(BF16), expand) * vd

    hv = _iota((vw, kw), 0) >> 6
    hk = _iota((vw, kw), 1) >> 5
    k_end = [(srows(k, si) * jnp.exp(beta_last[si] - srows(beta, si))).astype(BF16) for si in seqs]
    upd = [_dot(srows(v, si).T.astype(BF16), k_end[si]) for si in seqs]
    for si in seqs:
        ht_scr[si] = hts[si] * jnp.exp(beta_last[si]) + jnp.where(hv == hk, upd[si], 0.0)

    ms = _dot_seg(o * o, bdv_ref[...]) * (1.0 / HEAD_DIM)
    on = o * lax.rsqrt(ms + HEAD_NORM_EPS) * ng_ref[...]
    o_ref[...] = (on * (g * _sigmoid(g))).astype(o_ref.dtype).reshape(ns, t, vw)


def _gla(p_g, gk_up, gk_b, norm_g, bdv, expand):
    bsz, s, _ = p_g.shape
    t = GLA_CHUNK
    ns = GLA_G if bsz % GLA_G == 0 else 1
    const = lambda m, n: pl.BlockSpec((m, n), lambda b, i: (0, 0))
    return pl.pallas_call(
        functools.partial(_gla_kernel, ns=ns),
        grid=(bsz // ns, s // t),
        in_specs=[
            pl.BlockSpec((ns, t, GLA_COLS_PAD), lambda b, i: (b, i, 0)),
            const(LANES, GLA_KW), const(1, GLA_KW), const(1, GLA_VW), const(GLA_VW, GLA_VW),
            const(GLA_KW, GLA_VW),
        ],
        out_specs=pl.BlockSpec((ns, t, GLA_VW), lambda b, i: (b, i, 0)),
        out_shape=jax.ShapeDtypeStruct((bsz, s, GLA_VW), BF16),
        scratch_shapes=[pltpu.VMEM((ns, GLA_VW, GLA_KW), F32)],
        compiler_params=_cparams(("arbitrary", "arbitrary")),
        name="gla",
    )(p_g, gk_up, gk_b, norm_g, bdv, expand)


def _outproj_kernel(yr_ref, ys_ref, yg_ref, x_ref, mod_ref, g_ref, w_ref, wr_ref, x1_ref, h_ref, lg_ref):
    mod = mod_ref[...]
    o1 = RWKV_W
    o2 = o1 + SB_W
    mix = (_dot(yr_ref[...], w_ref[0:o1, :]) + _dot(ys_ref[...], w_ref[o1:o2, :])
           + _dot(yg_ref[...], w_ref[o2:, :]))
    x1 = x_ref[...] + mod[2:3] * mix
    x1_ref[...] = x1
    h = _rms_mod(x1, g_ref[...], mod[4:5], mod[3:4])
    h_ref[...] = h.astype(BF16)
    lg_ref[...] = _dot_nt_f32(wr_ref[...], h)


def _outproj(y_r, y_s, y_g, x, mod_l, g, w_out, w_router_t):
    bsz, s, d = x.shape
    tm = min(512, s)
    nt = s // tm
    return pl.pallas_call(
        _outproj_kernel,
        grid=(bsz, nt),
        in_specs=[
            pl.BlockSpec((None, tm, RWKV_W), lambda b, i: (b, i, 0)),
            pl.BlockSpec((None, tm, SB_W), lambda b, i: (b, i, 0)),
            pl.BlockSpec((None, tm, GLA_VW), lambda b, i: (b, i, 0)),
            pl.BlockSpec((None, tm, d), lambda b, i: (b, i, 0)),
            pl.BlockSpec((None, 6, d), lambda b, i: (b, 0, 0)),
            pl.BlockSpec((1, d), lambda b, i: (0, 0)),
            pl.BlockSpec((d, d), lambda b, i: (0, 0)),
            pl.BlockSpec((N_EXPERTS, d), lambda b, i: (0, 0)),
        ],
        out_specs=[
            pl.BlockSpec((None, tm, d), lambda b, i: (b, i, 0)),
            pl.BlockSpec((None, tm, d), lambda b, i: (b, i, 0)),
            pl.BlockSpec((N_EXPERTS, tm), lambda b, i: (0, b * nt + i)),
        ],
        out_shape=[
            jax.ShapeDtypeStruct((bsz, s, d), F32),
            jax.ShapeDtypeStruct((bsz, s, d), BF16),
            jax.ShapeDtypeStruct((N_EXPERTS, bsz * s), F32),
        ],
        compiler_params=_cparams(("arbitrary", "arbitrary")),
        name="outproj",
    )(y_r, y_s, y_g, x, mod_l, g, w_out, w_router_t)


def _route_kernel(lg_ref, bias_ref, tri_ref, w_ref, rank_ref, cnt_ref):
    aff = _sigmoid(lg_ref[...])
    sel = aff + bias_ref[...]
    e = EXPERTS_PER_GROUP
    rows = [sel[i:i + 1, :] for i in range(N_EXPERTS)]
    arow = [aff[i:i + 1, :] for i in range(N_EXPERTS)]
    scores = []
    for gi in range(N_GROUPS):
        a, b, c, d = rows[e * gi:e * gi + e]
        scores.append(jnp.maximum(jnp.maximum(jnp.maximum(a + b, a + c), jnp.maximum(a + d, b + c)),
                                  jnp.maximum(b + d, c + d)))
    grp = jnp.zeros_like(scores[0]).astype(jnp.int32)
    best = scores[0]
    for gi in range(1, N_GROUPS):
        better = scores[gi] > best
        grp = jnp.where(better, gi, grp)
        best = jnp.where(better, scores[gi], best)
    sin, ain = [], []
    for j in range(e):
        sv, av = rows[j], arow[j]
        for gi in range(1, N_GROUPS):
            sv = jnp.where(grp == gi, rows[e * gi + j], sv)
            av = jnp.where(grp == gi, arow[e * gi + j], av)
        sin.append(sv)
        ain.append(av)
    loc1 = jnp.zeros_like(grp)
    b1 = sin[0]
    for j in range(1, e):
        better = sin[j] > b1
        loc1 = jnp.where(better, j, loc1)
        b1 = jnp.where(better, sin[j], b1)
    neg = jnp.full_like(b1, -jnp.inf)
    loc2 = jnp.zeros_like(grp)
    b2 = neg
    for j in range(e):
        cand = jnp.where(loc1 == j, neg, sin[j])
        better = cand > b2
        loc2 = jnp.where(better, j, loc2)
        b2 = jnp.where(better, cand, b2)
    a1 = ain[0]
    a2 = ain[0]
    for j in range(1, e):
        a1 = jnp.where(loc1 == j, ain[j], a1)
        a2 = jnp.where(loc2 == j, ain[j], a2)
    den = a1 + a2
    e1 = grp * e + loc1
    e2 = grp * e + loc2
    eid = _iota(aff.shape, 0)
    is1 = eid == e1
    is2 = eid == e2
    w_ref[...] = jnp.where(is1, a1 / den, jnp.where(is2, a2 / den, 0.0))
    selected = is1 | is2
    self = jnp.where(selected, 1.0, 0.0)
    excl = _dot(self.astype(BF16), tri_ref[...])
    rank_ref[...] = jnp.where(selected, excl, -1.0)
    cnt = jnp.sum(self, axis=1, keepdims=True)
    cnt_ref[...] = jnp.broadcast_to(cnt, cnt_ref.shape)


def _route(logits_t, bias, tri):
    ne, n = logits_t.shape
    tm = tri.shape[0]
    nt = n // tm
    return pl.pallas_call(
        _route_kernel,
        grid=(nt,),
        in_specs=[
            pl.BlockSpec((ne, tm), lambda i: (0, i)),
            pl.BlockSpec((ne, 1), lambda i: (0, 0)),
            pl.BlockSpec((tm, tm), lambda i: (0, 0)),
        ],
        out_specs=[
            pl.BlockSpec((ne, tm), lambda i: (0, i)),
            pl.BlockSpec((ne, tm), lambda i: (0, i)),
            pl.BlockSpec((None, ne, LANES), lambda i: (i, 0, 0)),
        ],
        out_shape=[
            jax.ShapeDtypeStruct((ne, n), F32),
            jax.ShapeDtypeStruct((ne, n), F32),
            jax.ShapeDtypeStruct((nt, ne, LANES), F32),
        ],
        compiler_params=_cparams(("arbitrary",)),
        name="route",
    )(logits_t, bias, tri)


def _moe_kernel(cnt_ref, h_ref, rank_ref, w_ref, wg_ref, wu_ref, wd_ref, x1_ref, mod_ref, o_ref,
                acc_scr, obuf, ybuf, pend):
    i = pl.program_id(0)
    ep = pl.program_id(1)
    r = MOE_R
    log_r = r.bit_length() - 1

    @pl.when(ep == 0)
    def _():
        acc_scr[...] = jnp.zeros_like(acc_scr)
        pend[0] = 0

    def scatter(onehot, yw):
        acc_scr[...] += lax.dot_general(onehot, yw, (((0,), (0,)), ((), ())), preferred_element_type=F32)

    def select(k, base):
        hit = (rank_ref[k].astype(jnp.int32) - base) == _iota((r, 1), 0)
        wcol = jnp.sum(jnp.where(hit, w_ref[k], 0.0), axis=1, keepdims=True)
        return jnp.where(hit, 1.0, 0.0).astype(BF16), wcol

    def ffn_in(xr, k):
        gate = _dot(xr, wg_ref[k])
        up = _dot(xr, wu_ref[k])
        return (gate * _sigmoid(gate) * up).astype(BF16)

    n0 = (cnt_ref[i * N_EXPERTS + 2 * ep] + (r - 1)) >> log_r
    n1 = (cnt_ref[i * N_EXPERTS + 2 * ep + 1] + (r - 1)) >> log_r
    n_both = jnp.minimum(n0, n1)

    def both(j, c):
        oh0, w0 = select(0, j * r)
        oh1, w1 = select(1, j * r)
        onehot = jnp.concatenate([oh0, oh1], axis=0)
        xr = _dot(onehot, h_ref[...]).astype(BF16)
        a0 = ffn_in(xr[0:r], 0)
        a1 = ffn_in(xr[r:2 * r], 1)
        y0 = _dot(a0, wd_ref[0])
        y1 = _dot(a1, wd_ref[1])
        scatter(onehot, jnp.concatenate([y0 * w0, y1 * w1], axis=0).astype(BF16))
        return c

    lax.fori_loop(0, n_both, both, 0)

    k_rest = jnp.where(n0 > n1, 0, 1)

    def rest(j, c):
        onehot, wcol = select(k_rest, j * r)
        xr = _dot(onehot, h_ref[...]).astype(BF16)
        y = _dot(ffn_in(xr, k_rest), wd_ref[k_rest])
        slot = pend[0]
        off = pl.multiple_of(slot * r, r)
        obuf[pl.ds(off, r), :] = onehot
        ybuf[pl.ds(off, r), :] = (y * wcol).astype(BF16)

        @pl.when(slot == 1)
        def _():
            scatter(obuf[...], ybuf[...])

        pend[0] = 1 - slot
        return c

    lax.fori_loop(n_both, jnp.maximum(n0, n1), rest, 0)

    @pl.when(ep == N_EXPERTS // 2 - 1)
    def _():
        @pl.when(pend[0] == 1)
        def _():
            scatter(obuf[0:r, :], ybuf[0:r, :])

        o_ref[...] = x1_ref[...] + mod_ref[5:6, :] * acc_scr[...]


def _moe(cnt, h, rank_t, w_t, w_gate, w_up, w_down, x1, mod_l, tm, tiles_per_batch):
    n, d = h.shape
    de = w_gate.shape[-1]
    nt = n // tm
    grid_spec = pltpu.PrefetchScalarGridSpec(
        num_scalar_prefetch=1,
        grid=(nt, N_EXPERTS // 2),
        in_specs=[
            pl.BlockSpec((tm, d), lambda i, e, c: (i, 0)),
            pl.BlockSpec((2, 1, tm), lambda i, e, c: (e, 0, i)),
            pl.BlockSpec((2, 1, tm), lambda i, e, c: (e, 0, i)),
            pl.BlockSpec((2, d, de), lambda i, e, c: (e, 0, 0)),
            pl.BlockSpec((2, d, de), lambda i, e, c: (e, 0, 0)),
            pl.BlockSpec((2, de, d), lambda i, e, c: (e, 0, 0)),
            pl.BlockSpec((tm, d), lambda i, e, c: (i, 0)),
            pl.BlockSpec((None, 6, d), lambda i, e, c: (i // tiles_per_batch, 0, 0)),
        ],
        out_specs=pl.BlockSpec((tm, d), lambda i, e, c: (i, 0)),
        scratch_shapes=[pltpu.VMEM((tm, d), F32), pltpu.VMEM((2 * MOE_R, tm), BF16),
                        pltpu.VMEM((2 * MOE_R, d), BF16), pltpu.SMEM((1,), jnp.int32)],
    )
    return pl.pallas_call(
        _moe_kernel,
        grid_spec=grid_spec,
        out_shape=jax.ShapeDtypeStruct((n, d), F32),
        compiler_params=_cparams(("arbitrary", "arbitrary")),
        name="moe",
    )(cnt, h, rank_t.reshape(N_EXPERTS, 1, n), w_t.reshape(N_EXPERTS, 1, n), w_gate, w_up, w_down, x1, mod_l)


def _final_kernel(x_ref, g_ref, o_ref):
    x = x_ref[...]
    ms = jnp.mean(x * x, axis=-1, keepdims=True)
    o_ref[...] = x * lax.rsqrt(ms + RMS_EPS) * g_ref[...]


def _final_norm(x, g):
    n, d = x.shape
    tm = min(1024, n)
    return pl.pallas_call(
        _final_kernel,
        grid=(n // tm,),
        in_specs=[pl.BlockSpec((tm, d), lambda i: (i, 0)), pl.BlockSpec((1, d), lambda i: (0, 0))],
        out_specs=pl.BlockSpec((tm, d), lambda i: (i, 0)),
        out_shape=jax.ShapeDtypeStruct((n, d), F32),
        compiler_params=_cparams(("arbitrary",)),
        name="final_norm",
    )(x, g)


def _pad_cols(w, n):
    return jnp.pad(w, [(0, 0)] * (w.ndim - 1) + [(0, n - w.shape[-1])])


def _pad_rows(w, n):
    return jnp.pad(w, [(0, 0)] * (w.ndim - 2) + [(0, n - w.shape[-2]), (0, 0)])


def _layout_in_cols(w):
    rw = RWKV_W
    o = 3 * rw
    parts = [w[..., 0:o], _pad_cols(w[..., o:o + 64], LANES), _pad_cols(w[..., o + 64:o + 128], LANES),
             w[..., o + 128:o + 256]]
    o += 256
    parts.append(w[..., o:o + SB_COLS])
    o += SB_COLS
    parts.append(_pad_cols(w[..., o:], GLA_COLS_PAD))
    return jnp.concatenate(parts, axis=-1)


def kernel(x, c, rms_mix_g, rms_ffn_g, w_mod, b_mod, w_in, w_out, rwkv_mu, rwkv_w0, rwkv_w2, rwkv_a0, rwkv_a2, rwkv_g2, rwkv_k_k, rwkv_k_a, rwkv_r_k, rwkv_lnx_w, rwkv_lnx_b, sb_norm_g, gla_gk_up, gla_gk_b, gla_norm_g, w_router, router_bias, w_gate, w_up, w_down, final_g):
    bsz, s, d = x.shape
    depth = w_in.shape[0]
    n = bsz * s

    mod = _modulation(c, w_mod, b_mod).reshape(depth, bsz, 6, d)
    w_in_l = _layout_in_cols(w_in).astype(BF16)
    w_out_b = w_out.astype(BF16)
    mu_l = _layout_in_cols(jnp.pad(rwkv_mu, ((0, 0), (0, w_in.shape[-1] - rwkv_mu.shape[-1]))))[:, :RWKV_COLS_PAD]
    w2_p = _pad_rows(rwkv_w2, LANES)
    a2_p = _pad_rows(rwkv_a2, LANES)
    up_p = _pad_rows(gla_gk_up, LANES)
    gla_ng = jnp.tile(gla_norm_g, (1, GLA_HEADS))
    bd_rwkv = _block_diag_const(RWKV_W, HEAD_DIM, 1.0)
    bd_sb = _block_diag_const(LANES, HEAD_DIM, 1.0)
    bd_gla = _block_diag_const(GLA_VW, HEAD_DIM, 1.0)
    expand = (jnp.arange(GLA_KW)[:, None] // GLA_KD == jnp.arange(GLA_VW)[None, :] // HEAD_DIM).astype(BF16)
    tm_moe = min(MOE_TM, s)
    tri = (jnp.arange(tm_moe)[:, None] < jnp.arange(tm_moe)[None, :]).astype(BF16)
    w_router_t = w_router.T
    bias_col = router_bias.reshape(N_EXPERTS, 1)
    wg_b, wu_b, wd_b = w_gate.astype(BF16), w_up.astype(BF16), w_down.astype(BF16)
    row = lambda a: a.reshape(1, -1)

    for l in range(depth):
        p_r, p_s, p_g = _inproj(x, mod[l], row(rms_mix_g[l]), w_in_l[l])
        y_r = _rwkv(p_r, row(mu_l[l]), row(rwkv_w0[l]), w2_p[l], row(rwkv_a0[l]), a2_p[l], rwkv_g2[l],
                    row(rwkv_k_k[l]), row(rwkv_k_a[l]), row(rwkv_r_k[l]), row(rwkv_lnx_w[l]),
                    row(rwkv_lnx_b[l]), bd_rwkv)
        y_s = _sb(p_s, row(sb_norm_g[l]), bd_sb)
        y_g = _gla(p_g, up_p[l], row(gla_gk_b[l]), row(gla_ng[l]), bd_gla, expand)
        x1, h2, logits_t = _outproj(y_r, y_s, y_g, x, mod[l], row(rms_ffn_g[l]), w_out_b[l], w_router_t)
        w_t, rank_t, cnt = _route(logits_t, bias_col, tri)
        cnt_i = cnt[:, :, 0].astype(jnp.int32).reshape(-1)
        x = _moe(cnt_i, h2.reshape(n, d), rank_t, w_t, wg_b[l], wu_b[l], wd_b[l], x1.reshape(n, d),
                 mod[l], tm_moe, s // tm_moe).reshape(bsz, s, d)
    return _final_norm(x.reshape(n, d), row(final_g)).reshape(bsz, s, d)
```

```python
import functools

import jax
import jax.numpy as jnp
from jax import lax
from jax.experimental import pallas as pl
from jax.experimental.pallas import tpu as pltpu

F32 = jnp.float32
BF16 = jnp.bfloat16

LANES = 128
HEAD_DIM = 64
RWKV_W = 512
RWKV_PAIRS = RWKV_W // LANES
RWKV_COLS_PAD = 3 * RWKV_W + 3 * LANES
RWKV_CHUNK = 64
RWKV_G = 4
RWKV_GN_EPS = 64e-5
SB_W = 256
SB_COLS = 3 * SB_W
SB_TQ = 1024
SB_TK = 128
SB_DEAD = -104.0
GLA_VW = 256
GLA_KW = 128
GLA_HEADS = 4
GLA_KD = 32
GLA_LORA = 16
GLA_COLS_PAD = 2 * GLA_KW + 2 * GLA_VW + LANES
GLA_CHUNK = 64
GLA_SUB = 16
GLA_G = 4
GLA_NORMALIZER = 16.0
IN_COLS_PAD = RWKV_COLS_PAD + SB_COLS + GLA_COLS_PAD
HEAD_NORM_EPS = 1e-5
RMS_EPS = 1e-6
N_EXPERTS = 16
EXPERTS_PER_GROUP = 4
N_GROUPS = 4
OUTPROJ_SPLIT = 4
MOE_TM = 1024
MOE_R = 128
VMEM_LIMIT = 48 * 1024 * 1024


def _dot(a, b):
    return jnp.dot(a, b, preferred_element_type=F32)


def _dot_nt(a, b):
    return lax.dot_general(a, b, (((1,), (1,)), ((), ())), preferred_element_type=F32)


def _split2(x):
    hi = x.astype(BF16)
    lo = (x - hi.astype(F32)).astype(BF16)
    return hi, lo


def _split3(x):
    hi = x.astype(BF16)
    r = x - hi.astype(F32)
    mid = r.astype(BF16)
    lo = (r - mid.astype(F32)).astype(BF16)
    return hi, mid, lo


def _dot_seg(x, m):
    hi, lo = _split2(x)
    wm = m.shape[0]
    cols = [_dot(hi[:, c:c + wm], m) + _dot(lo[:, c:c + wm], m) for c in range(0, x.shape[1], wm)]
    return cols[0] if len(cols) == 1 else jnp.concatenate(cols, axis=1)


def _dot_exact_lhs(m, x):
    hi, mid, lo = _split3(x)
    return _dot(m, hi) + _dot(m, mid) + _dot(m, lo)


def _dot_f32(a, b):
    ah, al = _split2(a)
    bh, bl = _split2(b)
    return _dot(ah, bh) + _dot(ah, bl) + _dot(al, bh)


def _dot_nt_f32(a, b):
    ah, al = _split2(a)
    bh, bl = _split2(b)
    return _dot_nt(ah, bh) + _dot_nt(ah, bl) + _dot_nt(al, bh)


def _softplus(x):
    return jnp.maximum(x, 0.0) + jnp.log(1.0 + jnp.exp(-jnp.abs(x)))


def _log_sigmoid(x):
    return jnp.minimum(x, 0.0) - jnp.log(1.0 + jnp.exp(-jnp.abs(x)))


def _sigmoid(x):
    return 1.0 / (1.0 + jnp.exp(-x))


def _iota(shape, dim):
    return lax.broadcasted_iota(jnp.int32, shape, dim)


def _block_diag_const(n, blk, val):
    i = jnp.arange(n)
    return jnp.where((i[:, None] // blk) == (i[None, :] // blk), val, 0.0).astype(BF16)


def _cparams(sem):
    return pltpu.CompilerParams(dimension_semantics=sem, vmem_limit_bytes=VMEM_LIMIT)


def _mod_kernel(c_ref, w_ref, b_ref, o_ref):
    c = c_ref[...]
    ca = c * _sigmoid(c)
    o_ref[...] = _dot_f32(ca, w_ref[...]) + b_ref[...]


def _modulation(c, w_mod, b_mod):
    depth, d, six_d = w_mod.shape
    bsz = c.shape[0]
    tn = 1536
    return pl.pallas_call(
        _mod_kernel,
        grid=(depth, six_d // tn),
        in_specs=[
            pl.BlockSpec((bsz, d), lambda l, j: (0, 0)),
            pl.BlockSpec((None, d, tn), lambda l, j: (l, 0, j)),
            pl.BlockSpec((None, 1, tn), lambda l, j: (l, 0, j)),
        ],
        out_specs=pl.BlockSpec((None, bsz, tn), lambda l, j: (l, 0, j)),
        out_shape=jax.ShapeDtypeStruct((depth, bsz, six_d), F32),
        compiler_params=_cparams(("arbitrary", "arbitrary")),
        name="adaln_mod",
    )(c, w_mod, b_mod.reshape(depth, 1, six_d))


def _rms_mod(x, g, scale, shift):
    ms = jnp.mean(x * x, axis=-1, keepdims=True)
    return x * lax.rsqrt(ms + RMS_EPS) * g * (1.0 + scale) + shift


def _inproj_kernel(x_ref, mod_ref, g_ref, w_ref, pr_ref, ps_ref, pg_ref):
    mod = mod_ref[...]
    h = _rms_mod(x_ref[...], g_ref[...], mod[1:2], mod[0:1]).astype(BF16)
    o1 = RWKV_COLS_PAD
    o2 = o1 + SB_COLS
    pr_ref[...] = _dot(h, w_ref[:, 0:o1])
    ps_ref[...] = _dot(h, w_ref[:, o1:o2]).astype(BF16)
    pg_ref[...] = _dot(h, w_ref[:, o2:IN_COLS_PAD])


def _inproj(x, mod_l, g, w, layer):
    bsz, s, d = x.shape
    tm = min(512, s)
    return pl.pallas_call(
        _inproj_kernel,
        grid=(bsz, s // tm),
        in_specs=[
            pl.BlockSpec((None, tm, d), lambda b, i: (b, i, 0)),
            pl.BlockSpec((None, 6, d), lambda b, i: (b, 0, 0)),
            pl.BlockSpec((1, d), lambda b, i: (0, 0)),
            pl.BlockSpec((None, d, IN_COLS_PAD), lambda b, i: (layer, 0, 0)),
        ],
        out_specs=[
            pl.BlockSpec((None, tm, RWKV_COLS_PAD), lambda b, i: (b, i, 0)),
            pl.BlockSpec((None, tm, SB_COLS), lambda b, i: (b, i, 0)),
            pl.BlockSpec((None, tm, GLA_COLS_PAD), lambda b, i: (b, i, 0)),
        ],
        out_shape=[
            jax.ShapeDtypeStruct((bsz, s, RWKV_COLS_PAD), F32),
            jax.ShapeDtypeStruct((bsz, s, SB_COLS), BF16),
            jax.ShapeDtypeStruct((bsz, s, GLA_COLS_PAD), F32),
        ],
        compiler_params=_cparams(("arbitrary", "arbitrary")),
        name="inproj",
    )(x, mod_l, g, w)


def _rwkv_kernel(p_ref, mu_ref, w0_ref, w2_ref, a0_ref, a2_ref, g2_ref, kk_ref, ka_ref, rk_ref,
                 lnw_ref, lnb_ref, bd_ref, o_ref, ht_scr, prev_scr, *, ng):
    t = RWKV_CHUNK
    rows = ng * t

    @pl.when(pl.program_id(1) == 0)
    def _():
        ht_scr[...] = jnp.zeros_like(ht_scr)
        prev_scr[...] = jnp.zeros_like(prev_scr)

    p = p_ref[...].reshape(rows, RWKV_COLS_PAD)
    row = _iota((rows, 1), 0)
    prev = pltpu.roll(p, 1, axis=0)
    for gi in range(ng):
        prev = jnp.where(row == gi * t, prev_scr[gi, 0:1, :], prev)
        prev_scr[gi, 0:1, :] = p[gi * t + t - 1:gi * t + t, :]
    xm = p + (prev - p) * mu_ref[...]
    w = RWKV_W
    r = xm[:, 0:w]
    k = xm[:, w:2 * w]
    v = xm[:, 2 * w:3 * w]
    xw = xm[:, 3 * w:3 * w + LANES]
    xa = xm[:, 3 * w + LANES:3 * w + 2 * LANES]
    xg = xm[:, 3 * w + 2 * LANES:3 * w + 3 * LANES]
    w_log = -_softplus(-(w0_ref[...] + _dot_f32(jnp.tanh(xw), w2_ref[...]))) - 0.5
    lw = -jnp.exp(w_log)
    iclr = _sigmoid(a0_ref[...] + _dot_f32(xa, a2_ref[...]))
    g = _dot_f32(_sigmoid(xg), g2_ref[...])
    bd = bd_ref[...]
    kkr = k * kk_ref[...]
    ss = _dot_seg(kkr * kkr, bd)
    kk = kkr * lax.rsqrt(jnp.maximum(ss, 1e-24))
    k2 = k * (1.0 + (iclr - 1.0) * ka_ref[...])
    bonus = _dot_seg(r * k2 * rk_ref[...], bd) * v

    ti = _iota((rows, rows), 0)
    tj = _iota((rows, rows), 1)
    tri_incl = jnp.where(((ti >> 6) == (tj >> 6)) & (ti >= tj), 1.0, 0.0).astype(BF16)
    beta = _dot_exact_lhs(tri_incl, lw)
    gam = jnp.exp(beta)
    gam_inv = jnp.exp(-beta)
    a_t = -kk * jnp.exp(beta - lw)
    r_t = r * gam
    b_t = kk * iclr * gam_inv
    k_t = k2 * gam_inv

    lane = _iota((1, LANES), 1)
    m0 = jnp.where(lane < HEAD_DIM, 1.0, 0.0)
    m1 = 1.0 - m0
    n2 = 2 * t
    ii = _iota((n2, n2), 0)
    jj = _iota((n2, n2), 1)
    it = ii & (t - 1)
    jt = jj & (t - 1)
    same64 = (ii >> 6) == (jj >> 6)
    strict = same64 & (it > jt)
    incl = same64 & (it >= jt)
    blk16 = (ii >> 4) == (jj >> 4)
    blk32 = (ii >> 5) == (jj >> 5)
    not16 = jnp.logical_not(blk16)
    not32 = jnp.logical_not(blk32)
    eye = jnp.where(ii == jj, 1.0, 0.0)

    units = [(gi, pr) for gi in range(ng) for pr in range(RWKV_PAIRS)]
    U = range(len(units))

    def cut(arr, gi, pr):
        return arr[gi * t:(gi + 1) * t, pr * LANES:(pr + 1) * LANES]

    def stack(xa_, xb_, gi, pr):
        ca, cb = cut(xa_, gi, pr), cut(xb_, gi, pr)
        return jnp.concatenate([ca * m0, ca * m1, cb * m0, cb * m1], axis=0).astype(BF16)

    lhs = [stack(a_t, r_t, gi, pr) for gi, pr in units]
    rhs = [stack(b_t, k_t, gi, pr) for gi, pr in units]
    vst = [jnp.concatenate([cut(v, gi, pr) * m0, cut(v, gi, pr) * m1], axis=0) for gi, pr in units]
    vstb = [u.astype(BF16) for u in vst]
    gb = [_dot_nt(lhs[i], rhs[i]) for i in U]
    hts = [ht_scr[i] for i in U]
    p0 = [_dot_nt(lhs[i], hts[i].astype(BF16)) for i in U]
    a_ab = [jnp.where(strict, g_[0:n2, 0:n2], 0.0) for g_ in gb]
    a_ak = [jnp.where(strict, g_[0:n2, n2:2 * n2], 0.0).astype(BF16) for g_ in gb]
    a_r = [jnp.concatenate([jnp.where(incl, g_[n2:2 * n2, 0:n2], 0.0),
                            jnp.where(incl, g_[n2:2 * n2, n2:2 * n2], 0.0)], axis=1).astype(BF16) for g_ in gb]
    d1 = [jnp.where(blk16, a, 0.0) for a in a_ab]
    d1b = [d.astype(BF16) for d in d1]
    x = [eye + d for d in d1]
    d2b = [_dot(d, d).astype(BF16) for d in d1b]
    x = [x[i] + _dot(x[i].astype(BF16), d2b[i]) for i in U]
    d4b = [_dot(d, d).astype(BF16) for d in d2b]
    x = [x[i] + _dot(x[i].astype(BF16), d4b[i]) for i in U]
    d8b = [_dot(d, d).astype(BF16) for d in d4b]
    x = [x[i] + _dot(x[i].astype(BF16), d8b[i]) for i in U]
    e32 = [jnp.where(blk32 & not16, a, 0.0).astype(BF16) for a in a_ab]
    xb = [u.astype(BF16) for u in x]
    t1 = [_dot(e32[i], xb[i]).astype(BF16) for i in U]
    x = [x[i] + _dot(xb[i], t1[i]) for i in U]
    e64 = [jnp.where(not32, a, 0.0).astype(BF16) for a in a_ab]
    xb = [u.astype(BF16) for u in x]
    t2 = [_dot(e64[i], xb[i]).astype(BF16) for i in U]
    x = [x[i] + _dot(xb[i], t2[i]) for i in U]
    rhs_u = [p0[i][0:n2] + _dot(a_ak[i], vstb[i]) for i in U]
    ust = [_dot(x[i].astype(BF16), rhs_u[i].astype(BF16)) for i in U]
    uv = [jnp.concatenate([ust[i], vst[i]], axis=0) for i in U]
    yst = [p0[i][n2:2 * n2] + _dot(a_r[i], uv[i].astype(BF16)) for i in U]
    ys = [u[0:t] + u[t:n2] for u in yst]
    upd = [_dot(uv[i].T.astype(BF16), rhs[i]) for i in U]
    for i, (gi, pr) in enumerate(units):
        last = gi * t + t - 1
        ht_scr[i] = (hts[i] + upd[i]) * gam[last:last + 1, pr * LANES:(pr + 1) * LANES]
    y = jnp.concatenate([jnp.concatenate(ys[gi * RWKV_PAIRS:(gi + 1) * RWKV_PAIRS], axis=1)
                         for gi in range(ng)], axis=0)
    inv_n = 1.0 / HEAD_DIM
    mean = _dot_seg(y, bd) * inv_n
    yc = y - mean
    var = _dot_seg(yc * yc, bd) * inv_n
    yn = yc * lax.rsqrt(var + RWKV_GN_EPS) * lnw_ref[...] + lnb_ref[...]
    o_ref[...] = ((yn + bonus) * g).astype(o_ref.dtype).reshape(ng, t, RWKV_W)


def _rwkv(p_r, mu, w0, w2, a0, a2, g2, k_k, k_a, r_k, lnw, lnb, bd):
    bsz, s, _ = p_r.shape
    t = RWKV_CHUNK
    ng = RWKV_G if bsz % RWKV_G == 0 else 1
    vec = lambda n: pl.BlockSpec((1, n), lambda b, i: (0, 0))
    mat = lambda m, n: pl.BlockSpec((m, n), lambda b, i: (0, 0))
    return pl.pallas_call(
        functools.partial(_rwkv_kernel, ng=ng),
        grid=(bsz // ng, s // t),
        in_specs=[
            pl.BlockSpec((ng, t, RWKV_COLS_PAD), lambda b, i: (b, i, 0)),
            vec(RWKV_COLS_PAD), vec(RWKV_W), mat(LANES, RWKV_W), vec(RWKV_W), mat(LANES, RWKV_W),
            mat(LANES, RWKV_W), vec(RWKV_W), vec(RWKV_W), vec(RWKV_W), vec(RWKV_W), vec(RWKV_W),
            mat(2 * LANES, 2 * LANES),
        ],
        out_specs=pl.BlockSpec((ng, t, RWKV_W), lambda b, i: (b, i, 0)),
        out_shape=jax.ShapeDtypeStruct((bsz, s, RWKV_W), BF16),
        scratch_shapes=[pltpu.VMEM((ng * RWKV_PAIRS, LANES, LANES), F32),
                        pltpu.VMEM((ng, 8, RWKV_COLS_PAD), F32)],
        compiler_params=_cparams(("arbitrary", "arbitrary")),
        name="rwkv7",
    )(p_r, mu, w0, w2, a0, a2, g2, k_k, k_a, r_k, lnw, lnb, bd)


def _sb_kernel(q_ref, k_ref, v_ref, g_ref, bd_ref, sfx_ref, o_ref, acc_scr, carry_scr, *, tq):
    tk = SB_TK
    nsub = tq // tk
    qi = pl.program_id(2)
    lane = _iota((1, LANES), 1)
    m0 = jnp.where(lane < HEAD_DIM, 1.0, 0.0).astype(BF16)
    m1 = (1.0 - m0.astype(F32)).astype(BF16)
    q = q_ref[...] * jnp.asarray(HEAD_DIM ** -0.5, BF16)
    qh = (q * m0, q * m1)
    suffix = sfx_ref[...]
    acc_scr[...] = jnp.zeros_like(acc_scr)
    carry_scr[...] = jnp.zeros_like(carry_scr)
    heads = range(2)

    def block(j, r0, diagonal):
        off = pl.multiple_of(j * tk, tk)
        kb = k_ref[pl.ds(off, tk), :]
        vb = v_ref[pl.ds(off, tk), :]
        vcat = jnp.concatenate([vb * m0, vb * m1], axis=0)
        z = [_dot_nt(qh[h][r0:tq], kb) for h in heads]
        l1p = [jnp.log(1.0 + jnp.exp(-jnp.abs(u))) for u in z]
        lk = [-(jnp.maximum(z[h], 0.0) + l1p[h]) for h in heads]
        lsig = [jnp.minimum(z[h], 0.0) - l1p[h] for h in heads]
        if diagonal:
            rows = tq - r0
            causal = _iota((rows, tk), 1) < _iota((rows, tk), 0)
            lk = [jnp.where(causal, u, 0.0) for u in lk]
        cs = [_dot(lk[h].astype(BF16), suffix) for h in heads]
        carry = [carry_scr[h, r0:tq, :] for h in heads]
        wgt = [jnp.exp(lsig[h] + cs[h][:, 0:tk] + carry[h]) for h in heads]
        if diagonal:
            wgt = [jnp.where(causal, u, 0.0) for u in wgt]
        for h in heads:
            carry_scr[h, r0:tq, :] = carry[h] + cs[h][:, tk:2 * tk]
        wcat = jnp.concatenate([u.astype(BF16) for u in wgt], axis=1)
        acc_scr[r0:tq, :] += _dot(wcat, vcat)

    for jd in reversed(range(nsub)):
        block(qi * nsub + jd, jd * tk, True)

    def alive():
        return (jnp.max(jnp.maximum(carry_scr[0], carry_scr[1])) > SB_DEAD).astype(jnp.int32)

    def cond(c):
        return (c[0] < qi * nsub) & (c[1] > 0)

    def body(c):
        block(qi * nsub - 1 - c[0], 0, False)
        return c[0] + 1, alive()

    lax.while_loop(cond, body, (jnp.int32(0), alive()))
    o = acc_scr[...]
    ms = _dot_seg(o * o, bd_ref[...]) * (1.0 / HEAD_DIM)
    o_ref[...] = (o * lax.rsqrt(ms + HEAD_NORM_EPS) * g_ref[...]).astype(o_ref.dtype)


def _sb(p_s, norm_g, bd):
    bsz, s, _ = p_s.shape
    npair = SB_W // LANES
    tq = min(SB_TQ, s)
    si = jnp.arange(SB_TK)[:, None]
    sj = jnp.arange(2 * SB_TK)[None, :]
    sfx = ((si > sj) | (sj >= SB_TK)).astype(BF16)
    return pl.pallas_call(
        functools.partial(_sb_kernel, tq=tq),
        grid=(bsz, npair, s // tq),
        in_specs=[
            pl.BlockSpec((None, tq, LANES), lambda b, h, i: (b, i, h)),
            pl.BlockSpec((None, s, LANES), lambda b, h, i: (b, 0, npair + h)),
            pl.BlockSpec((None, s, LANES), lambda b, h, i: (b, 0, 2 * npair + h)),
            pl.BlockSpec((1, LANES), lambda b, h, i: (0, h)),
            pl.BlockSpec((LANES, LANES), lambda b, h, i: (0, 0)),
            pl.BlockSpec((SB_TK, 2 * SB_TK), lambda b, h, i: (0, 0)),
        ],
        out_specs=pl.BlockSpec((None, tq, LANES), lambda b, h, i: (b, i, h)),
        out_shape=jax.ShapeDtypeStruct((bsz, s, SB_W), BF16),
        scratch_shapes=[pltpu.VMEM((tq, LANES), F32), pltpu.VMEM((2, tq, LANES), F32)],
        compiler_params=_cparams(("arbitrary", "arbitrary", "arbitrary")),
        name="stickbreak",
    )(p_s, p_s, p_s, norm_g, bd, sfx)


def _gla_kernel(p_ref, up_ref, gkb_ref, ng_ref, bdv_ref, exp_ref, o_ref, ht_scr, *, ns):
    t = GLA_CHUNK
    kw, vw = GLA_KW, GLA_VW
    nh = GLA_HEADS
    rows = ns * t
    nsub = t // GLA_SUB
    seqs = range(ns)

    @pl.when(pl.program_id(1) == 0)
    def _():
        ht_scr[...] = jnp.zeros_like(ht_scr)

    p = p_ref[...].reshape(rows, GLA_COLS_PAD)
    q = p[:, 0:kw] * (GLA_KD ** -0.5)
    k = p[:, kw:2 * kw]
    v = p[:, 2 * kw:2 * kw + vw]
    g = p[:, 2 * kw + vw:2 * kw + 2 * vw]
    gk_low = p[:, 2 * kw + 2 * vw:2 * kw + 2 * vw + LANES]
    log_a = _log_sigmoid(_dot_f32(gk_low, up_ref[...]) + gkb_ref[...]) * (1.0 / GLA_NORMALIZER)
    ti = _iota((rows, rows), 0)
    tj = _iota((rows, rows), 1)
    tri_incl = jnp.where(((ti >> 6) == (tj >> 6)) & (ti >= tj), 1.0, 0.0).astype(BF16)
    beta = _dot_exact_lhs(tri_incl, log_a)
    row = _iota((rows, 1), 0)
    seq = row >> 6
    sub = (row & (t - 1)) >> 4

    def srows(a, si):
        return a[si * t:(si + 1) * t]

    beta_last = [beta[si * t + t - 1:si * t + t, :] for si in seqs]
    hts = [ht_scr[si] for si in seqs]
    q_exp = (q * jnp.exp(beta)).astype(BF16)
    o_inter = [_dot_nt(srows(q_exp, si), hts[si].astype(BF16)) for si in seqs]

    ref_rows = [[beta[si * t + GLA_SUB * i - 1:si * t + GLA_SUB * i, :] for i in range(1, nsub)] for si in seqs]
    beta_ref = jnp.zeros_like(beta)
    for si in seqs:
        for i in range(1, nsub):
            beta_ref = jnp.where((seq == si) & (sub == i), ref_rows[si][i - 1], beta_ref)
    q_hat = q * jnp.exp(jnp.minimum(beta - beta_ref, 0.0))
    lane_k = _iota((1, kw), 1)
    lane_v = _iota((1, vw), 1)
    mk = [jnp.where((lane_k >> 5) == h, 1.0, 0.0) for h in range(nh)]
    mv = [jnp.where((lane_v >> 6) == h, 1.0, 0.0) for h in range(nh)]
    q_st = [jnp.concatenate([srows(q_hat, si) * mk[h] for h in range(nh)], axis=0).astype(BF16) for si in seqs]
    v_st = [jnp.concatenate([srows(v, si) * mv[h] for h in range(nh)], axis=0).astype(BF16) for si in seqs]
    n4 = nh * t
    ri = _iota((n4, n4), 0)
    ci = _iota((n4, n4), 1)
    rsub = (ri & (t - 1)) >> 4
    ct = ci & (t - 1)
    attn = [jnp.zeros((n4, n4), F32) for _ in seqs]
    for i in range(1, nsub):
        k_hat = [srows(k, si) * jnp.exp(jnp.minimum(ref_rows[si][i - 1] - srows(beta, si), 0.0)) for si in seqs]
        k_st = [jnp.concatenate([k_hat[si] * mk[h] for h in range(nh)], axis=0).astype(BF16) for si in seqs]
        gi = [_dot_nt(q_st[si], k_st[si]) for si in seqs]
        sel = (rsub == i) & (ct < GLA_SUB * i)
        attn = [jnp.where(sel, gi[si], attn[si]) for si in seqs]
    o_st = [_dot(attn[si].astype(BF16), v_st[si]) for si in seqs]
    o_seq = []
    for si in seqs:
        acc = o_inter[si]
        for h in range(nh):
            acc = acc + o_st[si][h * t:(h + 1) * t]
        o_seq.append(acc)
    o = jnp.concatenate(o_seq, axis=0)

    expand = exp_ref[...]
    tsub = row & (GLA_SUB - 1)
    for d in range(GLA_SUB):
        if d == 0:
            kd, bd_, vd = k, beta, v
        else:
            kd = pltpu.roll(k, d, axis=0)
            bd_ = pltpu.roll(beta, d, axis=0)
            vd = pltpu.roll(v, d, axis=0)
        term = jnp.where(tsub >= d, q * kd * jnp.exp(jnp.minimum(beta - bd_, 0.0)), 0.0)
        o = o + _dot(term.astype(BF16), expand) * vd

    hv = _iota((vw, kw), 0) >> 6
    hk = _iota((vw, kw), 1) >> 5
    k_end = [(srows(k, si) * jnp.exp(beta_last[si] - srows(beta, si))).astype(BF16) for si in seqs]
    upd = [_dot(srows(v, si).T.astype(BF16), k_end[si]) for si in seqs]
    for si in seqs:
        ht_scr[si] = hts[si] * jnp.exp(beta_last[si]) + jnp.where(hv == hk, upd[si], 0.0)

    ms = _dot_seg(o * o, bdv_ref[...]) * (1.0 / HEAD_DIM)
    on = o * lax.rsqrt(ms + HEAD_NORM_EPS) * ng_ref[...]
    o_ref[...] = (on * (g * _sigmoid(g))).astype(o_ref.dtype).reshape(ns, t, vw)


def _gla(p_g, gk_up, gk_b, norm_g, bdv, expand):
    bsz, s, _ = p_g.shape
    t = GLA_CHUNK
    ns = GLA_G if bsz % GLA_G == 0 else 1
    const = lambda m, n: pl.BlockSpec((m, n), lambda b, i: (0, 0))
    return pl.pallas_call(
        functools.partial(_gla_kernel, ns=ns),
        grid=(bsz // ns, s // t),
        in_specs=[
            pl.BlockSpec((ns, t, GLA_COLS_PAD), lambda b, i: (b, i, 0)),
            const(LANES, GLA_KW), const(1, GLA_KW), const(1, GLA_VW), const(GLA_VW, GLA_VW),
            const(GLA_KW, GLA_VW),
        ],
        out_specs=pl.BlockSpec((ns, t, GLA_VW), lambda b, i: (b, i, 0)),
        out_shape=jax.ShapeDtypeStruct((bsz, s, GLA_VW), BF16),
        scratch_shapes=[pltpu.VMEM((ns, GLA_VW, GLA_KW), F32)],
        compiler_params=_cparams(("arbitrary", "arbitrary")),
        name="gla",
    )(p_g, gk_up, gk_b, norm_g, bdv, expand)


def _outproj_kernel(yr_ref, ys_ref, yg_ref, x_ref, mod_ref, g_ref, w_ref, wr_ref, x1_ref, h_ref, lg_ref):
    mod = mod_ref[...]
    o1 = RWKV_W
    o2 = o1 + SB_W
    tm = x_ref.shape[0]
    nq = OUTPROJ_SPLIT if tm % (OUTPROJ_SPLIT * LANES) == 0 else 1
    rq = tm // nq
    parts = [slice(qi * rq, (qi + 1) * rq) for qi in range(nq)]
    mix = [_dot(yr_ref[sl, :], w_ref[0:o1, :]) + _dot(ys_ref[sl, :], w_ref[o1:o2, :])
           + _dot(yg_ref[sl, :], w_ref[o2:, :]) for sl in parts]
    x1 = [x_ref[sl, :] + mod[2:3] * mix[qi] for qi, sl in enumerate(parts)]
    h = [_rms_mod(u, g_ref[...], mod[4:5], mod[3:4]) for u in x1]
    for qi, sl in enumerate(parts):
        x1_ref[sl, :] = x1[qi]
        h_ref[sl, :] = h[qi].astype(BF16)
        lg_ref[:, sl] = _dot_nt_f32(wr_ref[...], h[qi])


def _outproj(y_r, y_s, y_g, x, mod_l, g, w_out, layer, w_router_t):
    bsz, s, d = x.shape
    tm = min(512, s)
    nt = s // tm
    return pl.pallas_call(
        _outproj_kernel,
        grid=(bsz, nt),
        in_specs=[
            pl.BlockSpec((None, tm, RWKV_W), lambda b, i: (b, i, 0)),
            pl.BlockSpec((None, tm, SB_W), lambda b, i: (b, i, 0)),
            pl.BlockSpec((None, tm, GLA_VW), lambda b, i: (b, i, 0)),
            pl.BlockSpec((None, tm, d), lambda b, i: (b, i, 0)),
            pl.BlockSpec((None, 6, d), lambda b, i: (b, 0, 0)),
            pl.BlockSpec((1, d), lambda b, i: (0, 0)),
            pl.BlockSpec((None, d, d), lambda b, i: (layer, 0, 0)),
            pl.BlockSpec((N_EXPERTS, d), lambda b, i: (0, 0)),
        ],
        out_specs=[
            pl.BlockSpec((None, tm, d), lambda b, i: (b, i, 0)),
            pl.BlockSpec((None, tm, d), lambda b, i: (b, i, 0)),
            pl.BlockSpec((N_EXPERTS, tm), lambda b, i: (0, b * nt + i)),
        ],
        out_shape=[
            jax.ShapeDtypeStruct((bsz, s, d), F32),
            jax.ShapeDtypeStruct((bsz, s, d), BF16),
            jax.ShapeDtypeStruct((N_EXPERTS, bsz * s), F32),
        ],
        compiler_params=_cparams(("arbitrary", "arbitrary")),
        name="outproj",
    )(y_r, y_s, y_g, x, mod_l, g, w_out, w_router_t)


def _route_kernel(lg_ref, bias_ref, tri_ref, w_ref, rank_ref, cnt_ref):
    aff = _sigmoid(lg_ref[...])
    sel = aff + bias_ref[...]
    e = EXPERTS_PER_GROUP
    rows = [sel[i:i + 1, :] for i in range(N_EXPERTS)]
    arow = [aff[i:i + 1, :] for i in range(N_EXPERTS)]
    scores = []
    for gi in range(N_GROUPS):
        a, b, c, d = rows[e * gi:e * gi + e]
        scores.append(jnp.maximum(jnp.maximum(jnp.maximum(a + b, a + c), jnp.maximum(a + d, b + c)),
                                  jnp.maximum(b + d, c + d)))
    grp = jnp.zeros_like(scores[0]).astype(jnp.int32)
    best = scores[0]
    for gi in range(1, N_GROUPS):
        better = scores[gi] > best
        grp = jnp.where(better, gi, grp)
        best = jnp.where(better, scores[gi], best)
    sin, ain = [], []
    for j in range(e):
        sv, av = rows[j], arow[j]
        for gi in range(1, N_GROUPS):
            sv = jnp.where(grp == gi, rows[e * gi + j], sv)
            av = jnp.where(grp == gi, arow[e * gi + j], av)
        sin.append(sv)
        ain.append(av)
    loc1 = jnp.zeros_like(grp)
    b1 = sin[0]
    for j in range(1, e):
        better = sin[j] > b1
        loc1 = jnp.where(better, j, loc1)
        b1 = jnp.where(better, sin[j], b1)
    neg = jnp.full_like(b1, -jnp.inf)
    loc2 = jnp.zeros_like(grp)
    b2 = neg
    for j in range(e):
        cand = jnp.where(loc1 == j, neg, sin[j])
        better = cand > b2
        loc2 = jnp.where(better, j, loc2)
        b2 = jnp.where(better, cand, b2)
    a1 = ain[0]
    a2 = ain[0]
    for j in range(1, e):
        a1 = jnp.where(loc1 == j, ain[j], a1)
        a2 = jnp.where(loc2 == j, ain[j], a2)
    den = a1 + a2
    e1 = grp * e + loc1
    e2 = grp * e + loc2
    eid = _iota(aff.shape, 0)
    is1 = eid == e1
    is2 = eid == e2
    w_ref[...] = jnp.where(is1, a1 / den, jnp.where(is2, a2 / den, 0.0))
    selected = is1 | is2
    self = jnp.where(selected, 1.0, 0.0)
    excl = _dot(self.astype(BF16), tri_ref[...])
    rank_ref[...] = jnp.where(selected, excl, -1.0)
    cnt = jnp.sum(self, axis=1, keepdims=True)
    cnt_ref[...] = jnp.broadcast_to(cnt, cnt_ref.shape)


def _route(logits_t, bias, tri):
    ne, n = logits_t.shape
    tm = tri.shape[0]
    nt = n // tm
    return pl.pallas_call(
        _route_kernel,
        grid=(nt,),
        in_specs=[
            pl.BlockSpec((ne, tm), lambda i: (0, i)),
            pl.BlockSpec((ne, 1), lambda i: (0, 0)),
            pl.BlockSpec((tm, tm), lambda i: (0, 0)),
        ],
        out_specs=[
            pl.BlockSpec((ne, tm), lambda i: (0, i)),
            pl.BlockSpec((ne, tm), lambda i: (0, i)),
            pl.BlockSpec((None, ne, LANES), lambda i: (i, 0, 0)),
        ],
        out_shape=[
            jax.ShapeDtypeStruct((ne, n), F32),
            jax.ShapeDtypeStruct((ne, n), F32),
            jax.ShapeDtypeStruct((nt, ne, LANES), F32),
        ],
        compiler_params=_cparams(("arbitrary",)),
        name="route",
    )(logits_t, bias, tri)


def _moe_kernel(cnt_ref, h_ref, rank_ref, w_ref, wg_ref, wu_ref, wd_ref, x1_ref, mod_ref, o_ref,
                acc_scr, obuf, ybuf, pend):
    i = pl.program_id(0)
    ep = pl.program_id(1)
    r = MOE_R
    log_r = r.bit_length() - 1

    @pl.when(ep == 0)
    def _():
        acc_scr[...] = jnp.zeros_like(acc_scr)
        pend[0] = 0

    def scatter(onehot, yw):
        acc_scr[...] += lax.dot_general(onehot, yw, (((0,), (0,)), ((), ())), preferred_element_type=F32)

    def select(k, base):
        hit = (rank_ref[k].astype(jnp.int32) - base) == _iota((r, 1), 0)
        wcol = jnp.sum(jnp.where(hit, w_ref[k], 0.0), axis=1, keepdims=True)
        return jnp.where(hit, 1.0, 0.0).astype(BF16), wcol

    def ffn_in(xr, k):
        gate = _dot(xr, wg_ref[k])
        up = _dot(xr, wu_ref[k])
        return (gate * _sigmoid(gate) * up).astype(BF16)

    n0 = (cnt_ref[i * N_EXPERTS + 2 * ep] + (r - 1)) >> log_r
    n1 = (cnt_ref[i * N_EXPERTS + 2 * ep + 1] + (r - 1)) >> log_r
    n_both = jnp.minimum(n0, n1)

    def both(j, c):
        oh0, w0 = select(0, j * r)
        oh1, w1 = select(1, j * r)
        onehot = jnp.concatenate([oh0, oh1], axis=0)
        xr = _dot(onehot, h_ref[...]).astype(BF16)
        a0 = ffn_in(xr[0:r], 0)
        a1 = ffn_in(xr[r:2 * r], 1)
        y0 = _dot(a0, wd_ref[0])
        y1 = _dot(a1, wd_ref[1])
        scatter(onehot, jnp.concatenate([y0 * w0, y1 * w1], axis=0).astype(BF16))
        return c

    lax.fori_loop(0, n_both, both, 0)

    k_rest = jnp.where(n0 > n1, 0, 1)

    def rest(j, c):
        onehot, wcol = select(k_rest, j * r)
        xr = _dot(onehot, h_ref[...]).astype(BF16)
        y = _dot(ffn_in(xr, k_rest), wd_ref[k_rest])
        slot = pend[0]
        off = pl.multiple_of(slot * r, r)
        obuf[pl.ds(off, r), :] = onehot
        ybuf[pl.ds(off, r), :] = (y * wcol).astype(BF16)

        @pl.when(slot == 1)
        def _():
            scatter(obuf[...], ybuf[...])

        pend[0] = 1 - slot
        return c

    lax.fori_loop(n_both, jnp.maximum(n0, n1), rest, 0)

    @pl.when(ep == N_EXPERTS // 2 - 1)
    def _():
        @pl.when(pend[0] == 1)
        def _():
            scatter(obuf[0:r, :], ybuf[0:r, :])

        o_ref[...] = x1_ref[...] + mod_ref[5:6, :] * acc_scr[...]


def _moe(cnt, h, rank_t, w_t, w_gate, w_up, w_down, layer, x1, mod_l, tm, tiles_per_batch):
    n, d = h.shape
    de = w_gate.shape[-1]
    nt = n // tm
    grid_spec = pltpu.PrefetchScalarGridSpec(
        num_scalar_prefetch=1,
        grid=(nt, N_EXPERTS // 2),
        in_specs=[
            pl.BlockSpec((tm, d), lambda i, e, c: (i, 0)),
            pl.BlockSpec((2, 1, tm), lambda i, e, c: (e, 0, i)),
            pl.BlockSpec((2, 1, tm), lambda i, e, c: (e, 0, i)),
            pl.BlockSpec((None, 2, d, de), lambda i, e, c: (layer, e, 0, 0)),
            pl.BlockSpec((None, 2, d, de), lambda i, e, c: (layer, e, 0, 0)),
            pl.BlockSpec((None, 2, de, d), lambda i, e, c: (layer, e, 0, 0)),
            pl.BlockSpec((tm, d), lambda i, e, c: (i, 0)),
            pl.BlockSpec((None, 6, d), lambda i, e, c: (i // tiles_per_batch, 0, 0)),
        ],
        out_specs=pl.BlockSpec((tm, d), lambda i, e, c: (i, 0)),
        scratch_shapes=[pltpu.VMEM((tm, d), F32), pltpu.VMEM((2 * MOE_R, tm), BF16),
                        pltpu.VMEM((2 * MOE_R, d), BF16), pltpu.SMEM((1,), jnp.int32)],
    )
    return pl.pallas_call(
        _moe_kernel,
        grid_spec=grid_spec,
        out_shape=jax.ShapeDtypeStruct((n, d), F32),
        compiler_params=_cparams(("arbitrary", "arbitrary")),
        name="moe",
    )(cnt, h, rank_t.reshape(N_EXPERTS, 1, n), w_t.reshape(N_EXPERTS, 1, n), w_gate, w_up, w_down, x1, mod_l)


def _final_kernel(x_ref, g_ref, o_ref):
    x = x_ref[...]
    ms = jnp.mean(x * x, axis=-1, keepdims=True)
    o_ref[...] = x * lax.rsqrt(ms + RMS_EPS) * g_ref[...]


def _final_norm(x, g):
    n, d = x.shape
    tm = min(1024, n)
    return pl.pallas_call(
        _final_kernel,
        grid=(n // tm,),
        in_specs=[pl.BlockSpec((tm, d), lambda i: (i, 0)), pl.BlockSpec((1, d), lambda i: (0, 0))],
        out_specs=pl.BlockSpec((tm, d), lambda i: (i, 0)),
        out_shape=jax.ShapeDtypeStruct((n, d), F32),
        compiler_params=_cparams(("arbitrary",)),
        name="final_norm",
    )(x, g)


def _pad_cols(w, n):
    return jnp.pad(w, [(0, 0)] * (w.ndim - 1) + [(0, n - w.shape[-1])])


def _pad_rows(w, n):
    return jnp.pad(w, [(0, 0)] * (w.ndim - 2) + [(0, n - w.shape[-2]), (0, 0)])


def _layout_in_cols(w):
    rw = RWKV_W
    o = 3 * rw
    parts = [w[..., 0:o], _pad_cols(w[..., o:o + 64], LANES), _pad_cols(w[..., o + 64:o + 128], LANES),
             w[..., o + 128:o + 256]]
    o += 256
    parts.append(w[..., o:o + SB_COLS])
    o += SB_COLS
    parts.append(_pad_cols(w[..., o:], GLA_COLS_PAD))
    return jnp.concatenate(parts, axis=-1)


def kernel(x, c, rms_mix_g, rms_ffn_g, w_mod, b_mod, w_in, w_out, rwkv_mu, rwkv_w0, rwkv_w2, rwkv_a0, rwkv_a2, rwkv_g2, rwkv_k_k, rwkv_k_a, rwkv_r_k, rwkv_lnx_w, rwkv_lnx_b, sb_norm_g, gla_gk_up, gla_gk_b, gla_norm_g, w_router, router_bias, w_gate, w_up, w_down, final_g):
    bsz, s, d = x.shape
    depth = w_in.shape[0]
    n = bsz * s

    mod = _modulation(c, w_mod, b_mod).reshape(depth, bsz, 6, d)
    w_in_l = _layout_in_cols(w_in).astype(BF16)
    w_out_b = w_out.astype(BF16)
    mu_l = _layout_in_cols(jnp.pad(rwkv_mu, ((0, 0), (0, w_in.shape[-1] - rwkv_mu.shape[-1]))))[:, :RWKV_COLS_PAD]
    w2_p = _pad_rows(rwkv_w2, LANES)
    a2_p = _pad_rows(rwkv_a2, LANES)
    up_p = _pad_rows(gla_gk_up, LANES)
    gla_ng = jnp.tile(gla_norm_g, (1, GLA_HEADS))
    bd_rwkv = _block_diag_const(2 * LANES, HEAD_DIM, 1.0)
    bd_sb = _block_diag_const(LANES, HEAD_DIM, 1.0)
    bd_gla = _block_diag_const(GLA_VW, HEAD_DIM, 1.0)
    expand = (jnp.arange(GLA_KW)[:, None] // GLA_KD == jnp.arange(GLA_VW)[None, :] // HEAD_DIM).astype(BF16)
    tm_moe = min(MOE_TM, s)
    tri = (jnp.arange(tm_moe)[:, None] < jnp.arange(tm_moe)[None, :]).astype(BF16)
    w_router_t = w_router.T
    bias_col = router_bias.reshape(N_EXPERTS, 1)
    wg_b, wu_b, wd_b = w_gate.astype(BF16), w_up.astype(BF16), w_down.astype(BF16)
    row = lambda a: a.reshape(1, -1)

    for l in range(depth):
        p_r, p_s, p_g = _inproj(x, mod[l], row(rms_mix_g[l]), w_in_l, l)
        y_r = _rwkv(p_r, row(mu_l[l]), row(rwkv_w0[l]), w2_p[l], row(rwkv_a0[l]), a2_p[l], rwkv_g2[l],
                    row(rwkv_k_k[l]), row(rwkv_k_a[l]), row(rwkv_r_k[l]), row(rwkv_lnx_w[l]),
                    row(rwkv_lnx_b[l]), bd_rwkv)
        y_s = _sb(p_s, row(sb_norm_g[l]), bd_sb)
        y_g = _gla(p_g, up_p[l], row(gla_gk_b[l]), row(gla_ng[l]), bd_gla, expand)
        x1, h2, logits_t = _outproj(y_r, y_s, y_g, x, mod[l], row(rms_ffn_g[l]), w_out_b, l, w_router_t)
        w_t, rank_t, cnt = _route(logits_t, bias_col, tri)
        cnt_i = cnt[:, :, 0].astype(jnp.int32).reshape(-1)
        x = _moe(cnt_i, h2.reshape(n, d), rank_t, w_t, wg_b, wu_b, wd_b, l, x1.reshape(n, d),
                 mod[l], tm_moe, s // tm_moe).reshape(bsz, s, d)
    return _final_norm(x.reshape(n, d), row(final_g)).reshape(bsz, s, d)
```

```python
import functools

import jax
import jax.numpy as jnp
from jax import lax
from jax.experimental import pallas as pl
from jax.experimental.pallas import tpu as pltpu

F32 = jnp.float32
BF16 = jnp.bfloat16

LANES = 128
HEAD_DIM = 64
RWKV_W = 512
RWKV_PAIRS = RWKV_W // LANES
RWKV_COLS_PAD = 3 * RWKV_W + 3 * LANES
RWKV_CHUNK = 64
RWKV_G = 4
RWKV_GN_EPS = 64e-5
SB_W = 256
SB_COLS = 3 * SB_W
SB_TQ = 512
SB_TK = 128
SB_DEAD = -104.0
GLA_VW = 256
GLA_KW = 128
GLA_HEADS = 4
GLA_KD = 32
GLA_LORA = 16
GLA_COLS_PAD = 2 * GLA_KW + 2 * GLA_VW + LANES
GLA_CHUNK = 64
GLA_SUB = 16
GLA_G = 4
GLA_NORMALIZER = 16.0
IN_COLS_PAD = RWKV_COLS_PAD + SB_COLS + GLA_COLS_PAD
HEAD_NORM_EPS = 1e-5
RMS_EPS = 1e-6
N_EXPERTS = 16
EXPERTS_PER_GROUP = 4
N_GROUPS = 4
OUTPROJ_SPLIT = 4
MOE_TM = 1024
MOE_R = 128
VMEM_LIMIT = 48 * 1024 * 1024


def _dot(a, b):
    return jnp.dot(a, b, preferred_element_type=F32)


def _dot_nt(a, b):
    return lax.dot_general(a, b, (((1,), (1,)), ((), ())), preferred_element_type=F32)


def _split2(x):
    hi = x.astype(BF16)
    lo = (x - hi.astype(F32)).astype(BF16)
    return hi, lo


def _split3(x):
    hi = x.astype(BF16)
    r = x - hi.astype(F32)
    mid = r.astype(BF16)
    lo = (r - mid.astype(F32)).astype(BF16)
    return hi, mid, lo


def _dot_seg(x, m):
    hi, lo = _split2(x)
    wm = m.shape[0]
    cols = [_dot(hi[:, c:c + wm], m) + _dot(lo[:, c:c + wm], m) for c in range(0, x.shape[1], wm)]
    return cols[0] if len(cols) == 1 else jnp.concatenate(cols, axis=1)


def _dot_exact_lhs(m, x):
    hi, mid, lo = _split3(x)
    return _dot(m, hi) + _dot(m, mid) + _dot(m, lo)


def _dot_f32(a, b):
    ah, al = _split2(a)
    bh, bl = _split2(b)
    return _dot(ah, bh) + _dot(ah, bl) + _dot(al, bh)


def _dot_nt_f32(a, b):
    ah, al = _split2(a)
    bh, bl = _split2(b)
    return _dot_nt(ah, bh) + _dot_nt(ah, bl) + _dot_nt(al, bh)


def _softplus(x):
    return jnp.maximum(x, 0.0) + jnp.log(1.0 + jnp.exp(-jnp.abs(x)))


def _log_sigmoid(x):
    return jnp.minimum(x, 0.0) - jnp.log(1.0 + jnp.exp(-jnp.abs(x)))


def _sigmoid(x):
    return 1.0 / (1.0 + jnp.exp(-x))


def _iota(shape, dim):
    return lax.broadcasted_iota(jnp.int32, shape, dim)


def _block_diag_const(n, blk, val):
    i = jnp.arange(n)
    return jnp.where((i[:, None] // blk) == (i[None, :] // blk), val, 0.0).astype(BF16)


def _cparams(sem):
    return pltpu.CompilerParams(dimension_semantics=sem, vmem_limit_bytes=VMEM_LIMIT)


def _mod_kernel(c_ref, w_ref, b_ref, o_ref):
    c = c_ref[...]
    ca = c * _sigmoid(c)
    o_ref[...] = _dot_f32(ca, w_ref[...]) + b_ref[...]


def _modulation(c, w_mod, b_mod):
    depth, d, six_d = w_mod.shape
    bsz = c.shape[0]
    tn = 1536
    return pl.pallas_call(
        _mod_kernel,
        grid=(depth, six_d // tn),
        in_specs=[
            pl.BlockSpec((bsz, d), lambda l, j: (0, 0)),
            pl.BlockSpec((None, d, tn), lambda l, j: (l, 0, j)),
            pl.BlockSpec((None, 1, tn), lambda l, j: (l, 0, j)),
        ],
        out_specs=pl.BlockSpec((None, bsz, tn), lambda l, j: (l, 0, j)),
        out_shape=jax.ShapeDtypeStruct((depth, bsz, six_d), F32),
        compiler_params=_cparams(("arbitrary", "arbitrary")),
        name="adaln_mod",
    )(c, w_mod, b_mod.reshape(depth, 1, six_d))


def _rms_mod(x, g, scale, shift):
    ms = jnp.mean(x * x, axis=-1, keepdims=True)
    return x * lax.rsqrt(ms + RMS_EPS) * g * (1.0 + scale) + shift


def _inproj_kernel(x_ref, mod_ref, g_ref, w_ref, pr_ref, ps_ref, pg_ref):
    mod = mod_ref[...]
    h = _rms_mod(x_ref[...], g_ref[...], mod[1:2], mod[0:1]).astype(BF16)
    o1 = RWKV_COLS_PAD
    o2 = o1 + SB_COLS
    pr_ref[...] = _dot(h, w_ref[:, 0:o1])
    ps_ref[...] = _dot(h, w_ref[:, o1:o2]).astype(BF16)
    pg_ref[...] = _dot(h, w_ref[:, o2:IN_COLS_PAD])


def _inproj(x, mod_l, g, w, layer):
    bsz, s, d = x.shape
    tm = min(512, s)
    return pl.pallas_call(
        _inproj_kernel,
        grid=(bsz, s // tm),
        in_specs=[
            pl.BlockSpec((None, tm, d), lambda b, i: (b, i, 0)),
            pl.BlockSpec((None, 6, d), lambda b, i: (b, 0, 0)),
            pl.BlockSpec((1, d), lambda b, i: (0, 0)),
            pl.BlockSpec((None, d, IN_COLS_PAD), lambda b, i: (layer, 0, 0)),
        ],
        out_specs=[
            pl.BlockSpec((None, tm, RWKV_COLS_PAD), lambda b, i: (b, i, 0)),
            pl.BlockSpec((None, tm, SB_COLS), lambda b, i: (b, i, 0)),
            pl.BlockSpec((None, tm, GLA_COLS_PAD), lambda b, i: (b, i, 0)),
        ],
        out_shape=[
            jax.ShapeDtypeStruct((bsz, s, RWKV_COLS_PAD), F32),
            jax.ShapeDtypeStruct((bsz, s, SB_COLS), BF16),
            jax.ShapeDtypeStruct((bsz, s, GLA_COLS_PAD), F32),
        ],
        compiler_params=_cparams(("arbitrary", "arbitrary")),
        name="inproj",
    )(x, mod_l, g, w)


def _rwkv_kernel(p_ref, mu_ref, w0_ref, w2_ref, a0_ref, a2_ref, g2_ref, kk_ref, ka_ref, rk_ref,
                 lnw_ref, lnb_ref, bd_ref, o_ref, ht_scr, prev_scr, *, ng):
    t = RWKV_CHUNK
    rows = ng * t

    @pl.when(pl.program_id(1) == 0)
    def _():
        ht_scr[...] = jnp.zeros_like(ht_scr)
        prev_scr[...] = jnp.zeros_like(prev_scr)

    p = p_ref[...].reshape(rows, RWKV_COLS_PAD)
    row = _iota((rows, 1), 0)
    prev = pltpu.roll(p, 1, axis=0)
    for gi in range(ng):
        prev = jnp.where(row == gi * t, prev_scr[gi, 0:1, :], prev)
        prev_scr[gi, 0:1, :] = p[gi * t + t - 1:gi * t + t, :]
    xm = p + (prev - p) * mu_ref[...]
    w = RWKV_W
    r = xm[:, 0:w]
    k = xm[:, w:2 * w]
    v = xm[:, 2 * w:3 * w]
    xw = xm[:, 3 * w:3 * w + LANES]
    xa = xm[:, 3 * w + LANES:3 * w + 2 * LANES]
    xg = xm[:, 3 * w + 2 * LANES:3 * w + 3 * LANES]
    w_log = -_softplus(-(w0_ref[...] + _dot_f32(jnp.tanh(xw), w2_ref[...]))) - 0.5
    lw = -jnp.exp(w_log)
    iclr = _sigmoid(a0_ref[...] + _dot_f32(xa, a2_ref[...]))
    g = _dot_f32(_sigmoid(xg), g2_ref[...])
    bd = bd_ref[...]
    kkr = k * kk_ref[...]
    ss = _dot_seg(kkr * kkr, bd)
    kk = kkr * lax.rsqrt(jnp.maximum(ss, 1e-24))
    k2 = k * (1.0 + (iclr - 1.0) * ka_ref[...])
    bonus = _dot_seg(r * k2 * rk_ref[...], bd) * v

    ti = _iota((rows, rows), 0)
    tj = _iota((rows, rows), 1)
    tri_incl = jnp.where(((ti >> 6) == (tj >> 6)) & (ti >= tj), 1.0, 0.0).astype(BF16)
    beta = _dot_exact_lhs(tri_incl, lw)
    gam = jnp.exp(beta)
    gam_inv = jnp.exp(-beta)
    a_t = -kk * jnp.exp(beta - lw)
    r_t = r * gam
    b_t = kk * iclr * gam_inv
    k_t = k2 * gam_inv

    lane = _iota((1, LANES), 1)
    m0 = jnp.where(lane < HEAD_DIM, 1.0, 0.0)
    m1 = 1.0 - m0
    n2 = 2 * t
    ii = _iota((n2, n2), 0)
    jj = _iota((n2, n2), 1)
    it = ii & (t - 1)
    jt = jj & (t - 1)
    same64 = (ii >> 6) == (jj >> 6)
    strict = same64 & (it > jt)
    incl = same64 & (it >= jt)
    blk16 = (ii >> 4) == (jj >> 4)
    blk32 = (ii >> 5) == (jj >> 5)
    not16 = jnp.logical_not(blk16)
    not32 = jnp.logical_not(blk32)
    eye = jnp.where(ii == jj, 1.0, 0.0)

    units = [(gi, pr) for gi in range(ng) for pr in range(RWKV_PAIRS)]
    U = range(len(units))

    def cut(arr, gi, pr):
        return arr[gi * t:(gi + 1) * t, pr * LANES:(pr + 1) * LANES]

    def stack(xa_, xb_, gi, pr):
        ca, cb = cut(xa_, gi, pr), cut(xb_, gi, pr)
        return jnp.concatenate([ca * m0, ca * m1, cb * m0, cb * m1], axis=0).astype(BF16)

    lhs = [stack(a_t, r_t, gi, pr) for gi, pr in units]
    rhs = [stack(b_t, k_t, gi, pr) for gi, pr in units]
    vst = [jnp.concatenate([cut(v, gi, pr) * m0, cut(v, gi, pr) * m1], axis=0) for gi, pr in units]
    vstb = [u.astype(BF16) for u in vst]
    gb = [_dot_nt(lhs[i], rhs[i]) for i in U]
    hts = [ht_scr[i] for i in U]
    p0 = [_dot_nt(lhs[i], hts[i].astype(BF16)) for i in U]
    a_ab = [jnp.where(strict, g_[0:n2, 0:n2], 0.0) for g_ in gb]
    a_ak = [jnp.where(strict, g_[0:n2, n2:2 * n2], 0.0).astype(BF16) for g_ in gb]
    a_r = [jnp.concatenate([jnp.where(incl, g_[n2:2 * n2, 0:n2], 0.0),
                            jnp.where(incl, g_[n2:2 * n2, n2:2 * n2], 0.0)], axis=1).astype(BF16) for g_ in gb]
    d1 = [jnp.where(blk16, a, 0.0) for a in a_ab]
    d1b = [d.astype(BF16) for d in d1]
    x = [eye + d for d in d1]
    d2b = [_dot(d, d).astype(BF16) for d in d1b]
    x = [x[i] + _dot(x[i].astype(BF16), d2b[i]) for i in U]
    d4b = [_dot(d, d).astype(BF16) for d in d2b]
    x = [x[i] + _dot(x[i].astype(BF16), d4b[i]) for i in U]
    d8b = [_dot(d, d).astype(BF16) for d in d4b]
    x = [x[i] + _dot(x[i].astype(BF16), d8b[i]) for i in U]
    e32 = [jnp.where(blk32 & not16, a, 0.0).astype(BF16) for a in a_ab]
    xb = [u.astype(BF16) for u in x]
    t1 = [_dot(e32[i], xb[i]).astype(BF16) for i in U]
    x = [x[i] + _dot(xb[i], t1[i]) for i in U]
    e64 = [jnp.where(not32, a, 0.0).astype(BF16) for a in a_ab]
    xb = [u.astype(BF16) for u in x]
    t2 = [_dot(e64[i], xb[i]).astype(BF16) for i in U]
    x = [x[i] + _dot(xb[i], t2[i]) for i in U]
    rhs_u = [p0[i][0:n2] + _dot(a_ak[i], vstb[i]) for i in U]
    ust = [_dot(x[i].astype(BF16), rhs_u[i].astype(BF16)) for i in U]
    uv = [jnp.concatenate([ust[i], vst[i]], axis=0) for i in U]
    yst = [p0[i][n2:2 * n2] + _dot(a_r[i], uv[i].astype(BF16)) for i in U]
    ys = [u[0:t] + u[t:n2] for u in yst]
    upd = [_dot(uv[i].T.astype(BF16), rhs[i]) for i in U]
    for i, (gi, pr) in enumerate(units):
        last = gi * t + t - 1
        ht_scr[i] = (hts[i] + upd[i]) * gam[last:last + 1, pr * LANES:(pr + 1) * LANES]
    y = jnp.concatenate([jnp.concatenate(ys[gi * RWKV_PAIRS:(gi + 1) * RWKV_PAIRS], axis=1)
                         for gi in range(ng)], axis=0)
    inv_n = 1.0 / HEAD_DIM
    mean = _dot_seg(y, bd) * inv_n
    yc = y - mean
    var = _dot_seg(yc * yc, bd) * inv_n
    yn = yc * lax.rsqrt(var + RWKV_GN_EPS) * lnw_ref[...] + lnb_ref[...]
    o_ref[...] = ((yn + bonus) * g).astype(o_ref.dtype).reshape(ng, t, RWKV_W)


def _rwkv(p_r, mu, w0, w2, a0, a2, g2, k_k, k_a, r_k, lnw, lnb, bd):
    bsz, s, _ = p_r.shape
    t = RWKV_CHUNK
    ng = RWKV_G if bsz % RWKV_G == 0 else 1
    vec = lambda n: pl.BlockSpec((1, n), lambda b, i: (0, 0))
    mat = lambda m, n: pl.BlockSpec((m, n), lambda b, i: (0, 0))
    return pl.pallas_call(
        functools.partial(_rwkv_kernel, ng=ng),
        grid=(bsz // ng, s // t),
        in_specs=[
            pl.BlockSpec((ng, t, RWKV_COLS_PAD), lambda b, i: (b, i, 0)),
            vec(RWKV_COLS_PAD), vec(RWKV_W), mat(LANES, RWKV_W), vec(RWKV_W), mat(LANES, RWKV_W),
            mat(LANES, RWKV_W), vec(RWKV_W), vec(RWKV_W), vec(RWKV_W), vec(RWKV_W), vec(RWKV_W),
            mat(2 * LANES, 2 * LANES),
        ],
        out_specs=pl.BlockSpec((ng, t, RWKV_W), lambda b, i: (b, i, 0)),
        out_shape=jax.ShapeDtypeStruct((bsz, s, RWKV_W), BF16),
        scratch_shapes=[pltpu.VMEM((ng * RWKV_PAIRS, LANES, LANES), F32),
                        pltpu.VMEM((ng, 8, RWKV_COLS_PAD), F32)],
        compiler_params=_cparams(("arbitrary", "arbitrary")),
        name="rwkv7",
    )(p_r, mu, w0, w2, a0, a2, g2, k_k, k_a, r_k, lnw, lnb, bd)


def _sb_kernel(q_ref, k_ref, v_ref, g_ref, bd_ref, sfx_ref, o_ref, acc_scr, carry_scr, *, tq):
    tk = SB_TK
    nsub = tq // tk
    nh = SB_W // HEAD_DIM
    heads = range(nh)
    qi = pl.program_id(1)
    lane = _iota((1, SB_W), 1)
    hm = [jnp.where((lane >> 6) == h, 1.0, 0.0).astype(BF16) for h in heads]
    q = q_ref[...] * jnp.asarray(HEAD_DIM ** -0.5, BF16)
    qh = [q * hm[h] for h in heads]
    suffix = sfx_ref[...]
    acc_scr[...] = jnp.zeros_like(acc_scr)
    carry_scr[...] = jnp.zeros_like(carry_scr)

    def block(j, r0, diagonal):
        off = pl.multiple_of(j * tk, tk)
        kb = k_ref[pl.ds(off, tk), :]
        vb = v_ref[pl.ds(off, tk), :]
        vcat = jnp.concatenate([vb * hm[h] for h in heads], axis=0)
        z = [_dot_nt(qh[h][r0:tq], kb) for h in heads]
        l1p = [jnp.log(1.0 + jnp.exp(-jnp.abs(u))) for u in z]
        lk = [-(jnp.maximum(z[h], 0.0) + l1p[h]) for h in heads]
        lsig = [jnp.minimum(z[h], 0.0) - l1p[h] for h in heads]
        if diagonal:
            rows = tq - r0
            causal = _iota((rows, tk), 1) < _iota((rows, tk), 0)
            lk = [jnp.where(causal, u, 0.0) for u in lk]
        cs = [_dot(lk[h].astype(BF16), suffix) for h in heads]
        carry = [carry_scr[h, r0:tq, :] for h in heads]
        wgt = [jnp.exp(lsig[h] + cs[h][:, 0:tk] + carry[h]) for h in heads]
        if diagonal:
            wgt = [jnp.where(causal, u, 0.0) for u in wgt]
        for h in heads:
            carry_scr[h, r0:tq, :] = carry[h] + cs[h][:, tk:2 * tk]
        wcat = jnp.concatenate([u.astype(BF16) for u in wgt], axis=1)
        acc_scr[r0:tq, :] += _dot(wcat, vcat)

    for jd in reversed(range(nsub)):
        block(qi * nsub + jd, jd * tk, True)

    def alive():
        m = carry_scr[0]
        for h in range(1, nh):
            m = jnp.maximum(m, carry_scr[h])
        return (jnp.max(m) > SB_DEAD).astype(jnp.int32)

    def cond(c):
        return (c[0] < qi * nsub) & (c[1] > 0)

    def body(c):
        block(qi * nsub - 1 - c[0], 0, False)
        return c[0] + 1, alive()

    lax.while_loop(cond, body, (jnp.int32(0), alive()))
    o = acc_scr[...]
    ms = _dot_seg(o * o, bd_ref[...]) * (1.0 / HEAD_DIM)
    o_ref[...] = (o * lax.rsqrt(ms + HEAD_NORM_EPS) * g_ref[...]).astype(o_ref.dtype)


def _sb(p_s, norm_g, bd):
    bsz, s, _ = p_s.shape
    tq = min(SB_TQ, s)
    si = jnp.arange(SB_TK)[:, None]
    sj = jnp.arange(2 * SB_TK)[None, :]
    sfx = ((si > sj) | (sj >= SB_TK)).astype(BF16)
    return pl.pallas_call(
        functools.partial(_sb_kernel, tq=tq),
        grid=(bsz, s // tq),
        in_specs=[
            pl.BlockSpec((None, tq, SB_W), lambda b, i: (b, i, 0)),
            pl.BlockSpec((None, s, SB_W), lambda b, i: (b, 0, 1)),
            pl.BlockSpec((None, s, SB_W), lambda b, i: (b, 0, 2)),
            pl.BlockSpec((1, SB_W), lambda b, i: (0, 0)),
            pl.BlockSpec((SB_W, SB_W), lambda b, i: (0, 0)),
            pl.BlockSpec((SB_TK, 2 * SB_TK), lambda b, i: (0, 0)),
        ],
        out_specs=pl.BlockSpec((None, tq, SB_W), lambda b, i: (b, i, 0)),
        out_shape=jax.ShapeDtypeStruct((bsz, s, SB_W), BF16),
        scratch_shapes=[pltpu.VMEM((tq, SB_W), F32), pltpu.VMEM((SB_W // HEAD_DIM, tq, LANES), F32)],
        compiler_params=_cparams(("arbitrary", "arbitrary")),
        name="stickbreak",
    )(p_s, p_s, p_s, norm_g, bd, sfx)


def _gla_kernel(p_ref, up_ref, gkb_ref, ng_ref, bdv_ref, exp_ref, o_ref, ht_scr, *, ns):
    t = GLA_CHUNK
    kw, vw = GLA_KW, GLA_VW
    nh = GLA_HEADS
    rows = ns * t
    nsub = t // GLA_SUB
    seqs = range(ns)

    @pl.when(pl.program_id(1) == 0)
    def _():
        ht_scr[...] = jnp.zeros_like(ht_scr)

    p = p_ref[...].reshape(rows, GLA_COLS_PAD)
    q = p[:, 0:kw] * (GLA_KD ** -0.5)
    k = p[:, kw:2 * kw]
    v = p[:, 2 * kw:2 * kw + vw]
    g = p[:, 2 * kw + vw:2 * kw + 2 * vw]
    gk_low = p[:, 2 * kw + 2 * vw:2 * kw + 2 * vw + LANES]
    log_a = _log_sigmoid(_dot_f32(gk_low, up_ref[...]) + gkb_ref[...]) * (1.0 / GLA_NORMALIZER)
    ti = _iota((rows, rows), 0)
    tj = _iota((rows, rows), 1)
    tri_incl = jnp.where(((ti >> 6) == (tj >> 6)) & (ti >= tj), 1.0, 0.0).astype(BF16)
    beta = _dot_exact_lhs(tri_incl, log_a)
    row = _iota((rows, 1), 0)
    seq = row >> 6
    sub = (row & (t - 1)) >> 4

    def srows(a, si):
        return a[si * t:(si + 1) * t]

    beta_last = [beta[si * t + t - 1:si * t + t, :] for si in seqs]
    hts = [ht_scr[si] for si in seqs]
    q_exp = (q * jnp.exp(beta)).astype(BF16)
    o_inter = [_dot_nt(srows(q_exp, si), hts[si].astype(BF16)) for si in seqs]

    ref_rows = [[beta[si * t + GLA_SUB * i - 1:si * t + GLA_SUB * i, :] for i in range(1, nsub)] for si in seqs]
    beta_ref = jnp.zeros_like(beta)
    for si in seqs:
        for i in range(1, nsub):
            beta_ref = jnp.where((seq == si) & (sub == i), ref_rows[si][i - 1], beta_ref)
    q_hat = q * jnp.exp(jnp.minimum(beta - beta_ref, 0.0))
    lane_k = _iota((1, kw), 1)
    lane_v = _iota((1, vw), 1)
    mk = [jnp.where((lane_k >> 5) == h, 1.0, 0.0) for h in range(nh)]
    mv = [jnp.where((lane_v >> 6) == h, 1.0, 0.0) for h in range(nh)]
    q_st = [jnp.concatenate([srows(q_hat, si) * mk[h] for h in range(nh)], axis=0).astype(BF16) for si in seqs]
    v_st = [jnp.concatenate([srows(v, si) * mv[h] for h in range(nh)], axis=0).astype(BF16) for si in seqs]
    n4 = nh * t
    ri = _iota((n4, n4), 0)
    ci = _iota((n4, n4), 1)
    rsub = (ri & (t - 1)) >> 4
    ct = ci & (t - 1)
    attn = [jnp.zeros((n4, n4), F32) for _ in seqs]
    for i in range(1, nsub):
        k_hat = [srows(k, si) * jnp.exp(jnp.minimum(ref_rows[si][i - 1] - srows(beta, si), 0.0)) for si in seqs]
        k_st = [jnp.concatenate([k_hat[si] * mk[h] for h in range(nh)], axis=0).astype(BF16) for si in seqs]
        gi = [_dot_nt(q_st[si], k_st[si]) for si in seqs]
        sel = (rsub == i) & (ct < GLA_SUB * i)
        attn = [jnp.where(sel, gi[si], attn[si]) for si in seqs]
    o_st = [_dot(attn[si].astype(BF16), v_st[si]) for si in seqs]
    o_seq = []
    for si in seqs:
        acc = o_inter[si]
        for h in range(nh):
            acc = acc + o_st[si][h * t:(h + 1) * t]
        o_seq.append(acc)
    o = jnp.concatenate(o_seq, axis=0)

    expand = exp_ref[...]
    tsub = row & (GLA_SUB - 1)
    for d in range(GLA_SUB):
        if d == 0:
            kd, bd_, vd = k, beta, v
        else:
            kd = pltpu.roll(k, d, axis=0)
            bd_ = pltpu.roll(beta, d, axis=0)
            vd = pltpu.roll(v, d, axis=0)
        term = jnp.where(tsub >= d, q * kd * jnp.exp(jnp.minimum(beta - bd_, 0.0)), 0.0)
        o = o + _dot(term.astype(BF16), expand) * vd

    hv = _iota((vw, kw), 0) >> 6
    hk = _iota((vw, kw), 1) >> 5
    k_end = [(srows(k, si) * jnp.exp(beta_last[si] - srows(beta, si))).astype(BF16) for si in seqs]
    upd = [_dot(srows(v, si).T.astype(BF16), k_end[si]) for si in seqs]
    for si in seqs:
        ht_scr[si] = hts[si] * jnp.exp(beta_last[si]) + jnp.where(hv == hk, upd[si], 0.0)

    ms = _dot_seg(o * o, bdv_ref[...]) * (1.0 / HEAD_DIM)
    on = o * lax.rsqrt(ms + HEAD_NORM_EPS) * ng_ref[...]
    o_ref[...] = (on * (g * _sigmoid(g))).astype(o_ref.dtype).reshape(ns, t, vw)


def _gla(p_g, gk_up, gk_b, norm_g, bdv, expand):
    bsz, s, _ = p_g.shape
    t = GLA_CHUNK
    ns = GLA_G if bsz % GLA_G == 0 else 1
    const = lambda m, n: pl.BlockSpec((m, n), lambda b, i: (0, 0))
    return pl.pallas_call(
        functools.partial(_gla_kernel, ns=ns),
        grid=(bsz // ns, s // t),
        in_specs=[
            pl.BlockSpec((ns, t, GLA_COLS_PAD), lambda b, i: (b, i, 0)),
            const(LANES, GLA_KW), const(1, GLA_KW), const(1, GLA_VW), const(GLA_VW, GLA_VW),
            const(GLA_KW, GLA_VW),
        ],
        out_specs=pl.BlockSpec((ns, t, GLA_VW), lambda b, i: (b, i, 0)),
        out_shape=jax.ShapeDtypeStruct((bsz, s, GLA_VW), BF16),
        scratch_shapes=[pltpu.VMEM((ns, GLA_VW, GLA_KW), F32)],
        compiler_params=_cparams(("arbitrary", "arbitrary")),
        name="gla",
    )(p_g, gk_up, gk_b, norm_g, bdv, expand)


def _outproj_kernel(yr_ref, ys_ref, yg_ref, x_ref, mod_ref, g_ref, w_ref, wr_ref, x1_ref, h_ref, lg_ref):
    mod = mod_ref[...]
    o1 = RWKV_W
    o2 = o1 + SB_W
    tm = x_ref.shape[0]
    nq = OUTPROJ_SPLIT if tm % (OUTPROJ_SPLIT * LANES) == 0 else 1
    rq = tm // nq
    parts = [slice(qi * rq, (qi + 1) * rq) for qi in range(nq)]
    mix = [_dot(yr_ref[sl, :], w_ref[0:o1, :]) + _dot(ys_ref[sl, :], w_ref[o1:o2, :])
           + _dot(yg_ref[sl, :], w_ref[o2:, :]) for sl in parts]
    x1 = [x_ref[sl, :] + mod[2:3] * mix[qi] for qi, sl in enumerate(parts)]
    h = [_rms_mod(u, g_ref[...], mod[4:5], mod[3:4]) for u in x1]
    for qi, sl in enumerate(parts):
        x1_ref[sl, :] = x1[qi]
        h_ref[sl, :] = h[qi].astype(BF16)
        lg_ref[:, sl] = _dot_nt_f32(wr_ref[...], h[qi])


def _outproj(y_r, y_s, y_g, x, mod_l, g, w_out, layer, w_router_t):
    bsz, s, d = x.shape
    tm = min(512, s)
    nt = s // tm
    return pl.pallas_call(
        _outproj_kernel,
        grid=(bsz, nt),
        in_specs=[
            pl.BlockSpec((None, tm, RWKV_W), lambda b, i: (b, i, 0)),
            pl.BlockSpec((None, tm, SB_W), lambda b, i: (b, i, 0)),
            pl.BlockSpec((None, tm, GLA_VW), lambda b, i: (b, i, 0)),
            pl.BlockSpec((None, tm, d), lambda b, i: (b, i, 0)),
            pl.BlockSpec((None, 6, d), lambda b, i: (b, 0, 0)),
            pl.BlockSpec((1, d), lambda b, i: (0, 0)),
            pl.BlockSpec((None, d, d), lambda b, i: (layer, 0, 0)),
            pl.BlockSpec((N_EXPERTS, d), lambda b, i: (0, 0)),
        ],
        out_specs=[
            pl.BlockSpec((None, tm, d), lambda b, i: (b, i, 0)),
            pl.BlockSpec((None, tm, d), lambda b, i: (b, i, 0)),
            pl.BlockSpec((N_EXPERTS, tm), lambda b, i: (0, b * nt + i)),
        ],
        out_shape=[
            jax.ShapeDtypeStruct((bsz, s, d), F32),
            jax.ShapeDtypeStruct((bsz, s, d), BF16),
            jax.ShapeDtypeStruct((N_EXPERTS, bsz * s), F32),
        ],
        compiler_params=_cparams(("arbitrary", "arbitrary")),
        name="outproj",
    )(y_r, y_s, y_g, x, mod_l, g, w_out, w_router_t)


def _route_kernel(lg_ref, bias_ref, tri_ref, w_ref, rank_ref, cnt_ref):
    aff = _sigmoid(lg_ref[...])
    sel = aff + bias_ref[...]
    e = EXPERTS_PER_GROUP
    rows = [sel[i:i + 1, :] for i in range(N_EXPERTS)]
    arow = [aff[i:i + 1, :] for i in range(N_EXPERTS)]
    scores = []
    for gi in range(N_GROUPS):
        a, b, c, d = rows[e * gi:e * gi + e]
        scores.append(jnp.maximum(jnp.maximum(jnp.maximum(a + b, a + c), jnp.maximum(a + d, b + c)),
                                  jnp.maximum(b + d, c + d)))
    grp = jnp.zeros_like(scores[0]).astype(jnp.int32)
    best = scores[0]
    for gi in range(1, N_GROUPS):
        better = scores[gi] > best
        grp = jnp.where(better, gi, grp)
        best = jnp.where(better, scores[gi], best)
    sin, ain = [], []
    for j in range(e):
        sv, av = rows[j], arow[j]
        for gi in range(1, N_GROUPS):
            sv = jnp.where(grp == gi, rows[e * gi + j], sv)
            av = jnp.where(grp == gi, arow[e * gi + j], av)
        sin.append(sv)
        ain.append(av)
    loc1 = jnp.zeros_like(grp)
    b1 = sin[0]
    for j in range(1, e):
        better = sin[j] > b1
        loc1 = jnp.where(better, j, loc1)
        b1 = jnp.where(better, sin[j], b1)
    neg = jnp.full_like(b1, -jnp.inf)
    loc2 = jnp.zeros_like(grp)
    b2 = neg
    for j in range(e):
        cand = jnp.where(loc1 == j, neg, sin[j])
        better = cand > b2
        loc2 = jnp.where(better, j, loc2)
        b2 = jnp.where(better, cand, b2)
    a1 = ain[0]
    a2 = ain[0]
    for j in range(1, e):
        a1 = jnp.where(loc1 == j, ain[j], a1)
        a2 = jnp.where(loc2 == j, ain[j], a2)
    den = a1 + a2
    e1 = grp * e + loc1
    e2 = grp * e + loc2
    eid = _iota(aff.shape, 0)
    is1 = eid == e1
    is2 = eid == e2
    w_ref[...] = jnp.where(is1, a1 / den, jnp.where(is2, a2 / den, 0.0))
    selected = is1 | is2
    self = jnp.where(selected, 1.0, 0.0)
    excl = _dot(self.astype(BF16), tri_ref[...])
    rank_ref[...] = jnp.where(selected, excl, -1.0)
    cnt = jnp.sum(self, axis=1, keepdims=True)
    cnt_ref[...] = jnp.broadcast_to(cnt, cnt_ref.shape)


def _route(logits_t, bias, tri):
    ne, n = logits_t.shape
    tm = tri.shape[0]
    nt = n // tm
    return pl.pallas_call(
        _route_kernel,
        grid=(nt,),
        in_specs=[
            pl.BlockSpec((ne, tm), lambda i: (0, i)),
            pl.BlockSpec((ne, 1), lambda i: (0, 0)),
            pl.BlockSpec((tm, tm), lambda i: (0, 0)),
        ],
        out_specs=[
            pl.BlockSpec((ne, tm), lambda i: (0, i)),
            pl.BlockSpec((ne, tm), lambda i: (0, i)),
            pl.BlockSpec((None, ne, LANES), lambda i: (i, 0, 0)),
        ],
        out_shape=[
            jax.ShapeDtypeStruct((ne, n), F32),
            jax.ShapeDtypeStruct((ne, n), F32),
            jax.ShapeDtypeStruct((nt, ne, LANES), F32),
        ],
        compiler_params=_cparams(("arbitrary",)),
        name="route",
    )(logits_t, bias, tri)


def _moe_kernel(cnt_ref, h_ref, rank_ref, w_ref, wg_ref, wu_ref, wd_ref, x1_ref, mod_ref, o_ref,
                acc_scr, obuf, ybuf, pend):
    i = pl.program_id(0)
    ep = pl.program_id(1)
    r = MOE_R
    log_r = r.bit_length() - 1

    @pl.when(ep == 0)
    def _():
        acc_scr[...] = jnp.zeros_like(acc_scr)
        pend[0] = 0

    def scatter(onehot, yw):
        acc_scr[...] += lax.dot_general(onehot, yw, (((0,), (0,)), ((), ())), preferred_element_type=F32)

    def select(k, base):
        hit = (rank_ref[k].astype(jnp.int32) - base) == _iota((r, 1), 0)
        wcol = jnp.sum(jnp.where(hit, w_ref[k], 0.0), axis=1, keepdims=True)
        return jnp.where(hit, 1.0, 0.0).astype(BF16), wcol

    def ffn_in(xr, k):
        gate = _dot(xr, wg_ref[k])
        up = _dot(xr, wu_ref[k])
        return (gate * _sigmoid(gate) * up).astype(BF16)

    n0 = (cnt_ref[i * N_EXPERTS + 2 * ep] + (r - 1)) >> log_r
    n1 = (cnt_ref[i * N_EXPERTS + 2 * ep + 1] + (r - 1)) >> log_r
    n_both = jnp.minimum(n0, n1)

    def both(j, c):
        oh0, w0 = select(0, j * r)
        oh1, w1 = select(1, j * r)
        onehot = jnp.concatenate([oh0, oh1], axis=0)
        xr = _dot(onehot, h_ref[...]).astype(BF16)
        a0 = ffn_in(xr[0:r], 0)
        a1 = ffn_in(xr[r:2 * r], 1)
        y0 = _dot(a0, wd_ref[0])
        y1 = _dot(a1, wd_ref[1])
        scatter(onehot, jnp.concatenate([y0 * w0, y1 * w1], axis=0).astype(BF16))
        return c

    lax.fori_loop(0, n_both, both, 0)

    k_rest = jnp.where(n0 > n1, 0, 1)

    def rest(j, c):
        onehot, wcol = select(k_rest, j * r)
        xr = _dot(onehot, h_ref[...]).astype(BF16)
        y = _dot(ffn_in(xr, k_rest), wd_ref[k_rest])
        slot = pend[0]
        off = pl.multiple_of(slot * r, r)
        obuf[pl.ds(off, r), :] = onehot
        ybuf[pl.ds(off, r), :] = (y * wcol).astype(BF16)

        @pl.when(slot == 1)
        def _():
            scatter(obuf[...], ybuf[...])

        pend[0] = 1 - slot
        return c

    lax.fori_loop(n_both, jnp.maximum(n0, n1), rest, 0)

    @pl.when(ep == N_EXPERTS // 2 - 1)
    def _():
        @pl.when(pend[0] == 1)
        def _():
            scatter(obuf[0:r, :], ybuf[0:r, :])

        o_ref[...] = x1_ref[...] + mod_ref[5:6, :] * acc_scr[...]


def _moe(cnt, h, rank_t, w_t, w_gate, w_up, w_down, layer, x1, mod_l, tm, tiles_per_batch):
    n, d = h.shape
    de = w_gate.shape[-1]
    nt = n // tm
    grid_spec = pltpu.PrefetchScalarGridSpec(
        num_scalar_prefetch=1,
        grid=(nt, N_EXPERTS // 2),
        in_specs=[
            pl.BlockSpec((tm, d), lambda i, e, c: (i, 0)),
            pl.BlockSpec((2, 1, tm), lambda i, e, c: (e, 0, i)),
            pl.BlockSpec((2, 1, tm), lambda i, e, c: (e, 0, i)),
            pl.BlockSpec((None, 2, d, de), lambda i, e, c: (layer, e, 0, 0)),
            pl.BlockSpec((None, 2, d, de), lambda i, e, c: (layer, e, 0, 0)),
            pl.BlockSpec((None, 2, de, d), lambda i, e, c: (layer, e, 0, 0)),
            pl.BlockSpec((tm, d), lambda i, e, c: (i, 0)),
            pl.BlockSpec((None, 6, d), lambda i, e, c: (i // tiles_per_batch, 0, 0)),
        ],
        out_specs=pl.BlockSpec((tm, d), lambda i, e, c: (i, 0)),
        scratch_shapes=[pltpu.VMEM((tm, d), F32), pltpu.VMEM((2 * MOE_R, tm), BF16),
                        pltpu.VMEM((2 * MOE_R, d), BF16), pltpu.SMEM((1,), jnp.int32)],
    )
    return pl.pallas_call(
        _moe_kernel,
        grid_spec=grid_spec,
        out_shape=jax.ShapeDtypeStruct((n, d), F32),
        compiler_params=_cparams(("arbitrary", "arbitrary")),
        name="moe",
    )(cnt, h, rank_t.reshape(N_EXPERTS, 1, n), w_t.reshape(N_EXPERTS, 1, n), w_gate, w_up, w_down, x1, mod_l)


def _final_kernel(x_ref, g_ref, o_ref):
    x = x_ref[...]
    ms = jnp.mean(x * x, axis=-1, keepdims=True)
    o_ref[...] = x * lax.rsqrt(ms + RMS_EPS) * g_ref[...]


def _final_norm(x, g):
    n, d = x.shape
    tm = min(1024, n)
    return pl.pallas_call(
        _final_kernel,
        grid=(n // tm,),
        in_specs=[pl.BlockSpec((tm, d), lambda i: (i, 0)), pl.BlockSpec((1, d), lambda i: (0, 0))],
        out_specs=pl.BlockSpec((tm, d), lambda i: (i, 0)),
        out_shape=jax.ShapeDtypeStruct((n, d), F32),
        compiler_params=_cparams(("arbitrary",)),
        name="final_norm",
    )(x, g)


def _pad_cols(w, n):
    return jnp.pad(w, [(0, 0)] * (w.ndim - 1) + [(0, n - w.shape[-1])])


def _pad_rows(w, n):
    return jnp.pad(w, [(0, 0)] * (w.ndim - 2) + [(0, n - w.shape[-2]), (0, 0)])


def _layout_in_cols(w):
    rw = RWKV_W
    o = 3 * rw
    parts = [w[..., 0:o], _pad_cols(w[..., o:o + 64], LANES), _pad_cols(w[..., o + 64:o + 128], LANES),
             w[..., o + 128:o + 256]]
    o += 256
    parts.append(w[..., o:o + SB_COLS])
    o += SB_COLS
    parts.append(_pad_cols(w[..., o:], GLA_COLS_PAD))
    return jnp.concatenate(parts, axis=-1)


def kernel(x, c, rms_mix_g, rms_ffn_g, w_mod, b_mod, w_in, w_out, rwkv_mu, rwkv_w0, rwkv_w2, rwkv_a0, rwkv_a2, rwkv_g2, rwkv_k_k, rwkv_k_a, rwkv_r_k, rwkv_lnx_w, rwkv_lnx_b, sb_norm_g, gla_gk_up, gla_gk_b, gla_norm_g, w_router, router_bias, w_gate, w_up, w_down, final_g):
    bsz, s, d = x.shape
    depth = w_in.shape[0]
    n = bsz * s

    mod = _modulation(c, w_mod, b_mod).reshape(depth, bsz, 6, d)
    w_in_l = _layout_in_cols(w_in).astype(BF16)
    w_out_b = w_out.astype(BF16)
    mu_l = _layout_in_cols(jnp.pad(rwkv_mu, ((0, 0), (0, w_in.shape[-1] - rwkv_mu.shape[-1]))))[:, :RWKV_COLS_PAD]
    w2_p = _pad_rows(rwkv_w2, LANES)
    a2_p = _pad_rows(rwkv_a2, LANES)
    up_p = _pad_rows(gla_gk_up, LANES)
    gla_ng = jnp.tile(gla_norm_g, (1, GLA_HEADS))
    bd_rwkv = _block_diag_const(2 * LANES, HEAD_DIM, 1.0)
    bd_sb = _block_diag_const(SB_W, HEAD_DIM, 1.0)
    bd_gla = _block_diag_const(GLA_VW, HEAD_DIM, 1.0)
    expand = (jnp.arange(GLA_KW)[:, None] // GLA_KD == jnp.arange(GLA_VW)[None, :] // HEAD_DIM).astype(BF16)
    tm_moe = min(MOE_TM, s)
    tri = (jnp.arange(tm_moe)[:, None] < jnp.arange(tm_moe)[None, :]).astype(BF16)
    w_router_t = w_router.T
    bias_col = router_bias.reshape(N_EXPERTS, 1)
    wg_b, wu_b, wd_b = w_gate.astype(BF16), w_up.astype(BF16), w_down.astype(BF16)
    row = lambda a: a.reshape(1, -1)

    for l in range(depth):
        p_r, p_s, p_g = _inproj(x, mod[l], row(rms_mix_g[l]), w_in_l, l)
        y_r = _rwkv(p_r, row(mu_l[l]), row(rwkv_w0[l]), w2_p[l], row(rwkv_a0[l]), a2_p[l], rwkv_g2[l],
                    row(rwkv_k_k[l]), row(rwkv_k_a[l]), row(rwkv_r_k[l]), row(rwkv_lnx_w[l]),
                    row(rwkv_lnx_b[l]), bd_rwkv)
        y_s = _sb(p_s, row(sb_norm_g[l]), bd_sb)
        y_g = _gla(p_g, up_p[l], row(gla_gk_b[l]), row(gla_ng[l]), bd_gla, expand)
        x1, h2, logits_t = _outproj(y_r, y_s, y_g, x, mod[l], row(rms_ffn_g[l]), w_out_b, l, w_router_t)
        w_t, rank_t, cnt = _route(logits_t, bias_col, tri)
        cnt_i = cnt[:, :, 0].astype(jnp.int32).reshape(-1)
        x = _moe(cnt_i, h2.reshape(n, d), rank_t, w_t, wg_b, wu_b, wd_b, l, x1.reshape(n, d),
                 mod[l], tm_moe, s // tm_moe).reshape(bsz, s, d)
    return _final_norm(x.reshape(n, d), row(final_g)).reshape(bsz, s, d)
```

```python
import functools

import jax
import jax.numpy as jnp
from jax import lax
from jax.experimental import pallas as pl
from jax.experimental.pallas import tpu as pltpu

F32 = jnp.float32
BF16 = jnp.bfloat16

LANES = 128
HEAD_DIM = 64
RWKV_W = 512
RWKV_PAIRS = RWKV_W // LANES
RWKV_COLS_PAD = 3 * RWKV_W + 3 * LANES
RWKV_CHUNK = 64
RWKV_G = 4
RWKV_GN_EPS = 64e-5
SB_W = 256
SB_COLS = 3 * SB_W
SB_TQ = 512
SB_TK = 128
SB_DEAD = -104.0
GLA_VW = 256
GLA_KW = 128
GLA_HEADS = 4
GLA_KD = 32
GLA_LORA = 16
GLA_COLS_PAD = 2 * GLA_KW + 2 * GLA_VW + LANES
GLA_CHUNK = 64
GLA_SUB = 16
GLA_G = 4
GLA_NORMALIZER = 16.0
IN_COLS_PAD = RWKV_COLS_PAD + SB_COLS + GLA_COLS_PAD
HEAD_NORM_EPS = 1e-5
RMS_EPS = 1e-6
N_EXPERTS = 16
EXPERTS_PER_GROUP = 4
N_GROUPS = 4
OUTPROJ_SPLIT = 4
MOE_TM = 1024
MOE_R = 128
VMEM_LIMIT = 48 * 1024 * 1024


def _dot(a, b):
    return jnp.dot(a, b, preferred_element_type=F32)


def _dot_nt(a, b):
    return lax.dot_general(a, b, (((1,), (1,)), ((), ())), preferred_element_type=F32)


def _split2(x):
    hi = x.astype(BF16)
    lo = (x - hi.astype(F32)).astype(BF16)
    return hi, lo


def _split3(x):
    hi = x.astype(BF16)
    r = x - hi.astype(F32)
    mid = r.astype(BF16)
    lo = (r - mid.astype(F32)).astype(BF16)
    return hi, mid, lo


def _dot_seg(x, m):
    hi, lo = _split2(x)
    wm = m.shape[0]
    cols = [_dot(hi[:, c:c + wm], m) + _dot(lo[:, c:c + wm], m) for c in range(0, x.shape[1], wm)]
    return cols[0] if len(cols) == 1 else jnp.concatenate(cols, axis=1)


def _dot_exact_lhs(m, x):
    hi, mid, lo = _split3(x)
    return _dot(m, hi) + _dot(m, mid) + _dot(m, lo)


def _dot_f32(a, b):
    ah, al = _split2(a)
    bh, bl = _split2(b)
    return _dot(ah, bh) + _dot(ah, bl) + _dot(al, bh)


def _dot_nt_f32(a, b):
    ah, al = _split2(a)
    bh, bl = _split2(b)
    return _dot_nt(ah, bh) + _dot_nt(ah, bl) + _dot_nt(al, bh)


def _softplus(x):
    return jnp.maximum(x, 0.0) + jnp.log(1.0 + jnp.exp(-jnp.abs(x)))


def _log_sigmoid(x):
    return jnp.minimum(x, 0.0) - jnp.log(1.0 + jnp.exp(-jnp.abs(x)))


def _sigmoid(x):
    return 1.0 / (1.0 + jnp.exp(-x))


def _iota(shape, dim):
    return lax.broadcasted_iota(jnp.int32, shape, dim)


def _block_diag_const(n, blk, val):
    i = jnp.arange(n)
    return jnp.where((i[:, None] // blk) == (i[None, :] // blk), val, 0.0).astype(BF16)


def _cparams(sem):
    return pltpu.CompilerParams(dimension_semantics=sem, vmem_limit_bytes=VMEM_LIMIT)


def _mod_kernel(c_ref, w_ref, b_ref, o_ref):
    c = c_ref[...]
    ca = c * _sigmoid(c)
    o_ref[...] = _dot_f32(ca, w_ref[...]) + b_ref[...]


def _modulation(c, w_mod, b_mod):
    depth, d, six_d = w_mod.shape
    bsz = c.shape[0]
    tn = 1536
    return pl.pallas_call(
        _mod_kernel,
        grid=(depth, six_d // tn),
        in_specs=[
            pl.BlockSpec((bsz, d), lambda l, j: (0, 0)),
            pl.BlockSpec((None, d, tn), lambda l, j: (l, 0, j)),
            pl.BlockSpec((None, 1, tn), lambda l, j: (l, 0, j)),
        ],
        out_specs=pl.BlockSpec((None, bsz, tn), lambda l, j: (l, 0, j)),
        out_shape=jax.ShapeDtypeStruct((depth, bsz, six_d), F32),
        compiler_params=_cparams(("arbitrary", "arbitrary")),
        name="adaln_mod",
    )(c, w_mod, b_mod.reshape(depth, 1, six_d))


def _rms_mod(x, g, scale, shift):
    ms = jnp.mean(x * x, axis=-1, keepdims=True)
    return x * lax.rsqrt(ms + RMS_EPS) * g * (1.0 + scale) + shift


def _inproj_kernel(x_ref, mod_ref, g_ref, w_ref, pr_ref, ps_ref, pg_ref):
    mod = mod_ref[...]
    h = _rms_mod(x_ref[...], g_ref[...], mod[1:2], mod[0:1]).astype(BF16)
    o1 = RWKV_COLS_PAD
    o2 = o1 + SB_COLS
    pr_ref[...] = _dot(h, w_ref[:, 0:o1])
    ps_ref[...] = _dot(h, w_ref[:, o1:o2]).astype(BF16)
    pg_ref[...] = _dot(h, w_ref[:, o2:IN_COLS_PAD])


def _inproj(x, mod_l, g, w, layer):
    bsz, s, d = x.shape
    tm = min(512, s)
    return pl.pallas_call(
        _inproj_kernel,
        grid=(bsz, s // tm),
        in_specs=[
            pl.BlockSpec((None, tm, d), lambda b, i: (b, i, 0)),
            pl.BlockSpec((None, 6, d), lambda b, i: (b, 0, 0)),
            pl.BlockSpec((1, d), lambda b, i: (0, 0)),
            pl.BlockSpec((None, d, IN_COLS_PAD), lambda b, i: (layer, 0, 0)),
        ],
        out_specs=[
            pl.BlockSpec((None, tm, RWKV_COLS_PAD), lambda b, i: (b, i, 0)),
            pl.BlockSpec((None, tm, SB_COLS), lambda b, i: (b, i, 0)),
            pl.BlockSpec((None, tm, GLA_COLS_PAD), lambda b, i: (b, i, 0)),
        ],
        out_shape=[
            jax.ShapeDtypeStruct((bsz, s, RWKV_COLS_PAD), F32),
            jax.ShapeDtypeStruct((bsz, s, SB_COLS), BF16),
            jax.ShapeDtypeStruct((bsz, s, GLA_COLS_PAD), F32),
        ],
        compiler_params=_cparams(("arbitrary", "arbitrary")),
        name="inproj",
    )(x, mod_l, g, w)


def _rwkv_kernel(p_ref, mu_ref, w0_ref, w2_ref, a0_ref, a2_ref, g2_ref, kk_ref, ka_ref, rk_ref,
                 lnw_ref, lnb_ref, bd_ref, o_ref, ht_scr, prev_scr, *, ng):
    t = RWKV_CHUNK
    rows = ng * t

    @pl.when(pl.program_id(1) == 0)
    def _():
        ht_scr[...] = jnp.zeros_like(ht_scr)
        prev_scr[...] = jnp.zeros_like(prev_scr)

    p = p_ref[...].reshape(rows, RWKV_COLS_PAD)
    row = _iota((rows, 1), 0)
    prev = pltpu.roll(p, 1, axis=0)
    for gi in range(ng):
        prev = jnp.where(row == gi * t, prev_scr[gi, 0:1, :], prev)
        prev_scr[gi, 0:1, :] = p[gi * t + t - 1:gi * t + t, :]
    xm = p + (prev - p) * mu_ref[...]
    w = RWKV_W
    r = xm[:, 0:w]
    k = xm[:, w:2 * w]
    v = xm[:, 2 * w:3 * w]
    xw = xm[:, 3 * w:3 * w + LANES]
    xa = xm[:, 3 * w + LANES:3 * w + 2 * LANES]
    xg = xm[:, 3 * w + 2 * LANES:3 * w + 3 * LANES]
    w_log = -_softplus(-(w0_ref[...] + _dot_f32(jnp.tanh(xw), w2_ref[...]))) - 0.5
    lw = -jnp.exp(w_log)
    iclr = _sigmoid(a0_ref[...] + _dot_f32(xa, a2_ref[...]))
    g = _dot_f32(_sigmoid(xg), g2_ref[...])
    bd = bd_ref[...]
    kkr = k * kk_ref[...]
    ss = _dot_seg(kkr * kkr, bd)
    kk = kkr * lax.rsqrt(jnp.maximum(ss, 1e-24))
    k2 = k * (1.0 + (iclr - 1.0) * ka_ref[...])
    bonus = _dot_seg(r * k2 * rk_ref[...], bd) * v

    ti = _iota((rows, rows), 0)
    tj = _iota((rows, rows), 1)
    tri_incl = jnp.where(((ti >> 6) == (tj >> 6)) & (ti >= tj), 1.0, 0.0).astype(BF16)
    beta = _dot_exact_lhs(tri_incl, lw)
    gam = jnp.exp(beta)
    gam_inv = jnp.exp(-beta)
    a_t = -kk * jnp.exp(beta - lw)
    r_t = r * gam
    b_t = kk * iclr * gam_inv
    k_t = k2 * gam_inv

    lane = _iota((1, LANES), 1)
    m0 = jnp.where(lane < HEAD_DIM, 1.0, 0.0)
    m1 = 1.0 - m0
    n2 = 2 * t
    ii = _iota((n2, n2), 0)
    jj = _iota((n2, n2), 1)
    it = ii & (t - 1)
    jt = jj & (t - 1)
    same64 = (ii >> 6) == (jj >> 6)
    strict = same64 & (it > jt)
    incl = same64 & (it >= jt)
    blk16 = (ii >> 4) == (jj >> 4)
    blk32 = (ii >> 5) == (jj >> 5)
    not16 = jnp.logical_not(blk16)
    not32 = jnp.logical_not(blk32)
    eye = jnp.where(ii == jj, 1.0, 0.0)

    units = [(gi, pr) for gi in range(ng) for pr in range(RWKV_PAIRS)]
    U = range(len(units))

    def cut(arr, gi, pr):
        return arr[gi * t:(gi + 1) * t, pr * LANES:(pr + 1) * LANES]

    def stack(xa_, xb_, gi, pr):
        ca, cb = cut(xa_, gi, pr), cut(xb_, gi, pr)
        return jnp.concatenate([ca * m0, ca * m1, cb * m0, cb * m1], axis=0).astype(BF16)

    lhs = [stack(a_t, r_t, gi, pr) for gi, pr in units]
    rhs = [stack(b_t, k_t, gi, pr) for gi, pr in units]
    vst = [jnp.concatenate([cut(v, gi, pr) * m0, cut(v, gi, pr) * m1], axis=0) for gi, pr in units]
    vstb = [u.astype(BF16) for u in vst]
    gb = [_dot_nt(lhs[i], rhs[i]) for i in U]
    hts = [ht_scr[i] for i in U]
    p0 = [_dot_nt(lhs[i], hts[i].astype(BF16)) for i in U]
    a_ab = [jnp.where(strict, g_[0:n2, 0:n2], 0.0) for g_ in gb]
    a_ak = [jnp.where(strict, g_[0:n2, n2:2 * n2], 0.0).astype(BF16) for g_ in gb]
    a_r = [jnp.concatenate([jnp.where(incl, g_[n2:2 * n2, 0:n2], 0.0),
                            jnp.where(incl, g_[n2:2 * n2, n2:2 * n2], 0.0)], axis=1).astype(BF16) for g_ in gb]
    d1 = [jnp.where(blk16, a, 0.0) for a in a_ab]
    d1b = [d.astype(BF16) for d in d1]
    x = [eye + d for d in d1]
    d2b = [_dot(d, d).astype(BF16) for d in d1b]
    x = [x[i] + _dot(x[i].astype(BF16), d2b[i]) for i in U]
    d4b = [_dot(d, d).astype(BF16) for d in d2b]
    x = [x[i] + _dot(x[i].astype(BF16), d4b[i]) for i in U]
    d8b = [_dot(d, d).astype(BF16) for d in d4b]
    x = [x[i] + _dot(x[i].astype(BF16), d8b[i]) for i in U]
    e32 = [jnp.where(blk32 & not16, a, 0.0).astype(BF16) for a in a_ab]
    xb = [u.astype(BF16) for u in x]
    t1 = [_dot(e32[i], xb[i]).astype(BF16) for i in U]
    x = [x[i] + _dot(xb[i], t1[i]) for i in U]
    e64 = [jnp.where(not32, a, 0.0).astype(BF16) for a in a_ab]
    xb = [u.astype(BF16) for u in x]
    t2 = [_dot(e64[i], xb[i]).astype(BF16) for i in U]
    x = [x[i] + _dot(xb[i], t2[i]) for i in U]
    rhs_u = [p0[i][0:n2] + _dot(a_ak[i], vstb[i]) for i in U]
    ust = [_dot(x[i].astype(BF16), rhs_u[i].astype(BF16)) for i in U]
    uv = [jnp.concatenate([ust[i], vst[i]], axis=0) for i in U]
    yst = [p0[i][n2:2 * n2] + _dot(a_r[i], uv[i].astype(BF16)) for i in U]
    ys = [u[0:t] + u[t:n2] for u in yst]
    upd = [_dot(uv[i].T.astype(BF16), rhs[i]) for i in U]
    for i, (gi, pr) in enumerate(units):
        last = gi * t + t - 1
        ht_scr[i] = (hts[i] + upd[i]) * gam[last:last + 1, pr * LANES:(pr + 1) * LANES]
    y = jnp.concatenate([jnp.concatenate(ys[gi * RWKV_PAIRS:(gi + 1) * RWKV_PAIRS], axis=1)
                         for gi in range(ng)], axis=0)
    inv_n = 1.0 / HEAD_DIM
    mean = _dot_seg(y, bd) * inv_n
    yc = y - mean
    var = _dot_seg(yc * yc, bd) * inv_n
    yn = yc * lax.rsqrt(var + RWKV_GN_EPS) * lnw_ref[...] + lnb_ref[...]
    o_ref[...] = ((yn + bonus) * g).astype(o_ref.dtype).reshape(ng, t, RWKV_W)


def _rwkv(p_r, mu, w0, w2, a0, a2, g2, k_k, k_a, r_k, lnw, lnb, bd):
    bsz, s, _ = p_r.shape
    t = RWKV_CHUNK
    ng = RWKV_G if bsz % RWKV_G == 0 else 1
    vec = lambda n: pl.BlockSpec((1, n), lambda b, i: (0, 0))
    mat = lambda m, n: pl.BlockSpec((m, n), lambda b, i: (0, 0))
    return pl.pallas_call(
        functools.partial(_rwkv_kernel, ng=ng),
        grid=(bsz // ng, s // t),
        in_specs=[
            pl.BlockSpec((ng, t, RWKV_COLS_PAD), lambda b, i: (b, i, 0)),
            vec(RWKV_COLS_PAD), vec(RWKV_W), mat(LANES, RWKV_W), vec(RWKV_W), mat(LANES, RWKV_W),
            mat(LANES, RWKV_W), vec(RWKV_W), vec(RWKV_W), vec(RWKV_W), vec(RWKV_W), vec(RWKV_W),
            mat(2 * LANES, 2 * LANES),
        ],
        out_specs=pl.BlockSpec((ng, t, RWKV_W), lambda b, i: (b, i, 0)),
        out_shape=jax.ShapeDtypeStruct((bsz, s, RWKV_W), BF16),
        scratch_shapes=[pltpu.VMEM((ng * RWKV_PAIRS, LANES, LANES), F32),
                        pltpu.VMEM((ng, 8, RWKV_COLS_PAD), F32)],
        compiler_params=_cparams(("arbitrary", "arbitrary")),
        name="rwkv7",
    )(p_r, mu, w0, w2, a0, a2, g2, k_k, k_a, r_k, lnw, lnb, bd)


def _sb_kernel(q_ref, k_ref, v_ref, g_ref, bd_ref, sfx_ref, o_ref, acc_scr, carry_scr, *, tq):
    tk = SB_TK
    nsub = tq // tk
    nh = SB_W // HEAD_DIM
    heads = range(nh)
    qi = pl.program_id(1)
    lane = _iota((1, SB_W), 1)
    hm = [jnp.where((lane >> 6) == h, 1.0, 0.0).astype(BF16) for h in heads]
    q = q_ref[...] * jnp.asarray(HEAD_DIM ** -0.5, BF16)
    qh = [q * hm[h] for h in heads]
    suffix = sfx_ref[...]
    acc_scr[...] = jnp.zeros_like(acc_scr)
    carry_scr[...] = jnp.zeros_like(carry_scr)

    def block(j, r0, diagonal):
        off = pl.multiple_of(j * tk, tk)
        kb = k_ref[pl.ds(off, tk), :]
        vb = v_ref[pl.ds(off, tk), :]
        vcat = jnp.concatenate([vb * hm[h] for h in heads], axis=0)
        z = [_dot_nt(qh[h][r0:tq], kb) for h in heads]
        l1p = [jnp.log(1.0 + jnp.exp(-jnp.abs(u))) for u in z]
        lk = [-(jnp.maximum(z[h], 0.0) + l1p[h]) for h in heads]
        lsig = [jnp.minimum(z[h], 0.0) - l1p[h] for h in heads]
        if diagonal:
            rows = tq - r0
            causal = _iota((rows, tk), 1) < _iota((rows, tk), 0)
            lk = [jnp.where(causal, u, 0.0) for u in lk]
        cs = [_dot(lk[h].astype(BF16), suffix) for h in heads]
        carry = [carry_scr[h, r0:tq, :] for h in heads]
        wgt = [jnp.exp(lsig[h] + cs[h][:, 0:tk] + carry[h]) for h in heads]
        if diagonal:
            wgt = [jnp.where(causal, u, 0.0) for u in wgt]
        for h in heads:
            carry_scr[h, r0:tq, :] = carry[h] + cs[h][:, tk:2 * tk]
        wcat = jnp.concatenate([u.astype(BF16) for u in wgt], axis=1)
        acc_scr[r0:tq, :] += _dot(wcat, vcat)

    for jd in reversed(range(nsub)):
        block(qi * nsub + jd, jd * tk, True)

    def alive():
        m = carry_scr[0]
        for h in range(1, nh):
            m = jnp.maximum(m, carry_scr[h])
        return (jnp.max(m) > SB_DEAD).astype(jnp.int32)

    def cond(c):
        return (c[0] < qi * nsub) & (c[1] > 0)

    def body(c):
        block(qi * nsub - 1 - c[0], 0, False)
        return c[0] + 1, alive()

    lax.while_loop(cond, body, (jnp.int32(0), alive()))
    o = acc_scr[...]
    ms = _dot_seg(o * o, bd_ref[...]) * (1.0 / HEAD_DIM)
    o_ref[...] = (o * lax.rsqrt(ms + HEAD_NORM_EPS) * g_ref[...]).astype(o_ref.dtype)


def _sb(p_s, norm_g, bd):
    bsz, s, _ = p_s.shape
    tq = min(SB_TQ, s)
    si = jnp.arange(SB_TK)[:, None]
    sj = jnp.arange(2 * SB_TK)[None, :]
    sfx = ((si > sj) | (sj >= SB_TK)).astype(BF16)
    return pl.pallas_call(
        functools.partial(_sb_kernel, tq=tq),
        grid=(bsz, s // tq),
        in_specs=[
            pl.BlockSpec((None, tq, SB_W), lambda b, i: (b, i, 0)),
            pl.BlockSpec((None, s, SB_W), lambda b, i: (b, 0, 1)),
            pl.BlockSpec((None, s, SB_W), lambda b, i: (b, 0, 2)),
            pl.BlockSpec((1, SB_W), lambda b, i: (0, 0)),
            pl.BlockSpec((SB_W, SB_W), lambda b, i: (0, 0)),
            pl.BlockSpec((SB_TK, 2 * SB_TK), lambda b, i: (0, 0)),
        ],
        out_specs=pl.BlockSpec((None, tq, SB_W), lambda b, i: (b, i, 0)),
        out_shape=jax.ShapeDtypeStruct((bsz, s, SB_W), BF16),
        scratch_shapes=[pltpu.VMEM((tq, SB_W), F32), pltpu.VMEM((SB_W // HEAD_DIM, tq, LANES), F32)],
        compiler_params=_cparams(("arbitrary", "arbitrary")),
        name="stickbreak",
    )(p_s, p_s, p_s, norm_g, bd, sfx)


def _gla_kernel(p_ref, up_ref, gkb_ref, ng_ref, bdv_ref, exp_ref, o_ref, ht_scr, *, ns):
    t = GLA_CHUNK
    kw, vw = GLA_KW, GLA_VW
    nh = GLA_HEADS
    rows = ns * t
    nsub = t // GLA_SUB
    seqs = range(ns)

    @pl.when(pl.program_id(1) == 0)
    def _():
        ht_scr[...] = jnp.zeros_like(ht_scr)

    p = p_ref[...].reshape(rows, GLA_COLS_PAD)
    q = p[:, 0:kw] * (GLA_KD ** -0.5)
    k = p[:, kw:2 * kw]
    v = p[:, 2 * kw:2 * kw + vw]
    g = p[:, 2 * kw + vw:2 * kw + 2 * vw]
    gk_low = p[:, 2 * kw + 2 * vw:2 * kw + 2 * vw + LANES]
    log_a = _log_sigmoid(_dot_f32(gk_low, up_ref[...]) + gkb_ref[...]) * (1.0 / GLA_NORMALIZER)
    ti = _iota((rows, rows), 0)
    tj = _iota((rows, rows), 1)
    tri_incl = jnp.where(((ti >> 6) == (tj >> 6)) & (ti >= tj), 1.0, 0.0).astype(BF16)
    beta = _dot_exact_lhs(tri_incl, log_a)
    row = _iota((rows, 1), 0)
    seq = row >> 6
    sub = (row & (t - 1)) >> 4

    def srows(a, si):
        return a[si * t:(si + 1) * t]

    beta_last = [beta[si * t + t - 1:si * t + t, :] for si in seqs]
    hts = [ht_scr[si] for si in seqs]
    q_exp = (q * jnp.exp(beta)).astype(BF16)
    o_inter = [_dot_nt(srows(q_exp, si), hts[si].astype(BF16)) for si in seqs]

    ref_rows = [[beta[si * t + GLA_SUB * i - 1:si * t + GLA_SUB * i, :] for i in range(1, nsub)] for si in seqs]
    beta_ref = jnp.zeros_like(beta)
    for si in seqs:
        for i in range(1, nsub):
            beta_ref = jnp.where((seq == si) & (sub == i), ref_rows[si][i - 1], beta_ref)
    q_hat = q * jnp.exp(jnp.minimum(beta - beta_ref, 0.0))
    lane_k = _iota((1, kw), 1)
    lane_v = _iota((1, vw), 1)
    mk = [jnp.where((lane_k >> 5) == h, 1.0, 0.0) for h in range(nh)]
    mv = [jnp.where((lane_v >> 6) == h, 1.0, 0.0) for h in range(nh)]
    q_st = [jnp.concatenate([srows(q_hat, si) * mk[h] for h in range(nh)], axis=0).astype(BF16) for si in seqs]
    v_st = [jnp.concatenate([srows(v, si) * mv[h] for h in range(nh)], axis=0).astype(BF16) for si in seqs]
    n4 = nh * t
    ri = _iota((n4, n4), 0)
    ci = _iota((n4, n4), 1)
    rsub = (ri & (t - 1)) >> 4
    ct = ci & (t - 1)
    attn = [jnp.zeros((n4, n4), F32) for _ in seqs]
    for i in range(1, nsub):
        k_hat = [srows(k, si) * jnp.exp(jnp.minimum(ref_rows[si][i - 1] - srows(beta, si), 0.0)) for si in seqs]
        k_st = [jnp.concatenate([k_hat[si] * mk[h] for h in range(nh)], axis=0).astype(BF16) for si in seqs]
        gi = [_dot_nt(q_st[si], k_st[si]) for si in seqs]
        sel = (rsub == i) & (ct < GLA_SUB * i)
        attn = [jnp.where(sel, gi[si], attn[si]) for si in seqs]
    o_st = [_dot(attn[si].astype(BF16), v_st[si]) for si in seqs]
    o_seq = []
    for si in seqs:
        acc = o_inter[si]
        for h in range(nh):
            acc = acc + o_st[si][h * t:(h + 1) * t]
        o_seq.append(acc)
    o = jnp.concatenate(o_seq, axis=0)

    expand = exp_ref[...]
    tsub = row & (GLA_SUB - 1)
    for d in range(GLA_SUB):
        if d == 0:
            kd, bd_, vd = k, beta, v
        else:
            kd = pltpu.roll(k, d, axis=0)
            bd_ = pltpu.roll(beta, d, axis=0)
            vd = pltpu.roll(v, d, axis=0)
        term = jnp.where(tsub >= d, q * kd * jnp.exp(jnp.minimum(beta - bd_, 0.0)), 0.0)
        o = o + _dot(term.astype(BF16), expand) * vd

    hv = _iota((vw, kw), 0) >> 6
    hk = _iota((vw, kw), 1) >> 5
    k_end = [(srows(k, si) * jnp.exp(beta_last[si] - srows(beta, si))).astype(BF16) for si in seqs]
    upd = [_dot(srows(v, si).T.astype(BF16), k_end[si]) for si in seqs]
    for si in seqs:
        ht_scr[si] = hts[si] * jnp.exp(beta_last[si]) + jnp.where(hv == hk, upd[si], 0.0)

    ms = _dot_seg(o * o, bdv_ref[...]) * (1.0 / HEAD_DIM)
    on = o * lax.rsqrt(ms + HEAD_NORM_EPS) * ng_ref[...]
    o_ref[...] = (on * (g * _sigmoid(g))).astype(o_ref.dtype).reshape(ns, t, vw)


def _gla(p_g, gk_up, gk_b, norm_g, bdv, expand):
    bsz, s, _ = p_g.shape
    t = GLA_CHUNK
    ns = GLA_G if bsz % GLA_G == 0 else 1
    const = lambda m, n: pl.BlockSpec((m, n), lambda b, i: (0, 0))
    return pl.pallas_call(
        functools.partial(_gla_kernel, ns=ns),
        grid=(bsz // ns, s // t),
        in_specs=[
            pl.BlockSpec((ns, t, GLA_COLS_PAD), lambda b, i: (b, i, 0)),
            const(LANES, GLA_KW), const(1, GLA_KW), const(1, GLA_VW), const(GLA_VW, GLA_VW),
            const(GLA_KW, GLA_VW),
        ],
        out_specs=pl.BlockSpec((ns, t, GLA_VW), lambda b, i: (b, i, 0)),
        out_shape=jax.ShapeDtypeStruct((bsz, s, GLA_VW), BF16),
        scratch_shapes=[pltpu.VMEM((ns, GLA_VW, GLA_KW), F32)],
        compiler_params=_cparams(("arbitrary", "arbitrary")),
        name="gla",
    )(p_g, gk_up, gk_b, norm_g, bdv, expand)


def _outproj_kernel(yr_ref, ys_ref, yg_ref, x_ref, mod_ref, g_ref, w_ref, wr_ref, x1_ref, h_ref, lg_ref):
    mod = mod_ref[...]
    o1 = RWKV_W
    o2 = o1 + SB_W
    tm = x_ref.shape[0]
    nq = OUTPROJ_SPLIT if tm % (OUTPROJ_SPLIT * LANES) == 0 else 1
    rq = tm // nq
    parts = [slice(qi * rq, (qi + 1) * rq) for qi in range(nq)]
    mix = [_dot(yr_ref[sl, :], w_ref[0:o1, :]) + _dot(ys_ref[sl, :], w_ref[o1:o2, :])
           + _dot(yg_ref[sl, :], w_ref[o2:, :]) for sl in parts]
    x1 = [x_ref[sl, :] + mod[2:3] * mix[qi] for qi, sl in enumerate(parts)]
    h = [_rms_mod(u, g_ref[...], mod[4:5], mod[3:4]) for u in x1]
    for qi, sl in enumerate(parts):
        x1_ref[sl, :] = x1[qi]
        h_ref[sl, :] = h[qi].astype(BF16)
        lg_ref[:, sl] = _dot_nt_f32(wr_ref[...], h[qi])


def _outproj(y_r, y_s, y_g, x, mod_l, g, w_out, layer, w_router_t):
    bsz, s, d = x.shape
    tm = min(512, s)
    nt = s // tm
    return pl.pallas_call(
        _outproj_kernel,
        grid=(bsz, nt),
        in_specs=[
            pl.BlockSpec((None, tm, RWKV_W), lambda b, i: (b, i, 0)),
            pl.BlockSpec((None, tm, SB_W), lambda b, i: (b, i, 0)),
            pl.BlockSpec((None, tm, GLA_VW), lambda b, i: (b, i, 0)),
            pl.BlockSpec((None, tm, d), lambda b, i: (b, i, 0)),
            pl.BlockSpec((None, 6, d), lambda b, i: (b, 0, 0)),
            pl.BlockSpec((1, d), lambda b, i: (0, 0)),
            pl.BlockSpec((None, d, d), lambda b, i: (layer, 0, 0)),
            pl.BlockSpec((N_EXPERTS, d), lambda b, i: (0, 0)),
        ],
        out_specs=[
            pl.BlockSpec((None, tm, d), lambda b, i: (b, i, 0)),
            pl.BlockSpec((None, tm, d), lambda b, i: (b, i, 0)),
            pl.BlockSpec((N_EXPERTS, tm), lambda b, i: (0, b * nt + i)),
        ],
        out_shape=[
            jax.ShapeDtypeStruct((bsz, s, d), F32),
            jax.ShapeDtypeStruct((bsz, s, d), BF16),
            jax.ShapeDtypeStruct((N_EXPERTS, bsz * s), F32),
        ],
        compiler_params=_cparams(("arbitrary", "arbitrary")),
        name="outproj",
    )(y_r, y_s, y_g, x, mod_l, g, w_out, w_router_t)


def _route_kernel(lg_ref, bias_ref, tri_ref, w_ref, rank_ref, cnt_ref):
    aff = _sigmoid(lg_ref[...])
    sel = aff + bias_ref[...]
    e = EXPERTS_PER_GROUP
    rows = [sel[i:i + 1, :] for i in range(N_EXPERTS)]
    arow = [aff[i:i + 1, :] for i in range(N_EXPERTS)]
    scores = []
    for gi in range(N_GROUPS):
        a, b, c, d = rows[e * gi:e * gi + e]
        scores.append(jnp.maximum(jnp.maximum(jnp.maximum(a + b, a + c), jnp.maximum(a + d, b + c)),
                                  jnp.maximum(b + d, c + d)))
    grp = jnp.zeros_like(scores[0]).astype(jnp.int32)
    best = scores[0]
    for gi in range(1, N_GROUPS):
        better = scores[gi] > best
        grp = jnp.where(better, gi, grp)
        best = jnp.where(better, scores[gi], best)
    sin, ain = [], []
    for j in range(e):
        sv, av = rows[j], arow[j]
        for gi in range(1, N_GROUPS):
            sv = jnp.where(grp == gi, rows[e * gi + j], sv)
            av = jnp.where(grp == gi, arow[e * gi + j], av)
        sin.append(sv)
        ain.append(av)
    loc1 = jnp.zeros_like(grp)
    b1 = sin[0]
    for j in range(1, e):
        better = sin[j] > b1
        loc1 = jnp.where(better, j, loc1)
        b1 = jnp.where(better, sin[j], b1)
    neg = jnp.full_like(b1, -jnp.inf)
    loc2 = jnp.zeros_like(grp)
    b2 = neg
    for j in range(e):
        cand = jnp.where(loc1 == j, neg, sin[j])
        better = cand > b2
        loc2 = jnp.where(better, j, loc2)
        b2 = jnp.where(better, cand, b2)
    a1 = ain[0]
    a2 = ain[0]
    for j in range(1, e):
        a1 = jnp.where(loc1 == j, ain[j], a1)
        a2 = jnp.where(loc2 == j, ain[j], a2)
    den = a1 + a2
    e1 = grp * e + loc1
    e2 = grp * e + loc2
    eid = _iota(aff.shape, 0)
    is1 = eid == e1
    is2 = eid == e2
    w_ref[...] = jnp.where(is1, a1 / den, jnp.where(is2, a2 / den, 0.0))
    selected = is1 | is2
    self = jnp.where(selected, 1.0, 0.0)
    excl = _dot(self.astype(BF16), tri_ref[...])
    rank_ref[...] = jnp.where(selected, excl, -1.0)
    cnt = jnp.sum(self, axis=1, keepdims=True)
    cnt_ref[...] = jnp.broadcast_to(cnt, cnt_ref.shape)


def _route(logits_t, bias, tri):
    ne, n = logits_t.shape
    tm = tri.shape[0]
    nt = n // tm
    return pl.pallas_call(
        _route_kernel,
        grid=(nt,),
        in_specs=[
            pl.BlockSpec((ne, tm), lambda i: (0, i)),
            pl.BlockSpec((ne, 1), lambda i: (0, 0)),
            pl.BlockSpec((tm, tm), lambda i: (0, 0)),
        ],
        out_specs=[
            pl.BlockSpec((ne, tm), lambda i: (0, i)),
            pl.BlockSpec((ne, tm), lambda i: (0, i)),
            pl.BlockSpec((None, ne, LANES), lambda i: (i, 0, 0)),
        ],
        out_shape=[
            jax.ShapeDtypeStruct((ne, n), F32),
            jax.ShapeDtypeStruct((ne, n), F32),
            jax.ShapeDtypeStruct((nt, ne, LANES), F32),
        ],
        compiler_params=_cparams(("arbitrary",)),
        name="route",
    )(logits_t, bias, tri)


def _moe_kernel(cnt_ref, h_ref, rank_ref, w_ref, wg_ref, wu_ref, wd_ref, x1_ref, mod_ref, fg_ref, o_ref,
                acc_scr, obuf, ybuf, pend, *, final):
    i = pl.program_id(0)
    ep = pl.program_id(1)
    r = MOE_R
    log_r = r.bit_length() - 1

    @pl.when(ep == 0)
    def _():
        acc_scr[...] = jnp.zeros_like(acc_scr)
        pend[0] = 0

    def scatter(onehot, yw):
        acc_scr[...] += lax.dot_general(onehot, yw, (((0,), (0,)), ((), ())), preferred_element_type=F32)

    def select(k, base):
        hit = (rank_ref[k].astype(jnp.int32) - base) == _iota((r, 1), 0)
        wcol = jnp.sum(jnp.where(hit, w_ref[k], 0.0), axis=1, keepdims=True)
        return jnp.where(hit, 1.0, 0.0).astype(BF16), wcol

    def ffn_in(xr, k):
        gate = _dot(xr, wg_ref[k])
        up = _dot(xr, wu_ref[k])
        return (gate * _sigmoid(gate) * up).astype(BF16)

    n0 = (cnt_ref[i * N_EXPERTS + 2 * ep] + (r - 1)) >> log_r
    n1 = (cnt_ref[i * N_EXPERTS + 2 * ep + 1] + (r - 1)) >> log_r
    n_both = jnp.minimum(n0, n1)

    def both(j, c):
        oh0, w0 = select(0, j * r)
        oh1, w1 = select(1, j * r)
        onehot = jnp.concatenate([oh0, oh1], axis=0)
        xr = _dot(onehot, h_ref[...]).astype(BF16)
        a0 = ffn_in(xr[0:r], 0)
        a1 = ffn_in(xr[r:2 * r], 1)
        y0 = _dot(a0, wd_ref[0])
        y1 = _dot(a1, wd_ref[1])
        scatter(onehot, jnp.concatenate([y0 * w0, y1 * w1], axis=0).astype(BF16))
        return c

    lax.fori_loop(0, n_both, both, 0)

    k_rest = jnp.where(n0 > n1, 0, 1)

    def rest(j, c):
        onehot, wcol = select(k_rest, j * r)
        xr = _dot(onehot, h_ref[...]).astype(BF16)
        y = _dot(ffn_in(xr, k_rest), wd_ref[k_rest])
        slot = pend[0]
        off = pl.multiple_of(slot * r, r)
        obuf[pl.ds(off, r), :] = onehot
        ybuf[pl.ds(off, r), :] = (y * wcol).astype(BF16)

        @pl.when(slot == 1)
        def _():
            scatter(obuf[...], ybuf[...])

        pend[0] = 1 - slot
        return c

    lax.fori_loop(n_both, jnp.maximum(n0, n1), rest, 0)

    @pl.when(ep == N_EXPERTS // 2 - 1)
    def _():
        @pl.when(pend[0] == 1)
        def _():
            scatter(obuf[0:r, :], ybuf[0:r, :])

        x2 = x1_ref[...] + mod_ref[5:6, :] * acc_scr[...]
        if final:
            ms = jnp.mean(x2 * x2, axis=-1, keepdims=True)
            x2 = x2 * lax.rsqrt(ms + RMS_EPS) * fg_ref[...]
        o_ref[...] = x2


def _moe(cnt, h, rank_t, w_t, w_gate, w_up, w_down, layer, x1, mod_l, final_g, final, tm, tiles_per_batch):
    n, d = h.shape
    de = w_gate.shape[-1]
    nt = n // tm
    grid_spec = pltpu.PrefetchScalarGridSpec(
        num_scalar_prefetch=1,
        grid=(nt, N_EXPERTS // 2),
        in_specs=[
            pl.BlockSpec((tm, d), lambda i, e, c: (i, 0)),
            pl.BlockSpec((2, 1, tm), lambda i, e, c: (e, 0, i)),
            pl.BlockSpec((2, 1, tm), lambda i, e, c: (e, 0, i)),
            pl.BlockSpec((None, 2, d, de), lambda i, e, c: (layer, e, 0, 0)),
            pl.BlockSpec((None, 2, d, de), lambda i, e, c: (layer, e, 0, 0)),
            pl.BlockSpec((None, 2, de, d), lambda i, e, c: (layer, e, 0, 0)),
            pl.BlockSpec((tm, d), lambda i, e, c: (i, 0)),
            pl.BlockSpec((None, 6, d), lambda i, e, c: (i // tiles_per_batch, 0, 0)),
            pl.BlockSpec((1, d), lambda i, e, c: (0, 0)),
        ],
        out_specs=pl.BlockSpec((tm, d), lambda i, e, c: (i, 0)),
        scratch_shapes=[pltpu.VMEM((tm, d), F32), pltpu.VMEM((2 * MOE_R, tm), BF16),
                        pltpu.VMEM((2 * MOE_R, d), BF16), pltpu.SMEM((1,), jnp.int32)],
    )
    return pl.pallas_call(
        functools.partial(_moe_kernel, final=final),
        grid_spec=grid_spec,
        out_shape=jax.ShapeDtypeStruct((n, d), F32),
        compiler_params=_cparams(("arbitrary", "arbitrary")),
        name="moe",
    )(cnt, h, rank_t.reshape(N_EXPERTS, 1, n), w_t.reshape(N_EXPERTS, 1, n), w_gate, w_up, w_down, x1, mod_l,
      final_g)


def _pad_cols(w, n):
    return jnp.pad(w, [(0, 0)] * (w.ndim - 1) + [(0, n - w.shape[-1])])


def _pad_rows(w, n):
    return jnp.pad(w, [(0, 0)] * (w.ndim - 2) + [(0, n - w.shape[-2]), (0, 0)])


def _layout_in_cols(w):
    rw = RWKV_W
    o = 3 * rw
    parts = [w[..., 0:o], _pad_cols(w[..., o:o + 64], LANES), _pad_cols(w[..., o + 64:o + 128], LANES),
             w[..., o + 128:o + 256]]
    o += 256
    parts.append(w[..., o:o + SB_COLS])
    o += SB_COLS
    parts.append(_pad_cols(w[..., o:], GLA_COLS_PAD))
    return jnp.concatenate(parts, axis=-1)


def kernel(x, c, rms_mix_g, rms_ffn_g, w_mod, b_mod, w_in, w_out, rwkv_mu, rwkv_w0, rwkv_w2, rwkv_a0, rwkv_a2, rwkv_g2, rwkv_k_k, rwkv_k_a, rwkv_r_k, rwkv_lnx_w, rwkv_lnx_b, sb_norm_g, gla_gk_up, gla_gk_b, gla_norm_g, w_router, router_bias, w_gate, w_up, w_down, final_g):
    bsz, s, d = x.shape
    depth = w_in.shape[0]
    n = bsz * s

    mod = _modulation(c, w_mod, b_mod).reshape(depth, bsz, 6, d)
    w_in_l = _layout_in_cols(w_in).astype(BF16)
    w_out_b = w_out.astype(BF16)
    mu_l = _layout_in_cols(jnp.pad(rwkv_mu, ((0, 0), (0, w_in.shape[-1] - rwkv_mu.shape[-1]))))[:, :RWKV_COLS_PAD]
    w2_p = _pad_rows(rwkv_w2, LANES)
    a2_p = _pad_rows(rwkv_a2, LANES)
    up_p = _pad_rows(gla_gk_up, LANES)
    gla_ng = jnp.tile(gla_norm_g, (1, GLA_HEADS))
    bd_rwkv = _block_diag_const(2 * LANES, HEAD_DIM, 1.0)
    bd_sb = _block_diag_const(SB_W, HEAD_DIM, 1.0)
    bd_gla = _block_diag_const(GLA_VW, HEAD_DIM, 1.0)
    expand = (jnp.arange(GLA_KW)[:, None] // GLA_KD == jnp.arange(GLA_VW)[None, :] // HEAD_DIM).astype(BF16)
    tm_moe = min(MOE_TM, s)
    tri = (jnp.arange(tm_moe)[:, None] < jnp.arange(tm_moe)[None, :]).astype(BF16)
    w_router_t = w_router.T
    bias_col = router_bias.reshape(N_EXPERTS, 1)
    wg_b, wu_b, wd_b = w_gate.astype(BF16), w_up.astype(BF16), w_down.astype(BF16)
    row = lambda a: a.reshape(1, -1)

    for l in range(depth):
        p_r, p_s, p_g = _inproj(x, mod[l], row(rms_mix_g[l]), w_in_l, l)
        y_r = _rwkv(p_r, row(mu_l[l]), row(rwkv_w0[l]), w2_p[l], row(rwkv_a0[l]), a2_p[l], rwkv_g2[l],
                    row(rwkv_k_k[l]), row(rwkv_k_a[l]), row(rwkv_r_k[l]), row(rwkv_lnx_w[l]),
                    row(rwkv_lnx_b[l]), bd_rwkv)
        y_s = _sb(p_s, row(sb_norm_g[l]), bd_sb)
        y_g = _gla(p_g, up_p[l], row(gla_gk_b[l]), row(gla_ng[l]), bd_gla, expand)
        x1, h2, logits_t = _outproj(y_r, y_s, y_g, x, mod[l], row(rms_ffn_g[l]), w_out_b, l, w_router_t)
        w_t, rank_t, cnt = _route(logits_t, bias_col, tri)
        cnt_i = cnt[:, :, 0].astype(jnp.int32).reshape(-1)
        x = _moe(cnt_i, h2.reshape(n, d), rank_t, w_t, wg_b, wu_b, wd_b, l, x1.reshape(n, d),
                 mod[l], row(final_g), l == depth - 1, tm_moe, s // tm_moe).reshape(bsz, s, d)
    return x
```

```python
import functools

import jax
import jax.numpy as jnp
from jax import lax
from jax.experimental import pallas as pl
from jax.experimental.pallas import tpu as pltpu

F32 = jnp.float32
BF16 = jnp.bfloat16

LANES = 128
HEAD_DIM = 64
RWKV_W = 512
RWKV_PAIRS = RWKV_W // LANES
RWKV_COLS_PAD = 3 * RWKV_W + 3 * LANES
RWKV_CHUNK = 64
RWKV_G = 4
RWKV_GN_EPS = 64e-5
SB_W = 256
SB_COLS = 3 * SB_W
SB_TQ = 512
SB_TK = 128
SB_DEAD = -104.0
GLA_VW = 256
GLA_KW = 128
GLA_HEADS = 4
GLA_KD = 32
GLA_LORA = 16
GLA_COLS_PAD = 2 * GLA_KW + 2 * GLA_VW + LANES
GLA_CHUNK = 64
GLA_SUB = 16
GLA_G = 4
GLA_NORMALIZER = 16.0
IN_COLS_PAD = RWKV_COLS_PAD + SB_COLS + GLA_COLS_PAD
HEAD_NORM_EPS = 1e-5
RMS_EPS = 1e-6
N_EXPERTS = 16
EXPERTS_PER_GROUP = 4
N_GROUPS = 4
INPROJ_SPLIT = 4
OUTPROJ_SPLIT = 4
MOE_TM = 1024
MOE_R = 128
VMEM_LIMIT = 48 * 1024 * 1024


def _dot(a, b):
    return jnp.dot(a, b, preferred_element_type=F32)


def _dot_nt(a, b):
    return lax.dot_general(a, b, (((1,), (1,)), ((), ())), preferred_element_type=F32)


def _split2(x):
    hi = x.astype(BF16)
    lo = (x - hi.astype(F32)).astype(BF16)
    return hi, lo


def _split3(x):
    hi = x.astype(BF16)
    r = x - hi.astype(F32)
    mid = r.astype(BF16)
    lo = (r - mid.astype(F32)).astype(BF16)
    return hi, mid, lo


def _dot_seg(x, m):
    hi, lo = _split2(x)
    wm = m.shape[0]
    cols = [_dot(hi[:, c:c + wm], m) + _dot(lo[:, c:c + wm], m) for c in range(0, x.shape[1], wm)]
    return cols[0] if len(cols) == 1 else jnp.concatenate(cols, axis=1)


def _dot_exact_lhs(m, x):
    hi, mid, lo = _split3(x)
    return _dot(m, hi) + _dot(m, mid) + _dot(m, lo)


def _dot_f32(a, b):
    ah, al = _split2(a)
    bh, bl = _split2(b)
    return _dot(ah, bh) + _dot(ah, bl) + _dot(al, bh)


def _dot_nt_f32(a, b):
    ah, al = _split2(a)
    bh, bl = _split2(b)
    return _dot_nt(ah, bh) + _dot_nt(ah, bl) + _dot_nt(al, bh)


def _softplus(x):
    return jnp.maximum(x, 0.0) + jnp.log(1.0 + jnp.exp(-jnp.abs(x)))


def _log_sigmoid(x):
    return jnp.minimum(x, 0.0) - jnp.log(1.0 + jnp.exp(-jnp.abs(x)))


def _sigmoid(x):
    return 1.0 / (1.0 + jnp.exp(-x))


def _iota(shape, dim):
    return lax.broadcasted_iota(jnp.int32, shape, dim)


def _block_diag_const(n, blk, val):
    i = jnp.arange(n)
    return jnp.where((i[:, None] // blk) == (i[None, :] // blk), val, 0.0).astype(BF16)


def _cparams(sem):
    return pltpu.CompilerParams(dimension_semantics=sem, vmem_limit_bytes=VMEM_LIMIT)


def _mod_kernel(c_ref, w_ref, b_ref, o_ref):
    c = c_ref[...]
    ca = c * _sigmoid(c)
    o_ref[...] = _dot_f32(ca, w_ref[...]) + b_ref[...]


def _modulation(c, w_mod, b_mod):
    depth, d, six_d = w_mod.shape
    bsz = c.shape[0]
    tn = 1536
    return pl.pallas_call(
        _mod_kernel,
        grid=(depth, six_d // tn),
        in_specs=[
            pl.BlockSpec((bsz, d), lambda l, j: (0, 0)),
            pl.BlockSpec((None, d, tn), lambda l, j: (l, 0, j)),
            pl.BlockSpec((None, 1, tn), lambda l, j: (l, 0, j)),
        ],
        out_specs=pl.BlockSpec((None, bsz, tn), lambda l, j: (l, 0, j)),
        out_shape=jax.ShapeDtypeStruct((depth, bsz, six_d), F32),
        compiler_params=_cparams(("arbitrary", "arbitrary")),
        name="adaln_mod",
    )(c, w_mod, b_mod.reshape(depth, 1, six_d))


def _rms_mod(x, g, scale, shift):
    ms = jnp.mean(x * x, axis=-1, keepdims=True)
    return x * lax.rsqrt(ms + RMS_EPS) * g * (1.0 + scale) + shift


def _inproj_kernel(x_ref, mod_ref, g_ref, w_ref, pr_ref, ps_ref, pg_ref):
    mod = mod_ref[...]
    o1 = RWKV_COLS_PAD
    o2 = o1 + SB_COLS
    tm = x_ref.shape[0]
    nq = INPROJ_SPLIT if tm % (INPROJ_SPLIT * LANES) == 0 else 1
    rq = tm // nq
    parts = [slice(qi * rq, (qi + 1) * rq) for qi in range(nq)]
    h = [_rms_mod(x_ref[sl, :], g_ref[...], mod[1:2], mod[0:1]).astype(BF16) for sl in parts]
    for qi, sl in enumerate(parts):
        pr_ref[sl, :] = _dot(h[qi], w_ref[:, 0:o1])
        ps_ref[sl, :] = _dot(h[qi], w_ref[:, o1:o2]).astype(BF16)
        pg_ref[sl, :] = _dot(h[qi], w_ref[:, o2:IN_COLS_PAD])


def _inproj(x, mod_l, g, w, layer):
    bsz, s, d = x.shape
    tm = min(512, s)
    return pl.pallas_call(
        _inproj_kernel,
        grid=(bsz, s // tm),
        in_specs=[
            pl.BlockSpec((None, tm, d), lambda b, i: (b, i, 0)),
            pl.BlockSpec((None, 6, d), lambda b, i: (b, 0, 0)),
            pl.BlockSpec((1, d), lambda b, i: (0, 0)),
            pl.BlockSpec((None, d, IN_COLS_PAD), lambda b, i: (layer, 0, 0)),
        ],
        out_specs=[
            pl.BlockSpec((None, tm, RWKV_COLS_PAD), lambda b, i: (b, i, 0)),
            pl.BlockSpec((None, tm, SB_COLS), lambda b, i: (b, i, 0)),
            pl.BlockSpec((None, tm, GLA_COLS_PAD), lambda b, i: (b, i, 0)),
        ],
        out_shape=[
            jax.ShapeDtypeStruct((bsz, s, RWKV_COLS_PAD), F32),
            jax.ShapeDtypeStruct((bsz, s, SB_COLS), BF16),
            jax.ShapeDtypeStruct((bsz, s, GLA_COLS_PAD), F32),
        ],
        compiler_params=_cparams(("arbitrary", "arbitrary")),
        name="inproj",
    )(x, mod_l, g, w)


def _rwkv_kernel(p_ref, mu_ref, w0_ref, w2_ref, a0_ref, a2_ref, g2_ref, kk_ref, ka_ref, rk_ref,
                 lnw_ref, lnb_ref, bd_ref, o_ref, ht_scr, prev_scr, *, ng):
    t = RWKV_CHUNK
    rows = ng * t

    @pl.when(pl.program_id(1) == 0)
    def _():
        ht_scr[...] = jnp.zeros_like(ht_scr)
        prev_scr[...] = jnp.zeros_like(prev_scr)

    p = p_ref[...].reshape(rows, RWKV_COLS_PAD)
    row = _iota((rows, 1), 0)
    prev = pltpu.roll(p, 1, axis=0)
    for gi in range(ng):
        prev = jnp.where(row == gi * t, prev_scr[gi, 0:1, :], prev)
        prev_scr[gi, 0:1, :] = p[gi * t + t - 1:gi * t + t, :]
    xm = p + (prev - p) * mu_ref[...]
    w = RWKV_W
    r = xm[:, 0:w]
    k = xm[:, w:2 * w]
    v = xm[:, 2 * w:3 * w]
    xw = xm[:, 3 * w:3 * w + LANES]
    xa = xm[:, 3 * w + LANES:3 * w + 2 * LANES]
    xg = xm[:, 3 * w + 2 * LANES:3 * w + 3 * LANES]
    w_log = -_softplus(-(w0_ref[...] + _dot_f32(jnp.tanh(xw), w2_ref[...]))) - 0.5
    lw = -jnp.exp(w_log)
    iclr = _sigmoid(a0_ref[...] + _dot_f32(xa, a2_ref[...]))
    g = _dot_f32(_sigmoid(xg), g2_ref[...])
    bd = bd_ref[...]
    kkr = k * kk_ref[...]
    ss = _dot_seg(kkr * kkr, bd)
    kk = kkr * lax.rsqrt(jnp.maximum(ss, 1e-24))
    k2 = k * (1.0 + (iclr - 1.0) * ka_ref[...])
    bonus = _dot_seg(r * k2 * rk_ref[...], bd) * v

    ti = _iota((rows, rows), 0)
    tj = _iota((rows, rows), 1)
    tri_incl = jnp.where(((ti >> 6) == (tj >> 6)) & (ti >= tj), 1.0, 0.0).astype(BF16)
    beta = _dot_exact_lhs(tri_incl, lw)
    gam = jnp.exp(beta)
    gam_inv = jnp.exp(-beta)
    a_t = -kk * jnp.exp(beta - lw)
    r_t = r * gam
    b_t = kk * iclr * gam_inv
    k_t = k2 * gam_inv

    lane = _iota((1, LANES), 1)
    m0 = jnp.where(lane < HEAD_DIM, 1.0, 0.0)
    m1 = 1.0 - m0
    n2 = 2 * t
    ii = _iota((n2, n2), 0)
    jj = _iota((n2, n2), 1)
    it = ii & (t - 1)
    jt = jj & (t - 1)
    same64 = (ii >> 6) == (jj >> 6)
    strict = same64 & (it > jt)
    incl = same64 & (it >= jt)
    blk16 = (ii >> 4) == (jj >> 4)
    blk32 = (ii >> 5) == (jj >> 5)
    not16 = jnp.logical_not(blk16)
    not32 = jnp.logical_not(blk32)
    eye = jnp.where(ii == jj, 1.0, 0.0)

    units = [(gi, pr) for gi in range(ng) for pr in range(RWKV_PAIRS)]
    U = range(len(units))

    def cut(arr, gi, pr):
        return arr[gi * t:(gi + 1) * t, pr * LANES:(pr + 1) * LANES]

    def stack(xa_, xb_, gi, pr):
        ca, cb = cut(xa_, gi, pr), cut(xb_, gi, pr)
        return jnp.concatenate([ca * m0, ca * m1, cb * m0, cb * m1], axis=0).astype(BF16)

    lhs = [stack(a_t, r_t, gi, pr) for gi, pr in units]
    rhs = [stack(b_t, k_t, gi, pr) for gi, pr in units]
    vst = [jnp.concatenate([cut(v, gi, pr) * m0, cut(v, gi, pr) * m1], axis=0) for gi, pr in units]
    vstb = [u.astype(BF16) for u in vst]
    gb = [_dot_nt(lhs[i], rhs[i]) for i in U]
    hts = [ht_scr[i] for i in U]
    p0 = [_dot_nt(lhs[i], hts[i].astype(BF16)) for i in U]
    a_ab = [jnp.where(strict, g_[0:n2, 0:n2], 0.0) for g_ in gb]
    a_ak = [jnp.where(strict, g_[0:n2, n2:2 * n2], 0.0).astype(BF16) for g_ in gb]
    a_r = [jnp.concatenate([jnp.where(incl, g_[n2:2 * n2, 0:n2], 0.0),
                            jnp.where(incl, g_[n2:2 * n2, n2:2 * n2], 0.0)], axis=1).astype(BF16) for g_ in gb]
    d1 = [jnp.where(blk16, a, 0.0) for a in a_ab]
    d1b = [d.astype(BF16) for d in d1]
    x = [eye + d for d in d1]
    d2b = [_dot(d, d).astype(BF16) for d in d1b]
    x = [x[i] + _dot(x[i].astype(BF16), d2b[i]) for i in U]
    d4b = [_dot(d, d).astype(BF16) for d in d2b]
    x = [x[i] + _dot(x[i].astype(BF16), d4b[i]) for i in U]
    d8b = [_dot(d, d).astype(BF16) for d in d4b]
    x = [x[i] + _dot(x[i].astype(BF16), d8b[i]) for i in U]
    e32 = [jnp.where(blk32 & not16, a, 0.0).astype(BF16) for a in a_ab]
    xb = [u.astype(BF16) for u in x]
    t1 = [_dot(e32[i], xb[i]).astype(BF16) for i in U]
    x = [x[i] + _dot(xb[i], t1[i]) for i in U]
    e64 = [jnp.where(not32, a, 0.0).astype(BF16) for a in a_ab]
    xb = [u.astype(BF16) for u in x]
    t2 = [_dot(e64[i], xb[i]).astype(BF16) for i in U]
    x = [x[i] + _dot(xb[i], t2[i]) for i in U]
    rhs_u = [p0[i][0:n2] + _dot(a_ak[i], vstb[i]) for i in U]
    ust = [_dot(x[i].astype(BF16), rhs_u[i].astype(BF16)) for i in U]
    uv = [jnp.concatenate([ust[i], vst[i]], axis=0) for i in U]
    yst = [p0[i][n2:2 * n2] + _dot(a_r[i], uv[i].astype(BF16)) for i in U]
    ys = [u[0:t] + u[t:n2] for u in yst]
    upd = [_dot(uv[i].T.astype(BF16), rhs[i]) for i in U]
    for i, (gi, pr) in enumerate(units):
        last = gi * t + t - 1
        ht_scr[i] = (hts[i] + upd[i]) * gam[last:last + 1, pr * LANES:(pr + 1) * LANES]
    y = jnp.concatenate([jnp.concatenate(ys[gi * RWKV_PAIRS:(gi + 1) * RWKV_PAIRS], axis=1)
                         for gi in range(ng)], axis=0)
    inv_n = 1.0 / HEAD_DIM
    mean = _dot_seg(y, bd) * inv_n
    yc = y - mean
    var = _dot_seg(yc * yc, bd) * inv_n
    yn = yc * lax.rsqrt(var + RWKV_GN_EPS) * lnw_ref[...] + lnb_ref[...]
    o_ref[...] = ((yn + bonus) * g).astype(o_ref.dtype).reshape(ng, t, RWKV_W)


def _rwkv(p_r, mu, w0, w2, a0, a2, g2, k_k, k_a, r_k, lnw, lnb, bd):
    bsz, s, _ = p_r.shape
    t = RWKV_CHUNK
    ng = RWKV_G if bsz % RWKV_G == 0 else 1
    vec = lambda n: pl.BlockSpec((1, n), lambda b, i: (0, 0))
    mat = lambda m, n: pl.BlockSpec((m, n), lambda b, i: (0, 0))
    return pl.pallas_call(
        functools.partial(_rwkv_kernel, ng=ng),
        grid=(bsz // ng, s // t),
        in_specs=[
            pl.BlockSpec((ng, t, RWKV_COLS_PAD), lambda b, i: (b, i, 0)),
            vec(RWKV_COLS_PAD), vec(RWKV_W), mat(LANES, RWKV_W), vec(RWKV_W), mat(LANES, RWKV_W),
            mat(LANES, RWKV_W), vec(RWKV_W), vec(RWKV_W), vec(RWKV_W), vec(RWKV_W), vec(RWKV_W),
            mat(2 * LANES, 2 * LANES),
        ],
        out_specs=pl.BlockSpec((ng, t, RWKV_W), lambda b, i: (b, i, 0)),
        out_shape=jax.ShapeDtypeStruct((bsz, s, RWKV_W), BF16),
        scratch_shapes=[pltpu.VMEM((ng * RWKV_PAIRS, LANES, LANES), F32),
                        pltpu.VMEM((ng, 8, RWKV_COLS_PAD), F32)],
        compiler_params=_cparams(("arbitrary", "arbitrary")),
        name="rwkv7",
    )(p_r, mu, w0, w2, a0, a2, g2, k_k, k_a, r_k, lnw, lnb, bd)


def _sb_kernel(q_ref, k_ref, v_ref, g_ref, bd_ref, sfx_ref, o_ref, acc_scr, carry_scr, *, tq):
    tk = SB_TK
    nsub = tq // tk
    nh = SB_W // HEAD_DIM
    heads = range(nh)
    qi = pl.program_id(1)
    lane = _iota((1, SB_W), 1)
    hm = [jnp.where((lane >> 6) == h, 1.0, 0.0).astype(BF16) for h in heads]
    q = q_ref[...] * jnp.asarray(HEAD_DIM ** -0.5, BF16)
    qh = [q * hm[h] for h in heads]
    suffix = sfx_ref[...]
    acc_scr[...] = jnp.zeros_like(acc_scr)
    carry_scr[...] = jnp.zeros_like(carry_scr)

    def block(j, r0, diagonal):
        off = pl.multiple_of(j * tk, tk)
        kb = k_ref[pl.ds(off, tk), :]
        vb = v_ref[pl.ds(off, tk), :]
        vcat = jnp.concatenate([vb * hm[h] for h in heads], axis=0)
        z = [_dot_nt(qh[h][r0:tq], kb) for h in heads]
        l1p = [jnp.log(1.0 + jnp.exp(-jnp.abs(u))) for u in z]
        lk = [-(jnp.maximum(z[h], 0.0) + l1p[h]) for h in heads]
        lsig = [jnp.minimum(z[h], 0.0) - l1p[h] for h in heads]
        if diagonal:
            rows = tq - r0
            causal = _iota((rows, tk), 1) < _iota((rows, tk), 0)
            lk = [jnp.where(causal, u, 0.0) for u in lk]
        cs = [_dot(lk[h].astype(BF16), suffix) for h in heads]
        carry = [carry_scr[h, r0:tq, :] for h in heads]
        wgt = [jnp.exp(lsig[h] + cs[h][:, 0:tk] + carry[h]) for h in heads]
        if diagonal:
            wgt = [jnp.where(causal, u, 0.0) for u in wgt]
        for h in heads:
            carry_scr[h, r0:tq, :] = carry[h] + cs[h][:, tk:2 * tk]
        wcat = jnp.concatenate([u.astype(BF16) for u in wgt], axis=1)
        acc_scr[r0:tq, :] += _dot(wcat, vcat)

    for jd in reversed(range(nsub)):
        block(qi * nsub + jd, jd * tk, True)

    def alive():
        m = carry_scr[0]
        for h in range(1, nh):
            m = jnp.maximum(m, carry_scr[h])
        return (jnp.max(m) > SB_DEAD).astype(jnp.int32)

    def cond(c):
        return (c[0] < qi * nsub) & (c[1] > 0)

    def body(c):
        block(qi * nsub - 1 - c[0], 0, False)
        return c[0] + 1, alive()

    lax.while_loop(cond, body, (jnp.int32(0), alive()))
    o = acc_scr[...]
    ms = _dot_seg(o * o, bd_ref[...]) * (1.0 / HEAD_DIM)
    o_ref[...] = (o * lax.rsqrt(ms + HEAD_NORM_EPS) * g_ref[...]).astype(o_ref.dtype)


def _sb(p_s, norm_g, bd):
    bsz, s, _ = p_s.shape
    tq = min(SB_TQ, s)
    si = jnp.arange(SB_TK)[:, None]
    sj = jnp.arange(2 * SB_TK)[None, :]
    sfx = ((si > sj) | (sj >= SB_TK)).astype(BF16)
    return pl.pallas_call(
        functools.partial(_sb_kernel, tq=tq),
        grid=(bsz, s // tq),
        in_specs=[
            pl.BlockSpec((None, tq, SB_W), lambda b, i: (b, i, 0)),
            pl.BlockSpec((None, s, SB_W), lambda b, i: (b, 0, 1)),
            pl.BlockSpec((None, s, SB_W), lambda b, i: (b, 0, 2)),
            pl.BlockSpec((1, SB_W), lambda b, i: (0, 0)),
            pl.BlockSpec((SB_W, SB_W), lambda b, i: (0, 0)),
            pl.BlockSpec((SB_TK, 2 * SB_TK), lambda b, i: (0, 0)),
        ],
        out_specs=pl.BlockSpec((None, tq, SB_W), lambda b, i: (b, i, 0)),
        out_shape=jax.ShapeDtypeStruct((bsz, s, SB_W), BF16),
        scratch_shapes=[pltpu.VMEM((tq, SB_W), F32), pltpu.VMEM((SB_W // HEAD_DIM, tq, LANES), F32)],
        compiler_params=_cparams(("arbitrary", "arbitrary")),
        name="stickbreak",
    )(p_s, p_s, p_s, norm_g, bd, sfx)


def _gla_kernel(p_ref, up_ref, gkb_ref, ng_ref, bdv_ref, exp_ref, tri_ref, var_ref, smask_ref, o_ref, ht_scr, *, ns):
    t = GLA_CHUNK
    kw, vw = GLA_KW, GLA_VW
    nh = GLA_HEADS
    rows = ns * t
    nsub = t // GLA_SUB
    seqs = range(ns)

    @pl.when(pl.program_id(1) == 0)
    def _():
        ht_scr[...] = jnp.zeros_like(ht_scr)

    p = p_ref[...].reshape(rows, GLA_COLS_PAD)
    q = p[:, 0:kw] * (GLA_KD ** -0.5)
    k = p[:, kw:2 * kw]
    v = p[:, 2 * kw:2 * kw + vw]
    g = p[:, 2 * kw + vw:2 * kw + 2 * vw]
    gk_low = p[:, 2 * kw + 2 * vw:2 * kw + 2 * vw + LANES]
    log_a = _log_sigmoid(_dot_f32(gk_low, up_ref[...]) + gkb_ref[...]) * (1.0 / GLA_NORMALIZER)
    beta = _dot_exact_lhs(tri_ref[...], log_a)
    row = _iota((rows, 1), 0)
    seq = row >> 6
    sub = (row & (t - 1)) >> 4

    def srows(a, si):
        return a[si * t:(si + 1) * t]

    beta_last = [beta[si * t + t - 1:si * t + t, :] for si in seqs]
    hts = [ht_scr[si] for si in seqs]
    q_exp = (q * jnp.exp(beta)).astype(BF16)
    o_inter = [_dot_nt(srows(q_exp, si), hts[si].astype(BF16)) for si in seqs]

    ref_rows = [[beta[si * t + GLA_SUB * i - 1:si * t + GLA_SUB * i, :] for i in range(1, nsub)] for si in seqs]
    beta_ref = jnp.zeros_like(beta)
    for si in seqs:
        for i in range(1, nsub):
            beta_ref = jnp.where((seq == si) & (sub == i), ref_rows[si][i - 1], beta_ref)
    q_hat = q * jnp.exp(jnp.minimum(beta - beta_ref, 0.0))
    lane_k = _iota((1, kw), 1)
    lane_v = _iota((1, vw), 1)
    mk = [jnp.where((lane_k >> 5) == h, 1.0, 0.0) for h in range(nh)]
    mv = [jnp.where((lane_v >> 6) == h, 1.0, 0.0) for h in range(nh)]
    q_st = [jnp.concatenate([srows(q_hat, si) * mk[h] for h in range(nh)], axis=0).astype(BF16) for si in seqs]
    v_st = [jnp.concatenate([srows(v, si) * mv[h] for h in range(nh)], axis=0).astype(BF16) for si in seqs]
    n4 = nh * t
    variant = var_ref[...]
    attn = [jnp.zeros((n4, n4), F32) for _ in seqs]
    for i in range(1, nsub):
        k_hat = [srows(k, si) * jnp.exp(jnp.minimum(ref_rows[si][i - 1] - srows(beta, si), 0.0)) for si in seqs]
        k_st = [jnp.concatenate([k_hat[si] * mk[h] for h in range(nh)], axis=0).astype(BF16) for si in seqs]
        gi = [_dot_nt(q_st[si], k_st[si]) for si in seqs]
        attn = [jnp.where(variant == i, gi[si], attn[si]) for si in seqs]
    o_st = [_dot(attn[si].astype(BF16), v_st[si]) for si in seqs]
    o_seq = []
    for si in seqs:
        acc = o_inter[si]
        for h in range(nh):
            acc = acc + o_st[si][h * t:(h + 1) * t]
        o_seq.append(acc)
    o = jnp.concatenate(o_seq, axis=0)

    expand = exp_ref[...]
    tsub = row & (GLA_SUB - 1)
    for d in range(GLA_SUB):
        if d == 0:
            kd, bd_, vd = k, beta, v
        else:
            kd = pltpu.roll(k, d, axis=0)
            bd_ = pltpu.roll(beta, d, axis=0)
            vd = pltpu.roll(v, d, axis=0)
        term = jnp.where(tsub >= d, q * kd * jnp.exp(jnp.minimum(beta - bd_, 0.0)), 0.0)
        o = o + _dot(term.astype(BF16), expand) * vd

    smask = smask_ref[...]
    k_end = [(srows(k, si) * jnp.exp(beta_last[si] - srows(beta, si))).astype(BF16) for si in seqs]
    upd = [_dot(srows(v, si).T.astype(BF16), k_end[si]) for si in seqs]
    for si in seqs:
        ht_scr[si] = hts[si] * jnp.exp(beta_last[si]) + upd[si] * smask

    ms = _dot_seg(o * o, bdv_ref[...]) * (1.0 / HEAD_DIM)
    on = o * lax.rsqrt(ms + HEAD_NORM_EPS) * ng_ref[...]
    o_ref[...] = (on * (g * _sigmoid(g))).astype(o_ref.dtype).reshape(ns, t, vw)


def _gla(p_g, gk_up, gk_b, norm_g, bdv, expand):
    bsz, s, _ = p_g.shape
    t = GLA_CHUNK
    ns = GLA_G if bsz % GLA_G == 0 else 1
    const = lambda m, n: pl.BlockSpec((m, n), lambda b, i: (0, 0))
    rows = ns * t
    n4 = GLA_HEADS * t
    ri = jnp.arange(rows)
    tri = (((ri[:, None] // t) == (ri[None, :] // t)) & (ri[:, None] >= ri[None, :])).astype(BF16)
    r4 = jnp.arange(n4)
    rsub = (r4 % t) // GLA_SUB
    variant = jnp.where((r4[None, :] % t) < GLA_SUB * rsub[:, None], rsub[:, None], 0).astype(jnp.int32)
    smask = ((jnp.arange(GLA_VW)[:, None] // HEAD_DIM) == (jnp.arange(GLA_KW)[None, :] // GLA_KD)).astype(F32)
    return pl.pallas_call(
        functools.partial(_gla_kernel, ns=ns),
        grid=(bsz // ns, s // t),
        in_specs=[
            pl.BlockSpec((ns, t, GLA_COLS_PAD), lambda b, i: (b, i, 0)),
            const(LANES, GLA_KW), const(1, GLA_KW), const(1, GLA_VW), const(GLA_VW, GLA_VW),
            const(GLA_KW, GLA_VW), const(rows, rows), const(n4, n4), const(GLA_VW, GLA_KW),
        ],
        out_specs=pl.BlockSpec((ns, t, GLA_VW), lambda b, i: (b, i, 0)),
        out_shape=jax.ShapeDtypeStruct((bsz, s, GLA_VW), BF16),
        scratch_shapes=[pltpu.VMEM((ns, GLA_VW, GLA_KW), F32)],
        compiler_params=_cparams(("arbitrary", "arbitrary")),
        name="gla",
    )(p_g, gk_up, gk_b, norm_g, bdv, expand, tri, variant, smask)


def _outproj_kernel(yr_ref, ys_ref, yg_ref, x_ref, mod_ref, g_ref, w_ref, wr_ref, x1_ref, h_ref, lg_ref):
    mod = mod_ref[...]
    o1 = RWKV_W
    o2 = o1 + SB_W
    tm = x_ref.shape[0]
    nq = OUTPROJ_SPLIT if tm % (OUTPROJ_SPLIT * LANES) == 0 else 1
    rq = tm // nq
    parts = [slice(qi * rq, (qi + 1) * rq) for qi in range(nq)]
    mix = [_dot(yr_ref[sl, :], w_ref[0:o1, :]) + _dot(ys_ref[sl, :], w_ref[o1:o2, :])
           + _dot(yg_ref[sl, :], w_ref[o2:, :]) for sl in parts]
    x1 = [x_ref[sl, :] + mod[2:3] * mix[qi] for qi, sl in enumerate(parts)]
    h = [_rms_mod(u, g_ref[...], mod[4:5], mod[3:4]) for u in x1]
    for qi, sl in enumerate(parts):
        x1_ref[sl, :] = x1[qi]
        h_ref[sl, :] = h[qi].astype(BF16)
        lg_ref[:, sl] = _dot_nt_f32(wr_ref[...], h[qi])


def _outproj(y_r, y_s, y_g, x, mod_l, g, w_out, layer, w_router_t):
    bsz, s, d = x.shape
    tm = min(512, s)
    nt = s // tm
    return pl.pallas_call(
        _outproj_kernel,
        grid=(bsz, nt),
        in_specs=[
            pl.BlockSpec((None, tm, RWKV_W), lambda b, i: (b, i, 0)),
            pl.BlockSpec((None, tm, SB_W), lambda b, i: (b, i, 0)),
            pl.BlockSpec((None, tm, GLA_VW), lambda b, i: (b, i, 0)),
            pl.BlockSpec((None, tm, d), lambda b, i: (b, i, 0)),
            pl.BlockSpec((None, 6, d), lambda b, i: (b, 0, 0)),
            pl.BlockSpec((1, d), lambda b, i: (0, 0)),
            pl.BlockSpec((None, d, d), lambda b, i: (layer, 0, 0)),
            pl.BlockSpec((N_EXPERTS, d), lambda b, i: (0, 0)),
        ],
        out_specs=[
            pl.BlockSpec((None, tm, d), lambda b, i: (b, i, 0)),
            pl.BlockSpec((None, tm, d), lambda b, i: (b, i, 0)),
            pl.BlockSpec((N_EXPERTS, tm), lambda b, i: (0, b * nt + i)),
        ],
        out_shape=[
            jax.ShapeDtypeStruct((bsz, s, d), F32),
            jax.ShapeDtypeStruct((bsz, s, d), BF16),
            jax.ShapeDtypeStruct((N_EXPERTS, bsz * s), F32),
        ],
        compiler_params=_cparams(("arbitrary", "arbitrary")),
        name="outproj",
    )(y_r, y_s, y_g, x, mod_l, g, w_out, w_router_t)


def _route_kernel(lg_ref, bias_ref, tri_ref, w_ref, rank_ref, cnt_ref):
    aff = _sigmoid(lg_ref[...])
    sel = aff + bias_ref[...]
    e = EXPERTS_PER_GROUP
    rows = [sel[i:i + 1, :] for i in range(N_EXPERTS)]
    arow = [aff[i:i + 1, :] for i in range(N_EXPERTS)]
    scores = []
    for gi in range(N_GROUPS):
        a, b, c, d = rows[e * gi:e * gi + e]
        scores.append(jnp.maximum(jnp.maximum(jnp.maximum(a + b, a + c), jnp.maximum(a + d, b + c)),
                                  jnp.maximum(b + d, c + d)))
    grp = jnp.zeros_like(scores[0]).astype(jnp.int32)
    best = scores[0]
    for gi in range(1, N_GROUPS):
        better = scores[gi] > best
        grp = jnp.where(better, gi, grp)
        best = jnp.where(better, scores[gi], best)
    sin, ain = [], []
    for j in range(e):
        sv, av = rows[j], arow[j]
        for gi in range(1, N_GROUPS):
            sv = jnp.where(grp == gi, rows[e * gi + j], sv)
            av = jnp.where(grp == gi, arow[e * gi + j], av)
        sin.append(sv)
        ain.append(av)
    loc1 = jnp.zeros_like(grp)
    b1 = sin[0]
    for j in range(1, e):
        better = sin[j] > b1
        loc1 = jnp.where(better, j, loc1)
        b1 = jnp.where(better, sin[j], b1)
    neg = jnp.full_like(b1, -jnp.inf)
    loc2 = jnp.zeros_like(grp)
    b2 = neg
    for j in range(e):
        cand = jnp.where(loc1 == j, neg, sin[j])
        better = cand > b2
        loc2 = jnp.where(better, j, loc2)
        b2 = jnp.where(better, cand, b2)
    a1 = ain[0]
    a2 = ain[0]
    for j in range(1, e):
        a1 = jnp.where(loc1 == j, ain[j], a1)
        a2 = jnp.where(loc2 == j, ain[j], a2)
    den = a1 + a2
    e1 = grp * e + loc1
    e2 = grp * e + loc2
    eid = _iota(aff.shape, 0)
    is1 = eid == e1
    is2 = eid == e2
    w_ref[...] = jnp.where(is1, a1 / den, jnp.where(is2, a2 / den, 0.0))
    selected = is1 | is2
    self = jnp.where(selected, 1.0, 0.0)
    excl = _dot(self.astype(BF16), tri_ref[...])
    rank_ref[...] = jnp.where(selected, excl, -1.0)
    cnt = jnp.sum(self, axis=1, keepdims=True)
    cnt_ref[...] = jnp.broadcast_to(cnt, cnt_ref.shape)


def _route(logits_t, bias, tri):
    ne, n = logits_t.shape
    tm = tri.shape[0]
    nt = n // tm
    return pl.pallas_call(
        _route_kernel,
        grid=(nt,),
        in_specs=[
            pl.BlockSpec((ne, tm), lambda i: (0, i)),
            pl.BlockSpec((ne, 1), lambda i: (0, 0)),
            pl.BlockSpec((tm, tm), lambda i: (0, 0)),
        ],
        out_specs=[
            pl.BlockSpec((ne, tm), lambda i: (0, i)),
            pl.BlockSpec((ne, tm), lambda i: (0, i)),
            pl.BlockSpec((None, ne, LANES), lambda i: (i, 0, 0)),
        ],
        out_shape=[
            jax.ShapeDtypeStruct((ne, n), F32),
            jax.ShapeDtypeStruct((ne, n), F32),
            jax.ShapeDtypeStruct((nt, ne, LANES), F32),
        ],
        compiler_params=_cparams(("arbitrary",)),
        name="route",
    )(logits_t, bias, tri)


def _moe_kernel(cnt_ref, h_ref, rank_ref, w_ref, wg_ref, wu_ref, wd_ref, x1_ref, mod_ref, fg_ref, o_ref,
                acc_scr, obuf, ybuf, pend, *, final):
    i = pl.program_id(0)
    ep = pl.program_id(1)
    r = MOE_R
    log_r = r.bit_length() - 1

    @pl.when(ep == 0)
    def _():
        acc_scr[...] = jnp.zeros_like(acc_scr)
        pend[0] = 0

    def scatter(onehot, yw):
        acc_scr[...] += lax.dot_general(onehot, yw, (((0,), (0,)), ((), ())), preferred_element_type=F32)

    def select(k, base):
        hit = (rank_ref[k].astype(jnp.int32) - base) == _iota((r, 1), 0)
        wcol = jnp.sum(jnp.where(hit, w_ref[k], 0.0), axis=1, keepdims=True)
        return jnp.where(hit, 1.0, 0.0).astype(BF16), wcol

    def ffn_in(xr, k):
        gate = _dot(xr, wg_ref[k])
        up = _dot(xr, wu_ref[k])
        return (gate * _sigmoid(gate) * up).astype(BF16)

    n0 = (cnt_ref[i * N_EXPERTS + 2 * ep] + (r - 1)) >> log_r
    n1 = (cnt_ref[i * N_EXPERTS + 2 * ep + 1] + (r - 1)) >> log_r
    n_both = jnp.minimum(n0, n1)

    def both(j, c):
        oh0, w0 = select(0, j * r)
        oh1, w1 = select(1, j * r)
        onehot = jnp.concatenate([oh0, oh1], axis=0)
        xr = _dot(onehot, h_ref[...]).astype(BF16)
        a0 = ffn_in(xr[0:r], 0)
        a1 = ffn_in(xr[r:2 * r], 1)
        y0 = _dot(a0, wd_ref[0])
        y1 = _dot(a1, wd_ref[1])
        scatter(onehot, jnp.concatenate([y0 * w0, y1 * w1], axis=0).astype(BF16))
        return c

    lax.fori_loop(0, n_both, both, 0)

    k_rest = jnp.where(n0 > n1, 0, 1)

    def rest(j, c):
        onehot, wcol = select(k_rest, j * r)
        xr = _dot(onehot, h_ref[...]).astype(BF16)
        y = _dot(ffn_in(xr, k_rest), wd_ref[k_rest])
        slot = pend[0]
        off = pl.multiple_of(slot * r, r)
        obuf[pl.ds(off, r), :] = onehot
        ybuf[pl.ds(off, r), :] = (y * wcol).astype(BF16)

        @pl.when(slot == 1)
        def _():
            scatter(obuf[...], ybuf[...])

        pend[0] = 1 - slot
        return c

    lax.fori_loop(n_both, jnp.maximum(n0, n1), rest, 0)

    @pl.when(ep == N_EXPERTS // 2 - 1)
    def _():
        @pl.when(pend[0] == 1)
        def _():
            scatter(obuf[0:r, :], ybuf[0:r, :])

        x2 = x1_ref[...] + mod_ref[5:6, :] * acc_scr[...]
        if final:
            ms = jnp.mean(x2 * x2, axis=-1, keepdims=True)
            x2 = x2 * lax.rsqrt(ms + RMS_EPS) * fg_ref[...]
        o_ref[...] = x2


def _moe(cnt, h, rank_t, w_t, w_gate, w_up, w_down, layer, x1, mod_l, final_g, final, tm, tiles_per_batch):
    n, d = h.shape
    de = w_gate.shape[-1]
    nt = n // tm
    grid_spec = pltpu.PrefetchScalarGridSpec(
        num_scalar_prefetch=1,
        grid=(nt, N_EXPERTS // 2),
        in_specs=[
            pl.BlockSpec((tm, d), lambda i, e, c: (i, 0)),
            pl.BlockSpec((2, 1, tm), lambda i, e, c: (e, 0, i)),
            pl.BlockSpec((2, 1, tm), lambda i, e, c: (e, 0, i)),
            pl.BlockSpec((None, 2, d, de), lambda i, e, c: (layer, e, 0, 0)),
            pl.BlockSpec((None, 2, d, de), lambda i, e, c: (layer, e, 0, 0)),
            pl.BlockSpec((None, 2, de, d), lambda i, e, c: (layer, e, 0, 0)),
            pl.BlockSpec((tm, d), lambda i, e, c: (i, 0)),
            pl.BlockSpec((None, 6, d), lambda i, e, c: (i // tiles_per_batch, 0, 0)),
            pl.BlockSpec((1, d), lambda i, e, c: (0, 0)),
        ],
        out_specs=pl.BlockSpec((tm, d), lambda i, e, c: (i, 0)),
        scratch_shapes=[pltpu.VMEM((tm, d), F32), pltpu.VMEM((2 * MOE_R, tm), BF16),
                        pltpu.VMEM((2 * MOE_R, d), BF16), pltpu.SMEM((1,), jnp.int32)],
    )
    return pl.pallas_call(
        functools.partial(_moe_kernel, final=final),
        grid_spec=grid_spec,
        out_shape=jax.ShapeDtypeStruct((n, d), F32),
        compiler_params=_cparams(("arbitrary", "arbitrary")),
        name="moe",
    )(cnt, h, rank_t.reshape(N_EXPERTS, 1, n), w_t.reshape(N_EXPERTS, 1, n), w_gate, w_up, w_down, x1, mod_l,
      final_g)


def _pad_cols(w, n):
    return jnp.pad(w, [(0, 0)] * (w.ndim - 1) + [(0, n - w.shape[-1])])


def _pad_rows(w, n):
    return jnp.pad(w, [(0, 0)] * (w.ndim - 2) + [(0, n - w.shape[-2]), (0, 0)])


def _layout_in_cols(w):
    rw = RWKV_W
    o = 3 * rw
    parts = [w[..., 0:o], _pad_cols(w[..., o:o + 64], LANES), _pad_cols(w[..., o + 64:o + 128], LANES),
             w[..., o + 128:o + 256]]
    o += 256
    parts.append(w[..., o:o + SB_COLS])
    o += SB_COLS
    parts.append(_pad_cols(w[..., o:], GLA_COLS_PAD))
    return jnp.concatenate(parts, axis=-1)


def kernel(x, c, rms_mix_g, rms_ffn_g, w_mod, b_mod, w_in, w_out, rwkv_mu, rwkv_w0, rwkv_w2, rwkv_a0, rwkv_a2, rwkv_g2, rwkv_k_k, rwkv_k_a, rwkv_r_k, rwkv_lnx_w, rwkv_lnx_b, sb_norm_g, gla_gk_up, gla_gk_b, gla_norm_g, w_router, router_bias, w_gate, w_up, w_down, final_g):
    bsz, s, d = x.shape
    depth = w_in.shape[0]
    n = bsz * s

    mod = _modulation(c, w_mod, b_mod).reshape(depth, bsz, 6, d)
    w_in_l = _layout_in_cols(w_in).astype(BF16)
    w_out_b = w_out.astype(BF16)
    mu_l = _layout_in_cols(jnp.pad(rwkv_mu, ((0, 0), (0, w_in.shape[-1] - rwkv_mu.shape[-1]))))[:, :RWKV_COLS_PAD]
    w2_p = _pad_rows(rwkv_w2, LANES)
    a2_p = _pad_rows(rwkv_a2, LANES)
    up_p = _pad_rows(gla_gk_up, LANES)
    gla_ng = jnp.tile(gla_norm_g, (1, GLA_HEADS))
    bd_rwkv = _block_diag_const(2 * LANES, HEAD_DIM, 1.0)
    bd_sb = _block_diag_const(SB_W, HEAD_DIM, 1.0)
    bd_gla = _block_diag_const(GLA_VW, HEAD_DIM, 1.0)
    expand = (jnp.arange(GLA_KW)[:, None] // GLA_KD == jnp.arange(GLA_VW)[None, :] // HEAD_DIM).astype(BF16)
    tm_moe = min(MOE_TM, s)
    tri = (jnp.arange(tm_moe)[:, None] < jnp.arange(tm_moe)[None, :]).astype(BF16)
    w_router_t = w_router.T
    bias_col = router_bias.reshape(N_EXPERTS, 1)
    wg_b, wu_b, wd_b = w_gate.astype(BF16), w_up.astype(BF16), w_down.astype(BF16)
    row = lambda a: a.reshape(1, -1)

    for l in range(depth):
        p_r, p_s, p_g = _inproj(x, mod[l], row(rms_mix_g[l]), w_in_l, l)
        y_r = _rwkv(p_r, row(mu_l[l]), row(rwkv_w0[l]), w2_p[l], row(rwkv_a0[l]), a2_p[l], rwkv_g2[l],
                    row(rwkv_k_k[l]), row(rwkv_k_a[l]), row(rwkv_r_k[l]), row(rwkv_lnx_w[l]),
                    row(rwkv_lnx_b[l]), bd_rwkv)
        y_s = _sb(p_s, row(sb_norm_g[l]), bd_sb)
        y_g = _gla(p_g, up_p[l], row(gla_gk_b[l]), row(gla_ng[l]), bd_gla, expand)
        x1, h2, logits_t = _outproj(y_r, y_s, y_g, x, mod[l], row(rms_ffn_g[l]), w_out_b, l, w_router_t)
        w_t, rank_t, cnt = _route(logits_t, bias_col, tri)
        cnt_i = cnt[:, :, 0].astype(jnp.int32).reshape(-1)
        x = _moe(cnt_i, h2.reshape(n, d), rank_t, w_t, wg_b, wu_b, wd_b, l, x1.reshape(n, d),
                 mod[l], row(final_g), l == depth - 1, tm_moe, s // tm_moe).reshape(bsz, s, d)
    return x
```

```python
import functools

import jax
import jax.numpy as jnp
from jax import lax
from jax.experimental import pallas as pl
from jax.experimental.pallas import tpu as pltpu

F32 = jnp.float32
BF16 = jnp.bfloat16

LANES = 128
HEAD_DIM = 64
RWKV_W = 512
RWKV_PAIRS = RWKV_W // LANES
RWKV_COLS_PAD = 3 * RWKV_W + 3 * LANES
RWKV_CHUNK = 64
RWKV_G = 4
RWKV_GN_EPS = 64e-5
SB_W = 256
SB_COLS = 3 * SB_W
SB_TQ = 512
SB_TK = 128
SB_DEAD = -104.0
GLA_VW = 256
GLA_KW = 128
GLA_HEADS = 4
GLA_KD = 32
GLA_LORA = 16
GLA_COLS_PAD = 2 * GLA_KW + 2 * GLA_VW + LANES
GLA_CHUNK = 64
GLA_SUB = 16
GLA_G = 4
GLA_NORMALIZER = 16.0
IN_COLS_PAD = RWKV_COLS_PAD + SB_COLS + GLA_COLS_PAD
HEAD_NORM_EPS = 1e-5
RMS_EPS = 1e-6
N_EXPERTS = 16
EXPERTS_PER_GROUP = 4
N_GROUPS = 4
OUTPROJ_SPLIT = 4
MOE_TM = 1024
MOE_R = 128
VMEM_LIMIT = 48 * 1024 * 1024


def _dot(a, b):
    return jnp.dot(a, b, preferred_element_type=F32)


def _dot_nt(a, b):
    return lax.dot_general(a, b, (((1,), (1,)), ((), ())), preferred_element_type=F32)


def _split2(x):
    hi = x.astype(BF16)
    lo = (x - hi.astype(F32)).astype(BF16)
    return hi, lo


def _split3(x):
    hi = x.astype(BF16)
    r = x - hi.astype(F32)
    mid = r.astype(BF16)
    lo = (r - mid.astype(F32)).astype(BF16)
    return hi, mid, lo


def _dot_seg(x, m):
    hi, lo = _split2(x)
    wm = m.shape[0]
    cols = [_dot(hi[:, c:c + wm], m) + _dot(lo[:, c:c + wm], m) for c in range(0, x.shape[1], wm)]
    return cols[0] if len(cols) == 1 else jnp.concatenate(cols, axis=1)


def _dot_exact_lhs(m, x):
    hi, mid, lo = _split3(x)
    return _dot(m, hi) + _dot(m, mid) + _dot(m, lo)


def _dot_f32(a, b):
    ah, al = _split2(a)
    bh, bl = _split2(b)
    return _dot(ah, bh) + _dot(ah, bl) + _dot(al, bh)


def _dot_nt_f32(a, b):
    ah, al = _split2(a)
    bh, bl = _split2(b)
    return _dot_nt(ah, bh) + _dot_nt(ah, bl) + _dot_nt(al, bh)


def _softplus(x):
    return jnp.maximum(x, 0.0) + jnp.log(1.0 + jnp.exp(-jnp.abs(x)))


def _log_sigmoid(x):
    return jnp.minimum(x, 0.0) - jnp.log(1.0 + jnp.exp(-jnp.abs(x)))


def _sigmoid(x):
    return 1.0 / (1.0 + jnp.exp(-x))


def _iota(shape, dim):
    return lax.broadcasted_iota(jnp.int32, shape, dim)


def _block_diag_const(n, blk, val):
    i = jnp.arange(n)
    return jnp.where((i[:, None] // blk) == (i[None, :] // blk), val, 0.0).astype(BF16)


def _cparams(sem):
    return pltpu.CompilerParams(dimension_semantics=sem, vmem_limit_bytes=VMEM_LIMIT)


def _mod_kernel(c_ref, w_ref, b_ref, o_ref):
    c = c_ref[...]
    ca = c * _sigmoid(c)
    o_ref[...] = _dot_f32(ca, w_ref[...]) + b_ref[...]


def _modulation(c, w_mod, b_mod):
    depth, d, six_d = w_mod.shape
    bsz = c.shape[0]
    tn = 1536
    return pl.pallas_call(
        _mod_kernel,
        grid=(depth, six_d // tn),
        in_specs=[
            pl.BlockSpec((bsz, d), lambda l, j: (0, 0)),
            pl.BlockSpec((None, d, tn), lambda l, j: (l, 0, j)),
            pl.BlockSpec((None, 1, tn), lambda l, j: (l, 0, j)),
        ],
        out_specs=pl.BlockSpec((None, bsz, tn), lambda l, j: (l, 0, j)),
        out_shape=jax.ShapeDtypeStruct((depth, bsz, six_d), F32),
        compiler_params=_cparams(("arbitrary", "arbitrary")),
        name="adaln_mod",
    )(c, w_mod, b_mod.reshape(depth, 1, six_d))


def _rms_mod(x, g, scale, shift):
    ms = jnp.mean(x * x, axis=-1, keepdims=True)
    return x * lax.rsqrt(ms + RMS_EPS) * g * (1.0 + scale) + shift


def _inproj_kernel(x_ref, mod_ref, g_ref, w_ref, pr_ref, ps_ref, pg_ref):
    mod = mod_ref[...]
    h = _rms_mod(x_ref[...], g_ref[...], mod[1:2], mod[0:1]).astype(BF16)
    o1 = RWKV_COLS_PAD
    o2 = o1 + SB_COLS
    pr_ref[...] = _dot(h, w_ref[:, 0:o1])
    ps_ref[...] = _dot(h, w_ref[:, o1:o2]).astype(BF16)
    pg_ref[...] = _dot(h, w_ref[:, o2:IN_COLS_PAD])


def _inproj(x, mod_l, g, w, layer):
    bsz, s, d = x.shape
    tm = min(512, s)
    return pl.pallas_call(
        _inproj_kernel,
        grid=(bsz, s // tm),
        in_specs=[
            pl.BlockSpec((None, tm, d), lambda b, i: (b, i, 0)),
            pl.BlockSpec((None, 6, d), lambda b, i: (b, 0, 0)),
            pl.BlockSpec((1, d), lambda b, i: (0, 0)),
            pl.BlockSpec((None, d, IN_COLS_PAD), lambda b, i: (layer, 0, 0)),
        ],
        out_specs=[
            pl.BlockSpec((None, tm, RWKV_COLS_PAD), lambda b, i: (b, i, 0)),
            pl.BlockSpec((None, tm, SB_COLS), lambda b, i: (b, i, 0)),
            pl.BlockSpec((None, tm, GLA_COLS_PAD), lambda b, i: (b, i, 0)),
        ],
        out_shape=[
            jax.ShapeDtypeStruct((bsz, s, RWKV_COLS_PAD), F32),
            jax.ShapeDtypeStruct((bsz, s, SB_COLS), BF16),
            jax.ShapeDtypeStruct((bsz, s, GLA_COLS_PAD), F32),
        ],
        compiler_params=_cparams(("arbitrary", "arbitrary")),
        name="inproj",
    )(x, mod_l, g, w)


def _rwkv_kernel(p_ref, mu_ref, w0_ref, w2_ref, a0_ref, a2_ref, g2_ref, kk_ref, ka_ref, rk_ref,
                 lnw_ref, lnb_ref, bd_ref, o_ref, ht_scr, prev_scr, *, ng):
    t = RWKV_CHUNK
    rows = ng * t

    @pl.when(pl.program_id(1) == 0)
    def _():
        ht_scr[...] = jnp.zeros_like(ht_scr)
        prev_scr[...] = jnp.zeros_like(prev_scr)

    p = p_ref[...].reshape(rows, RWKV_COLS_PAD)
    row = _iota((rows, 1), 0)
    prev = pltpu.roll(p, 1, axis=0)
    for gi in range(ng):
        prev = jnp.where(row == gi * t, prev_scr[gi, 0:1, :], prev)
        prev_scr[gi, 0:1, :] = p[gi * t + t - 1:gi * t + t, :]
    xm = p + (prev - p) * mu_ref[...]
    w = RWKV_W
    r = xm[:, 0:w]
    k = xm[:, w:2 * w]
    v = xm[:, 2 * w:3 * w]
    xw = xm[:, 3 * w:3 * w + LANES]
    xa = xm[:, 3 * w + LANES:3 * w + 2 * LANES]
    xg = xm[:, 3 * w + 2 * LANES:3 * w + 3 * LANES]
    w_log = -_softplus(-(w0_ref[...] + _dot_f32(jnp.tanh(xw), w2_ref[...]))) - 0.5
    lw = -jnp.exp(w_log)
    iclr = _sigmoid(a0_ref[...] + _dot_f32(xa, a2_ref[...]))
    g = _dot_f32(_sigmoid(xg), g2_ref[...])
    bd = bd_ref[...]
    kkr = k * kk_ref[...]
    ss = _dot_seg(kkr * kkr, bd)
    kk = kkr * lax.rsqrt(jnp.maximum(ss, 1e-24))
    k2 = k * (1.0 + (iclr - 1.0) * ka_ref[...])
    bonus = _dot_seg(r * k2 * rk_ref[...], bd) * v

    ti = _iota((rows, rows), 0)
    tj = _iota((rows, rows), 1)
    tri_incl = jnp.where(((ti >> 6) == (tj >> 6)) & (ti >= tj), 1.0, 0.0).astype(BF16)
    beta = _dot_exact_lhs(tri_incl, lw)
    gam = jnp.exp(beta)
    gam_inv = jnp.exp(-beta)
    a_t = -kk * jnp.exp(beta - lw)
    r_t = r * gam
    b_t = kk * iclr * gam_inv
    k_t = k2 * gam_inv

    lane = _iota((1, LANES), 1)
    m0 = jnp.where(lane < HEAD_DIM, 1.0, 0.0)
    m1 = 1.0 - m0
    n2 = 2 * t
    ii = _iota((n2, n2), 0)
    jj = _iota((n2, n2), 1)
    it = ii & (t - 1)
    jt = jj & (t - 1)
    same64 = (ii >> 6) == (jj >> 6)
    strict = same64 & (it > jt)
    incl = same64 & (it >= jt)
    blk16 = (ii >> 4) == (jj >> 4)
    blk32 = (ii >> 5) == (jj >> 5)
    not16 = jnp.logical_not(blk16)
    not32 = jnp.logical_not(blk32)
    eye = jnp.where(ii == jj, 1.0, 0.0)

    units = [(gi, pr) for gi in range(ng) for pr in range(RWKV_PAIRS)]
    U = range(len(units))

    def cut(arr, gi, pr):
        return arr[gi * t:(gi + 1) * t, pr * LANES:(pr + 1) * LANES]

    def stack(xa_, xb_, gi, pr):
        ca, cb = cut(xa_, gi, pr), cut(xb_, gi, pr)
        return jnp.concatenate([ca * m0, ca * m1, cb * m0, cb * m1], axis=0).astype(BF16)

    lhs = [stack(a_t, r_t, gi, pr) for gi, pr in units]
    rhs = [stack(b_t, k_t, gi, pr) for gi, pr in units]
    vst = [jnp.concatenate([cut(v, gi, pr) * m0, cut(v, gi, pr) * m1], axis=0) for gi, pr in units]
    vstb = [u.astype(BF16) for u in vst]
    gb = [_dot_nt(lhs[i], rhs[i]) for i in U]
    hts = [ht_scr[i] for i in U]
    p0 = [_dot_nt(lhs[i], hts[i].astype(BF16)) for i in U]
    a_ab = [jnp.where(strict, g_[0:n2, 0:n2], 0.0) for g_ in gb]
    a_ak = [jnp.where(strict, g_[0:n2, n2:2 * n2], 0.0).astype(BF16) for g_ in gb]
    a_r = [jnp.concatenate([jnp.where(incl, g_[n2:2 * n2, 0:n2], 0.0),
                            jnp.where(incl, g_[n2:2 * n2, n2:2 * n2], 0.0)], axis=1).astype(BF16) for g_ in gb]
    d1 = [jnp.where(blk16, a, 0.0) for a in a_ab]
    d1b = [d.astype(BF16) for d in d1]
    x = [eye + d for d in d1]
    d2b = [_dot(d, d).astype(BF16) for d in d1b]
    x = [x[i] + _dot(x[i].astype(BF16), d2b[i]) for i in U]
    d4b = [_dot(d, d).astype(BF16) for d in d2b]
    x = [x[i] + _dot(x[i].astype(BF16), d4b[i]) for i in U]
    d8b = [_dot(d, d).astype(BF16) for d in d4b]
    x = [x[i] + _dot(x[i].astype(BF16), d8b[i]) for i in U]
    e32 = [jnp.where(blk32 & not16, a, 0.0).astype(BF16) for a in a_ab]
    xb = [u.astype(BF16) for u in x]
    t1 = [_dot(e32[i], xb[i]).astype(BF16) for i in U]
    x = [x[i] + _dot(xb[i], t1[i]) for i in U]
    e64 = [jnp.where(not32, a, 0.0).astype(BF16) for a in a_ab]
    xb = [u.astype(BF16) for u in x]
    t2 = [_dot(e64[i], xb[i]).astype(BF16) for i in U]
    x = [x[i] + _dot(xb[i], t2[i]) for i in U]
    rhs_u = [p0[i][0:n2] + _dot(a_ak[i], vstb[i]) for i in U]
    ust = [_dot(x[i].astype(BF16), rhs_u[i].astype(BF16)) for i in U]
    uv = [jnp.concatenate([ust[i], vst[i]], axis=0) for i in U]
    yst = [p0[i][n2:2 * n2] + _dot(a_r[i], uv[i].astype(BF16)) for i in U]
    ys = [u[0:t] + u[t:n2] for u in yst]
    upd = [_dot(uv[i].T.astype(BF16), rhs[i]) for i in U]
    for i, (gi, pr) in enumerate(units):
        last = gi * t + t - 1
        ht_scr[i] = (hts[i] + upd[i]) * gam[last:last + 1, pr * LANES:(pr + 1) * LANES]
    y = jnp.concatenate([jnp.concatenate(ys[gi * RWKV_PAIRS:(gi + 1) * RWKV_PAIRS], axis=1)
                         for gi in range(ng)], axis=0)
    inv_n = 1.0 / HEAD_DIM
    mean = _dot_seg(y, bd) * inv_n
    yc = y - mean
    var = _dot_seg(yc * yc, bd) * inv_n
    yn = yc * lax.rsqrt(var + RWKV_GN_EPS) * lnw_ref[...] + lnb_ref[...]
    o_ref[...] = ((yn + bonus) * g).astype(o_ref.dtype).reshape(ng, t, RWKV_W)


def _rwkv(p_r, mu, w0, w2, a0, a2, g2, k_k, k_a, r_k, lnw, lnb, bd):
    bsz, s, _ = p_r.shape
    t = RWKV_CHUNK
    ng = RWKV_G if bsz % RWKV_G == 0 else 1
    vec = lambda n: pl.BlockSpec((1, n), lambda b, i: (0, 0))
    mat = lambda m, n: pl.BlockSpec((m, n), lambda b, i: (0, 0))
    return pl.pallas_call(
        functools.partial(_rwkv_kernel, ng=ng),
        grid=(bsz // ng, s // t),
        in_specs=[
            pl.BlockSpec((ng, t, RWKV_COLS_PAD), lambda b, i: (b, i, 0)),
            vec(RWKV_COLS_PAD), vec(RWKV_W), mat(LANES, RWKV_W), vec(RWKV_W), mat(LANES, RWKV_W),
            mat(LANES, RWKV_W), vec(RWKV_W), vec(RWKV_W), vec(RWKV_W), vec(RWKV_W), vec(RWKV_W),
            mat(2 * LANES, 2 * LANES),
        ],
        out_specs=pl.BlockSpec((ng, t, RWKV_W), lambda b, i: (b, i, 0)),
        out_shape=jax.ShapeDtypeStruct((bsz, s, RWKV_W), BF16),
        scratch_shapes=[pltpu.VMEM((ng * RWKV_PAIRS, LANES, LANES), F32),
                        pltpu.VMEM((ng, 8, RWKV_COLS_PAD), F32)],
        compiler_params=_cparams(("arbitrary", "arbitrary")),
        name="rwkv7",
    )(p_r, mu, w0, w2, a0, a2, g2, k_k, k_a, r_k, lnw, lnb, bd)


def _sb_kernel(q_ref, k_ref, v_ref, g_ref, bd_ref, sfx_ref, o_ref, acc_scr, carry_scr, *, tq):
    tk = SB_TK
    nsub = tq // tk
    nh = SB_W // HEAD_DIM
    heads = range(nh)
    qi = pl.program_id(1)
    lane = _iota((1, SB_W), 1)
    hm = [jnp.where((lane >> 6) == h, 1.0, 0.0).astype(BF16) for h in heads]
    q = q_ref[...] * jnp.asarray(HEAD_DIM ** -0.5, BF16)
    qh = [q * hm[h] for h in heads]
    suffix = sfx_ref[...]
    acc_scr[...] = jnp.zeros_like(acc_scr)
    carry_scr[...] = jnp.zeros_like(carry_scr)

    def block(j, r0, diagonal):
        off = pl.multiple_of(j * tk, tk)
        kb = k_ref[pl.ds(off, tk), :]
        vb = v_ref[pl.ds(off, tk), :]
        vcat = jnp.concatenate([vb * hm[h] for h in heads], axis=0)
        z = [_dot_nt(qh[h][r0:tq], kb) for h in heads]
        l1p = [jnp.log(1.0 + jnp.exp(-jnp.abs(u))) for u in z]
        lk = [-(jnp.maximum(z[h], 0.0) + l1p[h]) for h in heads]
        lsig = [jnp.minimum(z[h], 0.0) - l1p[h] for h in heads]
        if diagonal:
            rows = tq - r0
            causal = _iota((rows, tk), 1) < _iota((rows, tk), 0)
            lk = [jnp.where(causal, u, 0.0) for u in lk]
        cs = [_dot(lk[h].astype(BF16), suffix) for h in heads]
        carry = [carry_scr[h, r0:tq, :] for h in heads]
        wgt = [jnp.exp(lsig[h] + cs[h][:, 0:tk] + carry[h]) for h in heads]
        if diagonal:
            wgt = [jnp.where(causal, u, 0.0) for u in wgt]
        for h in heads:
            carry_scr[h, r0:tq, :] = carry[h] + cs[h][:, tk:2 * tk]
        wcat = jnp.concatenate([u.astype(BF16) for u in wgt], axis=1)
        acc_scr[r0:tq, :] += _dot(wcat, vcat)

    for jd in reversed(range(nsub)):
        block(qi * nsub + jd, jd * tk, True)

    def alive():
        m = carry_scr[0]
        for h in range(1, nh):
            m = jnp.maximum(m, carry_scr[h])
        return (jnp.max(m) > SB_DEAD).astype(jnp.int32)

    def cond(c):
        return (c[0] < qi * nsub) & (c[1] > 0)

    def body(c):
        block(qi * nsub - 1 - c[0], 0, False)
        return c[0] + 1, alive()

    lax.while_loop(cond, body, (jnp.int32(0), alive()))
    o = acc_scr[...]
    ms = _dot_seg(o * o, bd_ref[...]) * (1.0 / HEAD_DIM)
    o_ref[...] = (o * lax.rsqrt(ms + HEAD_NORM_EPS) * g_ref[...]).astype(o_ref.dtype)


def _sb(p_s, norm_g, bd):
    bsz, s, _ = p_s.shape
    tq = min(SB_TQ, s)
    si = jnp.arange(SB_TK)[:, None]
    sj = jnp.arange(2 * SB_TK)[None, :]
    sfx = ((si > sj) | (sj >= SB_TK)).astype(BF16)
    return pl.pallas_call(
        functools.partial(_sb_kernel, tq=tq),
        grid=(bsz, s // tq),
        in_specs=[
            pl.BlockSpec((None, tq, SB_W), lambda b, i: (b, i, 0)),
            pl.BlockSpec((None, s, SB_W), lambda b, i: (b, 0, 1)),
            pl.BlockSpec((None, s, SB_W), lambda b, i: (b, 0, 2)),
            pl.BlockSpec((1, SB_W), lambda b, i: (0, 0)),
            pl.BlockSpec((SB_W, SB_W), lambda b, i: (0, 0)),
            pl.BlockSpec((SB_TK, 2 * SB_TK), lambda b, i: (0, 0)),
        ],
        out_specs=pl.BlockSpec((None, tq, SB_W), lambda b, i: (b, i, 0)),
        out_shape=jax.ShapeDtypeStruct((bsz, s, SB_W), BF16),
        scratch_shapes=[pltpu.VMEM((tq, SB_W), F32), pltpu.VMEM((SB_W // HEAD_DIM, tq, LANES), F32)],
        compiler_params=_cparams(("arbitrary", "arbitrary")),
        name="stickbreak",
    )(p_s, p_s, p_s, norm_g, bd, sfx)


def _gla_kernel(p_ref, up_ref, gkb_ref, ng_ref, bdv_ref, exp_ref, tri_ref, var_ref, smask_ref, o_ref, ht_scr, *, ns):
    t = GLA_CHUNK
    kw, vw = GLA_KW, GLA_VW
    nh = GLA_HEADS
    rows = ns * t
    nsub = t // GLA_SUB
    seqs = range(ns)

    @pl.when(pl.program_id(1) == 0)
    def _():
        ht_scr[...] = jnp.zeros_like(ht_scr)

    p = p_ref[...].reshape(rows, GLA_COLS_PAD)
    q = p[:, 0:kw] * (GLA_KD ** -0.5)
    k = p[:, kw:2 * kw]
    v = p[:, 2 * kw:2 * kw + vw]
    g = p[:, 2 * kw + vw:2 * kw + 2 * vw]
    gk_low = p[:, 2 * kw + 2 * vw:2 * kw + 2 * vw + LANES]
    log_a = _log_sigmoid(_dot_f32(gk_low, up_ref[...]) + gkb_ref[...]) * (1.0 / GLA_NORMALIZER)
    beta = _dot_exact_lhs(tri_ref[...], log_a)
    row = _iota((rows, 1), 0)
    seq = row >> 6
    sub = (row & (t - 1)) >> 4

    def srows(a, si):
        return a[si * t:(si + 1) * t]

    beta_last = [beta[si * t + t - 1:si * t + t, :] for si in seqs]
    hts = [ht_scr[si] for si in seqs]
    q_exp = (q * jnp.exp(beta)).astype(BF16)
    o_inter = [_dot_nt(srows(q_exp, si), hts[si].astype(BF16)) for si in seqs]

    ref_rows = [[beta[si * t + GLA_SUB * i - 1:si * t + GLA_SUB * i, :] for i in range(1, nsub)] for si in seqs]
    beta_ref = jnp.zeros_like(beta)
    for si in seqs:
        for i in range(1, nsub):
            beta_ref = jnp.where((seq == si) & (sub == i), ref_rows[si][i - 1], beta_ref)
    q_hat = q * jnp.exp(jnp.minimum(beta - beta_ref, 0.0))
    lane_k = _iota((1, kw), 1)
    lane_v = _iota((1, vw), 1)
    mk = [jnp.where((lane_k >> 5) == h, 1.0, 0.0) for h in range(nh)]
    mv = [jnp.where((lane_v >> 6) == h, 1.0, 0.0) for h in range(nh)]
    q_st = [jnp.concatenate([srows(q_hat, si) * mk[h] for h in range(nh)], axis=0).astype(BF16) for si in seqs]
    v_st = [jnp.concatenate([srows(v, si) * mv[h] for h in range(nh)], axis=0).astype(BF16) for si in seqs]
    n4 = nh * t
    variant = var_ref[...]
    attn = [jnp.zeros((n4, n4), F32) for _ in seqs]
    for i in range(1, nsub):
        k_hat = [srows(k, si) * jnp.exp(jnp.minimum(ref_rows[si][i - 1] - srows(beta, si), 0.0)) for si in seqs]
        k_st = [jnp.concatenate([k_hat[si] * mk[h] for h in range(nh)], axis=0).astype(BF16) for si in seqs]
        gi = [_dot_nt(q_st[si], k_st[si]) for si in seqs]
        attn = [jnp.where(variant == i, gi[si], attn[si]) for si in seqs]
    o_st = [_dot(attn[si].astype(BF16), v_st[si]) for si in seqs]
    o_seq = []
    for si in seqs:
        acc = o_inter[si]
        for h in range(nh):
            acc = acc + o_st[si][h * t:(h + 1) * t]
        o_seq.append(acc)
    o = jnp.concatenate(o_seq, axis=0)

    expand = exp_ref[...]
    tsub = row & (GLA_SUB - 1)
    for d in range(GLA_SUB):
        if d == 0:
            kd, bd_, vd = k, beta, v
        else:
            kd = pltpu.roll(k, d, axis=0)
            bd_ = pltpu.roll(beta, d, axis=0)
            vd = pltpu.roll(v, d, axis=0)
        term = jnp.where(tsub >= d, q * kd * jnp.exp(jnp.minimum(beta - bd_, 0.0)), 0.0)
        o = o + _dot(term.astype(BF16), expand) * vd

    smask = smask_ref[...]
    k_end = [(srows(k, si) * jnp.exp(beta_last[si] - srows(beta, si))).astype(BF16) for si in seqs]
    upd = [_dot(srows(v, si).T.astype(BF16), k_end[si]) for si in seqs]
    for si in seqs:
        ht_scr[si] = hts[si] * jnp.exp(beta_last[si]) + upd[si] * smask

    ms = _dot_seg(o * o, bdv_ref[...]) * (1.0 / HEAD_DIM)
    on = o * lax.rsqrt(ms + HEAD_NORM_EPS) * ng_ref[...]
    o_ref[...] = (on * (g * _sigmoid(g))).astype(o_ref.dtype).reshape(ns, t, vw)


def _gla(p_g, gk_up, gk_b, norm_g, bdv, expand):
    bsz, s, _ = p_g.shape
    t = GLA_CHUNK
    ns = GLA_G if bsz % GLA_G == 0 else 1
    const = lambda m, n: pl.BlockSpec((m, n), lambda b, i: (0, 0))
    rows = ns * t
    n4 = GLA_HEADS * t
    ri = jnp.arange(rows)
    tri = (((ri[:, None] // t) == (ri[None, :] // t)) & (ri[:, None] >= ri[None, :])).astype(BF16)
    r4 = jnp.arange(n4)
    rsub = (r4 % t) // GLA_SUB
    variant = jnp.where((r4[None, :] % t) < GLA_SUB * rsub[:, None], rsub[:, None], 0).astype(jnp.int32)
    smask = ((jnp.arange(GLA_VW)[:, None] // HEAD_DIM) == (jnp.arange(GLA_KW)[None, :] // GLA_KD)).astype(F32)
    return pl.pallas_call(
        functools.partial(_gla_kernel, ns=ns),
        grid=(bsz // ns, s // t),
        in_specs=[
            pl.BlockSpec((ns, t, GLA_COLS_PAD), lambda b, i: (b, i, 0)),
            const(LANES, GLA_KW), const(1, GLA_KW), const(1, GLA_VW), const(GLA_VW, GLA_VW),
            const(GLA_KW, GLA_VW), const(rows, rows), const(n4, n4), const(GLA_VW, GLA_KW),
        ],
        out_specs=pl.BlockSpec((ns, t, GLA_VW), lambda b, i: (b, i, 0)),
        out_shape=jax.ShapeDtypeStruct((bsz, s, GLA_VW), BF16),
        scratch_shapes=[pltpu.VMEM((ns, GLA_VW, GLA_KW), F32)],
        compiler_params=_cparams(("arbitrary", "arbitrary")),
        name="gla",
    )(p_g, gk_up, gk_b, norm_g, bdv, expand, tri, variant, smask)


def _outproj_kernel(yr_ref, ys_ref, yg_ref, x_ref, mod_ref, g_ref, w_ref, wr_ref, x1_ref, h_ref, lg_ref):
    mod = mod_ref[...]
    o1 = RWKV_W
    o2 = o1 + SB_W
    tm = x_ref.shape[0]
    nq = OUTPROJ_SPLIT if tm % (OUTPROJ_SPLIT * LANES) == 0 else 1
    rq = tm // nq
    parts = [slice(qi * rq, (qi + 1) * rq) for qi in range(nq)]
    mix = [_dot(yr_ref[sl, :], w_ref[0:o1, :]) + _dot(ys_ref[sl, :], w_ref[o1:o2, :])
           + _dot(yg_ref[sl, :], w_ref[o2:, :]) for sl in parts]
    x1 = [x_ref[sl, :] + mod[2:3] * mix[qi] for qi, sl in enumerate(parts)]
    h = [_rms_mod(u, g_ref[...], mod[4:5], mod[3:4]) for u in x1]
    for qi, sl in enumerate(parts):
        x1_ref[sl, :] = x1[qi]
        h_ref[sl, :] = h[qi].astype(BF16)
        lg_ref[:, sl] = _dot_nt_f32(wr_ref[...], h[qi])


def _outproj(y_r, y_s, y_g, x, mod_l, g, w_out, layer, w_router_t):
    bsz, s, d = x.shape
    tm = min(512, s)
    nt = s // tm
    return pl.pallas_call(
        _outproj_kernel,
        grid=(bsz, nt),
        in_specs=[
            pl.BlockSpec((None, tm, RWKV_W), lambda b, i: (b, i, 0)),
            pl.BlockSpec((None, tm, SB_W), lambda b, i: (b, i, 0)),
            pl.BlockSpec((None, tm, GLA_VW), lambda b, i: (b, i, 0)),
            pl.BlockSpec((None, tm, d), lambda b, i: (b, i, 0)),
            pl.BlockSpec((None, 6, d), lambda b, i: (b, 0, 0)),
            pl.BlockSpec((1, d), lambda b, i: (0, 0)),
            pl.BlockSpec((None, d, d), lambda b, i: (layer, 0, 0)),
            pl.BlockSpec((N_EXPERTS, d), lambda b, i: (0, 0)),
        ],
        out_specs=[
            pl.BlockSpec((None, tm, d), lambda b, i: (b, i, 0)),
            pl.BlockSpec((None, tm, d), lambda b, i: (b, i, 0)),
            pl.BlockSpec((N_EXPERTS, tm), lambda b, i: (0, b * nt + i)),
        ],
        out_shape=[
            jax.ShapeDtypeStruct((bsz, s, d), F32),
            jax.ShapeDtypeStruct((bsz, s, d), BF16),
            jax.ShapeDtypeStruct((N_EXPERTS, bsz * s), F32),
        ],
        compiler_params=_cparams(("arbitrary", "arbitrary")),
        name="outproj",
    )(y_r, y_s, y_g, x, mod_l, g, w_out, w_router_t)


def _route_kernel(lg_ref, bias_ref, tri_ref, w_ref, rank_ref, cnt_ref):
    aff = _sigmoid(lg_ref[...])
    sel = aff + bias_ref[...]
    e = EXPERTS_PER_GROUP
    rows = [sel[i:i + 1, :] for i in range(N_EXPERTS)]
    arow = [aff[i:i + 1, :] for i in range(N_EXPERTS)]
    scores = []
    for gi in range(N_GROUPS):
        a, b, c, d = rows[e * gi:e * gi + e]
        scores.append(jnp.maximum(jnp.maximum(jnp.maximum(a + b, a + c), jnp.maximum(a + d, b + c)),
                                  jnp.maximum(b + d, c + d)))
    grp = jnp.zeros_like(scores[0]).astype(jnp.int32)
    best = scores[0]
    for gi in range(1, N_GROUPS):
        better = scores[gi] > best
        grp = jnp.where(better, gi, grp)
        best = jnp.where(better, scores[gi], best)
    sin, ain = [], []
    for j in range(e):
        sv, av = rows[j], arow[j]
        for gi in range(1, N_GROUPS):
            sv = jnp.where(grp == gi, rows[e * gi + j], sv)
            av = jnp.where(grp == gi, arow[e * gi + j], av)
        sin.append(sv)
        ain.append(av)
    loc1 = jnp.zeros_like(grp)
    b1 = sin[0]
    for j in range(1, e):
        better = sin[j] > b1
        loc1 = jnp.where(better, j, loc1)
        b1 = jnp.where(better, sin[j], b1)
    neg = jnp.full_like(b1, -jnp.inf)
    loc2 = jnp.zeros_like(grp)
    b2 = neg
    for j in range(e):
        cand = jnp.where(loc1 == j, neg, sin[j])
        better = cand > b2
        loc2 = jnp.where(better, j, loc2)
        b2 = jnp.where(better, cand, b2)
    a1 = ain[0]
    a2 = ain[0]
    for j in range(1, e):
        a1 = jnp.where(loc1 == j, ain[j], a1)
        a2 = jnp.where(loc2 == j, ain[j], a2)
    den = a1 + a2
    e1 = grp * e + loc1
    e2 = grp * e + loc2
    eid = _iota(aff.shape, 0)
    is1 = eid == e1
    is2 = eid == e2
    w_ref[...] = jnp.where(is1, a1 / den, jnp.where(is2, a2 / den, 0.0))
    selected = is1 | is2
    self = jnp.where(selected, 1.0, 0.0)
    excl = _dot(self.astype(BF16), tri_ref[...])
    rank_ref[...] = jnp.where(selected, excl, -1.0)
    cnt = jnp.sum(self, axis=1, keepdims=True)
    cnt_ref[...] = jnp.broadcast_to(cnt, cnt_ref.shape)


def _route(logits_t, bias, tri):
    ne, n = logits_t.shape
    tm = tri.shape[0]
    nt = n // tm
    return pl.pallas_call(
        _route_kernel,
        grid=(nt,),
        in_specs=[
            pl.BlockSpec((ne, tm), lambda i: (0, i)),
            pl.BlockSpec((ne, 1), lambda i: (0, 0)),
            pl.BlockSpec((tm, tm), lambda i: (0, 0)),
        ],
        out_specs=[
            pl.BlockSpec((ne, tm), lambda i: (0, i)),
            pl.BlockSpec((ne, tm), lambda i: (0, i)),
            pl.BlockSpec((None, ne, LANES), lambda i: (i, 0, 0)),
        ],
        out_shape=[
            jax.ShapeDtypeStruct((ne, n), F32),
            jax.ShapeDtypeStruct((ne, n), F32),
            jax.ShapeDtypeStruct((nt, ne, LANES), F32),
        ],
        compiler_params=_cparams(("arbitrary",)),
        name="route",
    )(logits_t, bias, tri)


def _moe_kernel(cnt_ref, h_ref, rank_ref, w_ref, wg_ref, wu_ref, wd_ref, x1_ref, mod_ref, fg_ref, o_ref,
                acc_scr, obuf, ybuf, pend, *, final):
    i = pl.program_id(0)
    ep = pl.program_id(1)
    r = MOE_R
    log_r = r.bit_length() - 1

    @pl.when(ep == 0)
    def _():
        acc_scr[...] = jnp.zeros_like(acc_scr)
        pend[0] = 0

    def scatter(onehot, yw):
        acc_scr[...] += lax.dot_general(onehot, yw, (((0,), (0,)), ((), ())), preferred_element_type=F32)

    def select(k, base):
        hit = (rank_ref[k].astype(jnp.int32) - base) == _iota((r, 1), 0)
        wcol = jnp.sum(jnp.where(hit, w_ref[k], 0.0), axis=1, keepdims=True)
        return jnp.where(hit, 1.0, 0.0).astype(BF16), wcol

    def ffn_in(xr, k):
        gate = _dot(xr, wg_ref[k])
        up = _dot(xr, wu_ref[k])
        return (gate * _sigmoid(gate) * up).astype(BF16)

    n0 = (cnt_ref[i * N_EXPERTS + 2 * ep] + (r - 1)) >> log_r
    n1 = (cnt_ref[i * N_EXPERTS + 2 * ep + 1] + (r - 1)) >> log_r
    n_both = jnp.minimum(n0, n1)

    def both(j, c):
        oh0, w0 = select(0, j * r)
        oh1, w1 = select(1, j * r)
        onehot = jnp.concatenate([oh0, oh1], axis=0)
        xr = _dot(onehot, h_ref[...]).astype(BF16)
        a0 = ffn_in(xr[0:r], 0)
        a1 = ffn_in(xr[r:2 * r], 1)
        y0 = _dot(a0, wd_ref[0])
        y1 = _dot(a1, wd_ref[1])
        scatter(onehot, jnp.concatenate([y0 * w0, y1 * w1], axis=0).astype(BF16))
        return c

    lax.fori_loop(0, n_both, both, 0)

    k_rest = jnp.where(n0 > n1, 0, 1)

    def rest(j, c):
        onehot, wcol = select(k_rest, j * r)
        xr = _dot(onehot, h_ref[...]).astype(BF16)
        y = _dot(ffn_in(xr, k_rest), wd_ref[k_rest])
        slot = pend[0]
        off = pl.multiple_of(slot * r, r)
        obuf[pl.ds(off, r), :] = onehot
        ybuf[pl.ds(off, r), :] = (y * wcol).astype(BF16)

        @pl.when(slot == 1)
        def _():
            scatter(obuf[...], ybuf[...])

        pend[0] = 1 - slot
        return c

    lax.fori_loop(n_both, jnp.maximum(n0, n1), rest, 0)

    @pl.when(ep == N_EXPERTS // 2 - 1)
    def _():
        @pl.when(pend[0] == 1)
        def _():
            scatter(obuf[0:r, :], ybuf[0:r, :])

        x2 = x1_ref[...] + mod_ref[5:6, :] * acc_scr[...]
        if final:
            ms = jnp.mean(x2 * x2, axis=-1, keepdims=True)
            x2 = x2 * lax.rsqrt(ms + RMS_EPS) * fg_ref[...]
        o_ref[...] = x2


def _moe(cnt, h, rank_t, w_t, w_gate, w_up, w_down, layer, x1, mod_l, final_g, final, tm, tiles_per_batch):
    n, d = h.shape
    de = w_gate.shape[-1]
    nt = n // tm
    grid_spec = pltpu.PrefetchScalarGridSpec(
        num_scalar_prefetch=1,
        grid=(nt, N_EXPERTS // 2),
        in_specs=[
            pl.BlockSpec((tm, d), lambda i, e, c: (i, 0)),
            pl.BlockSpec((2, 1, tm), lambda i, e, c: (e, 0, i)),
            pl.BlockSpec((2, 1, tm), lambda i, e, c: (e, 0, i)),
            pl.BlockSpec((None, 2, d, de), lambda i, e, c: (layer, e, 0, 0)),
            pl.BlockSpec((None, 2, d, de), lambda i, e, c: (layer, e, 0, 0)),
            pl.BlockSpec((None, 2, de, d), lambda i, e, c: (layer, e, 0, 0)),
            pl.BlockSpec((tm, d), lambda i, e, c: (i, 0)),
            pl.BlockSpec((None, 6, d), lambda i, e, c: (i // tiles_per_batch, 0, 0)),
            pl.BlockSpec((1, d), lambda i, e, c: (0, 0)),
        ],
        out_specs=pl.BlockSpec((tm, d), lambda i, e, c: (i, 0)),
        scratch_shapes=[pltpu.VMEM((tm, d), F32), pltpu.VMEM((2 * MOE_R, tm), BF16),
                        pltpu.VMEM((2 * MOE_R, d), BF16), pltpu.SMEM((1,), jnp.int32)],
    )
    return pl.pallas_call(
        functools.partial(_moe_kernel, final=final),
        grid_spec=grid_spec,
        out_shape=jax.ShapeDtypeStruct((n, d), F32),
        compiler_params=_cparams(("arbitrary", "arbitrary")),
        name="moe",
    )(cnt, h, rank_t.reshape(N_EXPERTS, 1, n), w_t.reshape(N_EXPERTS, 1, n), w_gate, w_up, w_down, x1, mod_l,
      final_g)


def _pad_cols(w, n):
    return jnp.pad(w, [(0, 0)] * (w.ndim - 1) + [(0, n - w.shape[-1])])


def _pad_rows(w, n):
    return jnp.pad(w, [(0, 0)] * (w.ndim - 2) + [(0, n - w.shape[-2]), (0, 0)])


def _layout_in_cols(w):
    rw = RWKV_W
    o = 3 * rw
    parts = [w[..., 0:o], _pad_cols(w[..., o:o + 64], LANES), _pad_cols(w[..., o + 64:o + 128], LANES),
             w[..., o + 128:o + 256]]
    o += 256
    parts.append(w[..., o:o + SB_COLS])
    o += SB_COLS
    parts.append(_pad_cols(w[..., o:], GLA_COLS_PAD))
    return jnp.concatenate(parts, axis=-1)


def kernel(x, c, rms_mix_g, rms_ffn_g, w_mod, b_mod, w_in, w_out, rwkv_mu, rwkv_w0, rwkv_w2, rwkv_a0, rwkv_a2, rwkv_g2, rwkv_k_k, rwkv_k_a, rwkv_r_k, rwkv_lnx_w, rwkv_lnx_b, sb_norm_g, gla_gk_up, gla_gk_b, gla_norm_g, w_router, router_bias, w_gate, w_up, w_down, final_g):
    bsz, s, d = x.shape
    depth = w_in.shape[0]
    n = bsz * s

    mod = _modulation(c, w_mod, b_mod).reshape(depth, bsz, 6, d)
    w_in_l = _layout_in_cols(w_in).astype(BF16)
    w_out_b = w_out.astype(BF16)
    mu_l = _layout_in_cols(jnp.pad(rwkv_mu, ((0, 0), (0, w_in.shape[-1] - rwkv_mu.shape[-1]))))[:, :RWKV_COLS_PAD]
    w2_p = _pad_rows(rwkv_w2, LANES)
    a2_p = _pad_rows(rwkv_a2, LANES)
    up_p = _pad_rows(gla_gk_up, LANES)
    gla_ng = jnp.tile(gla_norm_g, (1, GLA_HEADS))
    bd_rwkv = _block_diag_const(2 * LANES, HEAD_DIM, 1.0)
    bd_sb = _block_diag_const(SB_W, HEAD_DIM, 1.0)
    bd_gla = _block_diag_const(GLA_VW, HEAD_DIM, 1.0)
    expand = (jnp.arange(GLA_KW)[:, None] // GLA_KD == jnp.arange(GLA_VW)[None, :] // HEAD_DIM).astype(BF16)
    tm_moe = min(MOE_TM, s)
    tri = (jnp.arange(tm_moe)[:, None] < jnp.arange(tm_moe)[None, :]).astype(BF16)
    w_router_t = w_router.T
    bias_col = router_bias.reshape(N_EXPERTS, 1)
    wg_b, wu_b, wd_b = w_gate.astype(BF16), w_up.astype(BF16), w_down.astype(BF16)
    row = lambda a: a.reshape(1, -1)

    for l in range(depth):
        p_r, p_s, p_g = _inproj(x, mod[l], row(rms_mix_g[l]), w_in_l, l)
        y_r = _rwkv(p_r, row(mu_l[l]), row(rwkv_w0[l]), w2_p[l], row(rwkv_a0[l]), a2_p[l], rwkv_g2[l],
                    row(rwkv_k_k[l]), row(rwkv_k_a[l]), row(rwkv_r_k[l]), row(rwkv_lnx_w[l]),
                    row(rwkv_lnx_b[l]), bd_rwkv)
        y_s = _sb(p_s, row(sb_norm_g[l]), bd_sb)
        y_g = _gla(p_g, up_p[l], row(gla_gk_b[l]), row(gla_ng[l]), bd_gla, expand)
        x1, h2, logits_t = _outproj(y_r, y_s, y_g, x, mod[l], row(rms_ffn_g[l]), w_out_b, l, w_router_t)
        w_t, rank_t, cnt = _route(logits_t, bias_col, tri)
        cnt_i = cnt[:, :, 0].astype(jnp.int32).reshape(-1)
        x = _moe(cnt_i, h2.reshape(n, d), rank_t, w_t, wg_b, wu_b, wd_b, l, x1.reshape(n, d),
                 mod[l], row(final_g), l == depth - 1, tm_moe, s // tm_moe).reshape(bsz, s, d)
    return x
```

```python
import functools

import jax
import jax.numpy as jnp
from jax import lax
from jax.experimental import pallas as pl
from jax.experimental.pallas import tpu as pltpu

F32 = jnp.float32
BF16 = jnp.bfloat16

LANES = 128
HEAD_DIM = 64
RWKV_W = 512
RWKV_PAIRS = RWKV_W // LANES
RWKV_COLS_PAD = 3 * RWKV_W + 3 * LANES
RWKV_CHUNK = 64
RWKV_G = 4
RWKV_GN_EPS = 64e-5
SB_W = 256
SB_COLS = 3 * SB_W
SB_TQ = 512
SB_TK = 128
SB_DEAD = -104.0
GLA_VW = 256
GLA_KW = 128
GLA_HEADS = 4
GLA_KD = 32
GLA_LORA = 16
GLA_COLS_PAD = 2 * GLA_KW + 2 * GLA_VW + LANES
GLA_CHUNK = 64
GLA_SUB = 16
GLA_G = 4
GLA_NORMALIZER = 16.0
IN_COLS_PAD = RWKV_COLS_PAD + SB_COLS + GLA_COLS_PAD
HEAD_NORM_EPS = 1e-5
RMS_EPS = 1e-6
N_EXPERTS = 16
EXPERTS_PER_GROUP = 4
N_GROUPS = 4
OUTPROJ_SPLIT = 4
MOE_TM = 1024
MOE_R = 128
VMEM_LIMIT = 48 * 1024 * 1024


def _dot(a, b):
    return jnp.dot(a, b, preferred_element_type=F32)


def _dot_nt(a, b):
    return lax.dot_general(a, b, (((1,), (1,)), ((), ())), preferred_element_type=F32)


def _split2(x):
    hi = x.astype(BF16)
    lo = (x - hi.astype(F32)).astype(BF16)
    return hi, lo


def _split3(x):
    hi = x.astype(BF16)
    r = x - hi.astype(F32)
    mid = r.astype(BF16)
    lo = (r - mid.astype(F32)).astype(BF16)
    return hi, mid, lo


def _dot_seg(x, m):
    hi, lo = _split2(x)
    wm = m.shape[0]
    cols = [_dot(hi[:, c:c + wm], m) + _dot(lo[:, c:c + wm], m) for c in range(0, x.shape[1], wm)]
    return cols[0] if len(cols) == 1 else jnp.concatenate(cols, axis=1)


def _dot_exact_lhs(m, x):
    hi, mid, lo = _split3(x)
    return _dot(m, hi) + _dot(m, mid) + _dot(m, lo)


def _dot_f32(a, b):
    ah, al = _split2(a)
    bh, bl = _split2(b)
    return _dot(ah, bh) + _dot(ah, bl) + _dot(al, bh)


def _dot_nt_f32(a, b):
    ah, al = _split2(a)
    bh, bl = _split2(b)
    m = a.shape[0]
    p = _dot_nt(jnp.concatenate([ah, al], axis=0), bh)
    return p[0:m] + _dot_nt(ah, bl) + p[m:2 * m]


def _softplus(x):
    return jnp.maximum(x, 0.0) + jnp.log(1.0 + jnp.exp(-jnp.abs(x)))


def _log_sigmoid(x):
    return jnp.minimum(x, 0.0) - jnp.log(1.0 + jnp.exp(-jnp.abs(x)))


def _sigmoid(x):
    return 1.0 / (1.0 + jnp.exp(-x))


def _iota(shape, dim):
    return lax.broadcasted_iota(jnp.int32, shape, dim)


def _block_diag_const(n, blk, val):
    i = jnp.arange(n)
    return jnp.where((i[:, None] // blk) == (i[None, :] // blk), val, 0.0).astype(BF16)


def _cparams(sem):
    return pltpu.CompilerParams(dimension_semantics=sem, vmem_limit_bytes=VMEM_LIMIT)


def _mod_kernel(c_ref, w_ref, b_ref, o_ref):
    c = c_ref[...]
    ca = c * _sigmoid(c)
    o_ref[...] = _dot_f32(ca, w_ref[...]) + b_ref[...]


def _modulation(c, w_mod, b_mod):
    depth, d, six_d = w_mod.shape
    bsz = c.shape[0]
    tn = 1536
    return pl.pallas_call(
        _mod_kernel,
        grid=(depth, six_d // tn),
        in_specs=[
            pl.BlockSpec((bsz, d), lambda l, j: (0, 0)),
            pl.BlockSpec((None, d, tn), lambda l, j: (l, 0, j)),
            pl.BlockSpec((None, 1, tn), lambda l, j: (l, 0, j)),
        ],
        out_specs=pl.BlockSpec((None, bsz, tn), lambda l, j: (l, 0, j)),
        out_shape=jax.ShapeDtypeStruct((depth, bsz, six_d), F32),
        compiler_params=_cparams(("arbitrary", "arbitrary")),
        name="adaln_mod",
    )(c, w_mod, b_mod.reshape(depth, 1, six_d))


def _rms_mod(x, g, scale, shift):
    ms = jnp.mean(x * x, axis=-1, keepdims=True)
    return x * lax.rsqrt(ms + RMS_EPS) * g * (1.0 + scale) + shift


def _inproj_kernel(x_ref, mod_ref, g_ref, w_ref, pr_ref, ps_ref, pg_ref):
    mod = mod_ref[...]
    h = _rms_mod(x_ref[...], g_ref[...], mod[1:2], mod[0:1]).astype(BF16)
    o1 = RWKV_COLS_PAD
    o2 = o1 + SB_COLS
    pr_ref[...] = _dot(h, w_ref[:, 0:o1])
    ps_ref[...] = _dot(h, w_ref[:, o1:o2]).astype(BF16)
    pg_ref[...] = _dot(h, w_ref[:, o2:IN_COLS_PAD])


def _inproj(x, mod_l, g, w, layer):
    bsz, s, d = x.shape
    tm = min(512, s)
    return pl.pallas_call(
        _inproj_kernel,
        grid=(bsz, s // tm),
        in_specs=[
            pl.BlockSpec((None, tm, d), lambda b, i: (b, i, 0)),
            pl.BlockSpec((None, 6, d), lambda b, i: (b, 0, 0)),
            pl.BlockSpec((1, d), lambda b, i: (0, 0)),
            pl.BlockSpec((None, d, IN_COLS_PAD), lambda b, i: (layer, 0, 0)),
        ],
        out_specs=[
            pl.BlockSpec((None, tm, RWKV_COLS_PAD), lambda b, i: (b, i, 0)),
            pl.BlockSpec((None, tm, SB_COLS), lambda b, i: (b, i, 0)),
            pl.BlockSpec((None, tm, GLA_COLS_PAD), lambda b, i: (b, i, 0)),
        ],
        out_shape=[
            jax.ShapeDtypeStruct((bsz, s, RWKV_COLS_PAD), F32),
            jax.ShapeDtypeStruct((bsz, s, SB_COLS), BF16),
            jax.ShapeDtypeStruct((bsz, s, GLA_COLS_PAD), F32),
        ],
        compiler_params=_cparams(("arbitrary", "arbitrary")),
        name="inproj",
    )(x, mod_l, g, w)


def _rwkv_kernel(p_ref, mu_ref, w0_ref, w2_ref, a0_ref, a2_ref, g2_ref, kk_ref, ka_ref, rk_ref,
                 lnw_ref, lnb_ref, bd_ref, o_ref, ht_scr, prev_scr, *, ng):
    t = RWKV_CHUNK
    rows = ng * t

    @pl.when(pl.program_id(1) == 0)
    def _():
        ht_scr[...] = jnp.zeros_like(ht_scr)
        prev_scr[...] = jnp.zeros_like(prev_scr)

    p = p_ref[...].reshape(rows, RWKV_COLS_PAD)
    row = _iota((rows, 1), 0)
    prev = pltpu.roll(p, 1, axis=0)
    for gi in range(ng):
        prev = jnp.where(row == gi * t, prev_scr[gi, 0:1, :], prev)
        prev_scr[gi, 0:1, :] = p[gi * t + t - 1:gi * t + t, :]
    xm = p + (prev - p) * mu_ref[...]
    w = RWKV_W
    r = xm[:, 0:w]
    k = xm[:, w:2 * w]
    v = xm[:, 2 * w:3 * w]
    xw = xm[:, 3 * w:3 * w + LANES]
    xa = xm[:, 3 * w + LANES:3 * w + 2 * LANES]
    xg = xm[:, 3 * w + 2 * LANES:3 * w + 3 * LANES]
    w_log = -_softplus(-(w0_ref[...] + _dot_f32(jnp.tanh(xw), w2_ref[...]))) - 0.5
    lw = -jnp.exp(w_log)
    iclr = _sigmoid(a0_ref[...] + _dot_f32(xa, a2_ref[...]))
    g = _dot_f32(_sigmoid(xg), g2_ref[...])
    bd = bd_ref[...]
    kkr = k * kk_ref[...]
    ss = _dot_seg(kkr * kkr, bd)
    kk = kkr * lax.rsqrt(jnp.maximum(ss, 1e-24))
    k2 = k * (1.0 + (iclr - 1.0) * ka_ref[...])
    bonus = _dot_seg(r * k2 * rk_ref[...], bd) * v

    ti = _iota((rows, rows), 0)
    tj = _iota((rows, rows), 1)
    tri_incl = jnp.where(((ti >> 6) == (tj >> 6)) & (ti >= tj), 1.0, 0.0).astype(BF16)
    beta = _dot_exact_lhs(tri_incl, lw)
    gam = jnp.exp(beta)
    gam_inv = jnp.exp(-beta)
    a_t = -kk * jnp.exp(beta - lw)
    r_t = r * gam
    b_t = kk * iclr * gam_inv
    k_t = k2 * gam_inv

    lane = _iota((1, LANES), 1)
    m0 = jnp.where(lane < HEAD_DIM, 1.0, 0.0)
    m1 = 1.0 - m0
    n2 = 2 * t
    ii = _iota((n2, n2), 0)
    jj = _iota((n2, n2), 1)
    it = ii & (t - 1)
    jt = jj & (t - 1)
    same64 = (ii >> 6) == (jj >> 6)
    strict = same64 & (it > jt)
    incl = same64 & (it >= jt)
    blk16 = (ii >> 4) == (jj >> 4)
    blk32 = (ii >> 5) == (jj >> 5)
    not16 = jnp.logical_not(blk16)
    not32 = jnp.logical_not(blk32)
    eye = jnp.where(ii == jj, 1.0, 0.0)

    units = [(gi, pr) for gi in range(ng) for pr in range(RWKV_PAIRS)]
    U = range(len(units))

    def cut(arr, gi, pr):
        return arr[gi * t:(gi + 1) * t, pr * LANES:(pr + 1) * LANES]

    def stack(xa_, xb_, gi, pr):
        ca, cb = cut(xa_, gi, pr), cut(xb_, gi, pr)
        return jnp.concatenate([ca * m0, ca * m1, cb * m0, cb * m1], axis=0).astype(BF16)

    lhs = [stack(a_t, r_t, gi, pr) for gi, pr in units]
    rhs = [stack(b_t, k_t, gi, pr) for gi, pr in units]
    vst = [jnp.concatenate([cut(v, gi, pr) * m0, cut(v, gi, pr) * m1], axis=0) for gi, pr in units]
    vstb = [u.astype(BF16) for u in vst]
    gb = [_dot_nt(lhs[i], rhs[i]) for i in U]
    hts = [ht_scr[i] for i in U]
    p0 = [_dot_nt(lhs[i], hts[i].astype(BF16)) for i in U]
    a_ab = [jnp.where(strict, g_[0:n2, 0:n2], 0.0) for g_ in gb]
    a_ak = [jnp.where(strict, g_[0:n2, n2:2 * n2], 0.0).astype(BF16) for g_ in gb]
    a_r = [jnp.concatenate([jnp.where(incl, g_[n2:2 * n2, 0:n2], 0.0),
                            jnp.where(incl, g_[n2:2 * n2, n2:2 * n2], 0.0)], axis=1).astype(BF16) for g_ in gb]
    d1 = [jnp.where(blk16, a, 0.0) for a in a_ab]
    d1b = [d.astype(BF16) for d in d1]
    x = [eye + d for d in d1]
    d2b = [_dot(d, d).astype(BF16) for d in d1b]
    x = [x[i] + _dot(x[i].astype(BF16), d2b[i]) for i in U]
    d4b = [_dot(d, d).astype(BF16) for d in d2b]
    x = [x[i] + _dot(x[i].astype(BF16), d4b[i]) for i in U]
    d8b = [_dot(d, d).astype(BF16) for d in d4b]
    x = [x[i] + _dot(x[i].astype(BF16), d8b[i]) for i in U]
    e32 = [jnp.where(blk32 & not16, a, 0.0).astype(BF16) for a in a_ab]
    xb = [u.astype(BF16) for u in x]
    t1 = [_dot(e32[i], xb[i]).astype(BF16) for i in U]
    x = [x[i] + _dot(xb[i], t1[i]) for i in U]
    e64 = [jnp.where(not32, a, 0.0).astype(BF16) for a in a_ab]
    xb = [u.astype(BF16) for u in x]
    t2 = [_dot(e64[i], xb[i]).astype(BF16) for i in U]
    x = [x[i] + _dot(xb[i], t2[i]) for i in U]
    rhs_u = [p0[i][0:n2] + _dot(a_ak[i], vstb[i]) for i in U]
    ust = [_dot(x[i].astype(BF16), rhs_u[i].astype(BF16)) for i in U]
    uv = [jnp.concatenate([ust[i], vst[i]], axis=0) for i in U]
    yst = [p0[i][n2:2 * n2] + _dot(a_r[i], uv[i].astype(BF16)) for i in U]
    ys = [u[0:t] + u[t:n2] for u in yst]
    upd = [_dot(uv[i].T.astype(BF16), rhs[i]) for i in U]
    for i, (gi, pr) in enumerate(units):
        last = gi * t + t - 1
        ht_scr[i] = (hts[i] + upd[i]) * gam[last:last + 1, pr * LANES:(pr + 1) * LANES]
    y = jnp.concatenate([jnp.concatenate(ys[gi * RWKV_PAIRS:(gi + 1) * RWKV_PAIRS], axis=1)
                         for gi in range(ng)], axis=0)
    inv_n = 1.0 / HEAD_DIM
    mean = _dot_seg(y, bd) * inv_n
    yc = y - mean
    var = _dot_seg(yc * yc, bd) * inv_n
    yn = yc * lax.rsqrt(var + RWKV_GN_EPS) * lnw_ref[...] + lnb_ref[...]
    o_ref[...] = ((yn + bonus) * g).astype(o_ref.dtype).reshape(ng, t, RWKV_W)


def _rwkv(p_r, mu, w0, w2, a0, a2, g2, k_k, k_a, r_k, lnw, lnb, bd):
    bsz, s, _ = p_r.shape
    t = RWKV_CHUNK
    ng = RWKV_G if bsz % RWKV_G == 0 else 1
    vec = lambda n: pl.BlockSpec((1, n), lambda b, i: (0, 0))
    mat = lambda m, n: pl.BlockSpec((m, n), lambda b, i: (0, 0))
    return pl.pallas_call(
        functools.partial(_rwkv_kernel, ng=ng),
        grid=(bsz // ng, s // t),
        in_specs=[
            pl.BlockSpec((ng, t, RWKV_COLS_PAD), lambda b, i: (b, i, 0)),
            vec(RWKV_COLS_PAD), vec(RWKV_W), mat(LANES, RWKV_W), vec(RWKV_W), mat(LANES, RWKV_W),
            mat(LANES, RWKV_W), vec(RWKV_W), vec(RWKV_W), vec(RWKV_W), vec(RWKV_W), vec(RWKV_W),
            mat(2 * LANES, 2 * LANES),
        ],
        out_specs=pl.BlockSpec((ng, t, RWKV_W), lambda b, i: (b, i, 0)),
        out_shape=jax.ShapeDtypeStruct((bsz, s, RWKV_W), BF16),
        scratch_shapes=[pltpu.VMEM((ng * RWKV_PAIRS, LANES, LANES), F32),
                        pltpu.VMEM((ng, 8, RWKV_COLS_PAD), F32)],
        compiler_params=_cparams(("arbitrary", "arbitrary")),
        name="rwkv7",
    )(p_r, mu, w0, w2, a0, a2, g2, k_k, k_a, r_k, lnw, lnb, bd)


def _sb_kernel(q_ref, k_ref, v_ref, g_ref, bd_ref, sfx_ref, o_ref, acc_scr, carry_scr, *, tq):
    tk = SB_TK
    nsub = tq // tk
    nh = SB_W // HEAD_DIM
    heads = range(nh)
    qi = pl.program_id(1)
    lane = _iota((1, SB_W), 1)
    hm = [jnp.where((lane >> 6) == h, 1.0, 0.0).astype(BF16) for h in heads]
    q = q_ref[...] * jnp.asarray(HEAD_DIM ** -0.5, BF16)
    qh = [q * hm[h] for h in heads]
    suffix = sfx_ref[...]
    acc_scr[...] = jnp.zeros_like(acc_scr)
    carry_scr[...] = jnp.zeros_like(carry_scr)

    def block(j, r0, diagonal):
        off = pl.multiple_of(j * tk, tk)
        kb = k_ref[pl.ds(off, tk), :]
        vb = v_ref[pl.ds(off, tk), :]
        vcat = jnp.concatenate([vb * hm[h] for h in heads], axis=0)
        z = [_dot_nt(qh[h][r0:tq], kb) for h in heads]
        l1p = [jnp.log(1.0 + jnp.exp(-jnp.abs(u))) for u in z]
        lk = [-(jnp.maximum(z[h], 0.0) + l1p[h]) for h in heads]
        lsig = [jnp.minimum(z[h], 0.0) - l1p[h] for h in heads]
        if diagonal:
            rows = tq - r0
            causal = _iota((rows, tk), 1) < _iota((rows, tk), 0)
            lk = [jnp.where(causal, u, 0.0) for u in lk]
        cs = [_dot(lk[h].astype(BF16), suffix) for h in heads]
        carry = [carry_scr[h, r0:tq, :] for h in heads]
        wgt = [jnp.exp(lsig[h] + cs[h][:, 0:tk] + carry[h]) for h in heads]
        if diagonal:
            wgt = [jnp.where(causal, u, 0.0) for u in wgt]
        for h in heads:
            carry_scr[h, r0:tq, :] = carry[h] + cs[h][:, tk:2 * tk]
        wcat = jnp.concatenate([u.astype(BF16) for u in wgt], axis=1)
        acc_scr[r0:tq, :] += _dot(wcat, vcat)

    for jd in reversed(range(nsub)):
        block(qi * nsub + jd, jd * tk, True)

    def alive():
        m = carry_scr[0]
        for h in range(1, nh):
            m = jnp.maximum(m, carry_scr[h])
        return (jnp.max(m) > SB_DEAD).astype(jnp.int32)

    def cond(c):
        return (c[0] < qi * nsub) & (c[1] > 0)

    def body(c):
        block(qi * nsub - 1 - c[0], 0, False)
        return c[0] + 1, alive()

    lax.while_loop(cond, body, (jnp.int32(0), alive()))
    o = acc_scr[...]
    ms = _dot_seg(o * o, bd_ref[...]) * (1.0 / HEAD_DIM)
    o_ref[...] = (o * lax.rsqrt(ms + HEAD_NORM_EPS) * g_ref[...]).astype(o_ref.dtype)


def _sb(p_s, norm_g, bd):
    bsz, s, _ = p_s.shape
    tq = min(SB_TQ, s)
    si = jnp.arange(SB_TK)[:, None]
    sj = jnp.arange(2 * SB_TK)[None, :]
    sfx = ((si > sj) | (sj >= SB_TK)).astype(BF16)
    return pl.pallas_call(
        functools.partial(_sb_kernel, tq=tq),
        grid=(bsz, s // tq),
        in_specs=[
            pl.BlockSpec((None, tq, SB_W), lambda b, i: (b, i, 0)),
            pl.BlockSpec((None, s, SB_W), lambda b, i: (b, 0, 1)),
            pl.BlockSpec((None, s, SB_W), lambda b, i: (b, 0, 2)),
            pl.BlockSpec((1, SB_W), lambda b, i: (0, 0)),
            pl.BlockSpec((SB_W, SB_W), lambda b, i: (0, 0)),
            pl.BlockSpec((SB_TK, 2 * SB_TK), lambda b, i: (0, 0)),
        ],
        out_specs=pl.BlockSpec((None, tq, SB_W), lambda b, i: (b, i, 0)),
        out_shape=jax.ShapeDtypeStruct((bsz, s, SB_W), BF16),
        scratch_shapes=[pltpu.VMEM((tq, SB_W), F32), pltpu.VMEM((SB_W // HEAD_DIM, tq, LANES), F32)],
        compiler_params=_cparams(("arbitrary", "arbitrary")),
        name="stickbreak",
    )(p_s, p_s, p_s, norm_g, bd, sfx)


def _gla_kernel(p_ref, up_ref, gkb_ref, ng_ref, bdv_ref, exp_ref, tri_ref, var_ref, smask_ref, o_ref, ht_scr, *, ns):
    t = GLA_CHUNK
    kw, vw = GLA_KW, GLA_VW
    nh = GLA_HEADS
    rows = ns * t
    nsub = t // GLA_SUB
    seqs = range(ns)

    @pl.when(pl.program_id(1) == 0)
    def _():
        ht_scr[...] = jnp.zeros_like(ht_scr)

    p = p_ref[...].reshape(rows, GLA_COLS_PAD)
    q = p[:, 0:kw] * (GLA_KD ** -0.5)
    k = p[:, kw:2 * kw]
    v = p[:, 2 * kw:2 * kw + vw]
    g = p[:, 2 * kw + vw:2 * kw + 2 * vw]
    gk_low = p[:, 2 * kw + 2 * vw:2 * kw + 2 * vw + LANES]
    log_a = _log_sigmoid(_dot_f32(gk_low, up_ref[...]) + gkb_ref[...]) * (1.0 / GLA_NORMALIZER)
    beta = _dot_exact_lhs(tri_ref[...], log_a)
    row = _iota((rows, 1), 0)
    seq = row >> 6
    sub = (row & (t - 1)) >> 4

    def srows(a, si):
        return a[si * t:(si + 1) * t]

    beta_last = [beta[si * t + t - 1:si * t + t, :] for si in seqs]
    hts = [ht_scr[si] for si in seqs]
    q_exp = (q * jnp.exp(beta)).astype(BF16)
    o_inter = [_dot_nt(srows(q_exp, si), hts[si].astype(BF16)) for si in seqs]

    ref_rows = [[beta[si * t + GLA_SUB * i - 1:si * t + GLA_SUB * i, :] for i in range(1, nsub)] for si in seqs]
    beta_ref = jnp.zeros_like(beta)
    for si in seqs:
        for i in range(1, nsub):
            beta_ref = jnp.where((seq == si) & (sub == i), ref_rows[si][i - 1], beta_ref)
    q_hat = q * jnp.exp(jnp.minimum(beta - beta_ref, 0.0))
    lane_k = _iota((1, kw), 1)
    lane_v = _iota((1, vw), 1)
    mk = [jnp.where((lane_k >> 5) == h, 1.0, 0.0) for h in range(nh)]
    mv = [jnp.where((lane_v >> 6) == h, 1.0, 0.0) for h in range(nh)]
    q_st = [jnp.concatenate([srows(q_hat, si) * mk[h] for h in range(nh)], axis=0).astype(BF16) for si in seqs]
    v_st = [jnp.concatenate([srows(v, si) * mv[h] for h in range(nh)], axis=0).astype(BF16) for si in seqs]
    n4 = nh * t
    variant = var_ref[...]
    attn = [jnp.zeros((n4, n4), F32) for _ in seqs]
    for i in range(1, nsub):
        k_hat = [srows(k, si) * jnp.exp(jnp.minimum(ref_rows[si][i - 1] - srows(beta, si), 0.0)) for si in seqs]
        k_st = [jnp.concatenate([k_hat[si] * mk[h] for h in range(nh)], axis=0).astype(BF16) for si in seqs]
        gi = [_dot_nt(q_st[si], k_st[si]) for si in seqs]
        attn = [jnp.where(variant == i, gi[si], attn[si]) for si in seqs]
    o_st = [_dot(attn[si].astype(BF16), v_st[si]) for si in seqs]
    o_seq = []
    for si in seqs:
        acc = o_inter[si]
        for h in range(nh):
            acc = acc + o_st[si][h * t:(h + 1) * t]
        o_seq.append(acc)
    o = jnp.concatenate(o_seq, axis=0)

    expand = exp_ref[...]
    tsub = row & (GLA_SUB - 1)
    for d in range(GLA_SUB):
        if d == 0:
            kd, bd_, vd = k, beta, v
        else:
            kd = pltpu.roll(k, d, axis=0)
            bd_ = pltpu.roll(beta, d, axis=0)
            vd = pltpu.roll(v, d, axis=0)
        term = jnp.where(tsub >= d, q * kd * jnp.exp(jnp.minimum(beta - bd_, 0.0)), 0.0)
        o = o + _dot(term.astype(BF16), expand) * vd

    smask = smask_ref[...]
    k_end = [(srows(k, si) * jnp.exp(beta_last[si] - srows(beta, si))).astype(BF16) for si in seqs]
    upd = [_dot(srows(v, si).T.astype(BF16), k_end[si]) for si in seqs]
    for si in seqs:
        ht_scr[si] = hts[si] * jnp.exp(beta_last[si]) + upd[si] * smask

    ms = _dot_seg(o * o, bdv_ref[...]) * (1.0 / HEAD_DIM)
    on = o * lax.rsqrt(ms + HEAD_NORM_EPS) * ng_ref[...]
    o_ref[...] = (on * (g * _sigmoid(g))).astype(o_ref.dtype).reshape(ns, t, vw)


def _gla(p_g, gk_up, gk_b, norm_g, bdv, expand):
    bsz, s, _ = p_g.shape
    t = GLA_CHUNK
    ns = GLA_G if bsz % GLA_G == 0 else 1
    const = lambda m, n: pl.BlockSpec((m, n), lambda b, i: (0, 0))
    rows = ns * t
    n4 = GLA_HEADS * t
    ri = jnp.arange(rows)
    tri = (((ri[:, None] // t) == (ri[None, :] // t)) & (ri[:, None] >= ri[None, :])).astype(BF16)
    r4 = jnp.arange(n4)
    rsub = (r4 % t) // GLA_SUB
    variant = jnp.where((r4[None, :] % t) < GLA_SUB * rsub[:, None], rsub[:, None], 0).astype(jnp.int32)
    smask = ((jnp.arange(GLA_VW)[:, None] // HEAD_DIM) == (jnp.arange(GLA_KW)[None, :] // GLA_KD)).astype(F32)
    return pl.pallas_call(
        functools.partial(_gla_kernel, ns=ns),
        grid=(bsz // ns, s // t),
        in_specs=[
            pl.BlockSpec((ns, t, GLA_COLS_PAD), lambda b, i: (b, i, 0)),
            const(LANES, GLA_KW), const(1, GLA_KW), const(1, GLA_VW), const(GLA_VW, GLA_VW),
            const(GLA_KW, GLA_VW), const(rows, rows), const(n4, n4), const(GLA_VW, GLA_KW),
        ],
        out_specs=pl.BlockSpec((ns, t, GLA_VW), lambda b, i: (b, i, 0)),
        out_shape=jax.ShapeDtypeStruct((bsz, s, GLA_VW), BF16),
        scratch_shapes=[pltpu.VMEM((ns, GLA_VW, GLA_KW), F32)],
        compiler_params=_cparams(("arbitrary", "arbitrary")),
        name="gla",
    )(p_g, gk_up, gk_b, norm_g, bdv, expand, tri, variant, smask)


def _outproj_kernel(yr_ref, ys_ref, yg_ref, x_ref, mod_ref, g_ref, w_ref, wr_ref, x1_ref, h_ref, lg_ref):
    mod = mod_ref[...]
    o1 = RWKV_W
    o2 = o1 + SB_W
    tm = x_ref.shape[0]
    nq = OUTPROJ_SPLIT if tm % (OUTPROJ_SPLIT * LANES) == 0 else 1
    rq = tm // nq
    parts = [slice(qi * rq, (qi + 1) * rq) for qi in range(nq)]
    mix = [_dot(yr_ref[sl, :], w_ref[0:o1, :]) + _dot(ys_ref[sl, :], w_ref[o1:o2, :])
           + _dot(yg_ref[sl, :], w_ref[o2:, :]) for sl in parts]
    x1 = [x_ref[sl, :] + mod[2:3] * mix[qi] for qi, sl in enumerate(parts)]
    h = [_rms_mod(u, g_ref[...], mod[4:5], mod[3:4]) for u in x1]
    for qi, sl in enumerate(parts):
        x1_ref[sl, :] = x1[qi]
        h_ref[sl, :] = h[qi].astype(BF16)
        lg_ref[:, sl] = _dot_nt_f32(wr_ref[...], h[qi])


def _outproj(y_r, y_s, y_g, x, mod_l, g, w_out, layer, w_router_t):
    bsz, s, d = x.shape
    tm = min(512, s)
    nt = s // tm
    return pl.pallas_call(
        _outproj_kernel,
        grid=(bsz, nt),
        in_specs=[
            pl.BlockSpec((None, tm, RWKV_W), lambda b, i: (b, i, 0)),
            pl.BlockSpec((None, tm, SB_W), lambda b, i: (b, i, 0)),
            pl.BlockSpec((None, tm, GLA_VW), lambda b, i: (b, i, 0)),
            pl.BlockSpec((None, tm, d), lambda b, i: (b, i, 0)),
            pl.BlockSpec((None, 6, d), lambda b, i: (b, 0, 0)),
            pl.BlockSpec((1, d), lambda b, i: (0, 0)),
            pl.BlockSpec((None, d, d), lambda b, i: (layer, 0, 0)),
            pl.BlockSpec((N_EXPERTS, d), lambda b, i: (0, 0)),
        ],
        out_specs=[
            pl.BlockSpec((None, tm, d), lambda b, i: (b, i, 0)),
            pl.BlockSpec((None, tm, d), lambda b, i: (b, i, 0)),
            pl.BlockSpec((N_EXPERTS, tm), lambda b, i: (0, b * nt + i)),
        ],
        out_shape=[
            jax.ShapeDtypeStruct((bsz, s, d), F32),
            jax.ShapeDtypeStruct((bsz, s, d), BF16),
            jax.ShapeDtypeStruct((N_EXPERTS, bsz * s), F32),
        ],
        compiler_params=_cparams(("arbitrary", "arbitrary")),
        name="outproj",
    )(y_r, y_s, y_g, x, mod_l, g, w_out, w_router_t)


def _route_kernel(lg_ref, bias_ref, tri_ref, w_ref, rank_ref, cnt_ref):
    aff = _sigmoid(lg_ref[...])
    sel = aff + bias_ref[...]
    e = EXPERTS_PER_GROUP
    rows = [sel[i:i + 1, :] for i in range(N_EXPERTS)]
    arow = [aff[i:i + 1, :] for i in range(N_EXPERTS)]
    scores = []
    for gi in range(N_GROUPS):
        a, b, c, d = rows[e * gi:e * gi + e]
        scores.append(jnp.maximum(jnp.maximum(jnp.maximum(a + b, a + c), jnp.maximum(a + d, b + c)),
                                  jnp.maximum(b + d, c + d)))
    grp = jnp.zeros_like(scores[0]).astype(jnp.int32)
    best = scores[0]
    for gi in range(1, N_GROUPS):
        better = scores[gi] > best
        grp = jnp.where(better, gi, grp)
        best = jnp.where(better, scores[gi], best)
    sin, ain = [], []
    for j in range(e):
        sv, av = rows[j], arow[j]
        for gi in range(1, N_GROUPS):
            sv = jnp.where(grp == gi, rows[e * gi + j], sv)
            av = jnp.where(grp == gi, arow[e * gi + j], av)
        sin.append(sv)
        ain.append(av)
    loc1 = jnp.zeros_like(grp)
    b1 = sin[0]
    for j in range(1, e):
        better = sin[j] > b1
        loc1 = jnp.where(better, j, loc1)
        b1 = jnp.where(better, sin[j], b1)
    neg = jnp.full_like(b1, -jnp.inf)
    loc2 = jnp.zeros_like(grp)
    b2 = neg
    for j in range(e):
        cand = jnp.where(loc1 == j, neg, sin[j])
        better = cand > b2
        loc2 = jnp.where(better, j, loc2)
        b2 = jnp.where(better, cand, b2)
    a1 = ain[0]
    a2 = ain[0]
    for j in range(1, e):
        a1 = jnp.where(loc1 == j, ain[j], a1)
        a2 = jnp.where(loc2 == j, ain[j], a2)
    den = a1 + a2
    e1 = grp * e + loc1
    e2 = grp * e + loc2
    eid = _iota(aff.shape, 0)
    is1 = eid == e1
    is2 = eid == e2
    w_ref[...] = jnp.where(is1, a1 / den, jnp.where(is2, a2 / den, 0.0))
    selected = is1 | is2
    self = jnp.where(selected, 1.0, 0.0)
    excl = _dot(self.astype(BF16), tri_ref[...])
    rank_ref[...] = jnp.where(selected, excl, -1.0)
    cnt = jnp.sum(self, axis=1, keepdims=True)
    cnt_ref[...] = jnp.broadcast_to(cnt, cnt_ref.shape)


def _route(logits_t, bias, tri):
    ne, n = logits_t.shape
    tm = tri.shape[0]
    nt = n // tm
    return pl.pallas_call(
        _route_kernel,
        grid=(nt,),
        in_specs=[
            pl.BlockSpec((ne, tm), lambda i: (0, i)),
            pl.BlockSpec((ne, 1), lambda i: (0, 0)),
            pl.BlockSpec((tm, tm), lambda i: (0, 0)),
        ],
        out_specs=[
            pl.BlockSpec((ne, tm), lambda i: (0, i)),
            pl.BlockSpec((ne, tm), lambda i: (0, i)),
            pl.BlockSpec((None, ne, LANES), lambda i: (i, 0, 0)),
        ],
        out_shape=[
            jax.ShapeDtypeStruct((ne, n), F32),
            jax.ShapeDtypeStruct((ne, n), F32),
            jax.ShapeDtypeStruct((nt, ne, LANES), F32),
        ],
        compiler_params=_cparams(("arbitrary",)),
        name="route",
    )(logits_t, bias, tri)


def _moe_kernel(cnt_ref, h_ref, rank_ref, w_ref, wg_ref, wu_ref, wd_ref, x1_ref, mod_ref, fg_ref, o_ref,
                acc_scr, obuf, ybuf, pend, *, final):
    i = pl.program_id(0)
    ep = pl.program_id(1)
    r = MOE_R
    log_r = r.bit_length() - 1

    @pl.when(ep == 0)
    def _():
        acc_scr[...] = jnp.zeros_like(acc_scr)
        pend[0] = 0

    def scatter(onehot, yw):
        acc_scr[...] += lax.dot_general(onehot, yw, (((0,), (0,)), ((), ())), preferred_element_type=F32)

    def select(k, base):
        hit = (rank_ref[k].astype(jnp.int32) - base) == _iota((r, 1), 0)
        wcol = jnp.sum(jnp.where(hit, w_ref[k], 0.0), axis=1, keepdims=True)
        return jnp.where(hit, 1.0, 0.0).astype(BF16), wcol

    def ffn_in(xr, k):
        gate = _dot(xr, wg_ref[k])
        up = _dot(xr, wu_ref[k])
        return (gate * _sigmoid(gate) * up).astype(BF16)

    n0 = (cnt_ref[i * N_EXPERTS + 2 * ep] + (r - 1)) >> log_r
    n1 = (cnt_ref[i * N_EXPERTS + 2 * ep + 1] + (r - 1)) >> log_r
    n_both = jnp.minimum(n0, n1)

    def both(j, c):
        oh0, w0 = select(0, j * r)
        oh1, w1 = select(1, j * r)
        onehot = jnp.concatenate([oh0, oh1], axis=0)
        xr = _dot(onehot, h_ref[...]).astype(BF16)
        a0 = ffn_in(xr[0:r], 0)
        a1 = ffn_in(xr[r:2 * r], 1)
        y0 = _dot(a0, wd_ref[0])
        y1 = _dot(a1, wd_ref[1])
        scatter(onehot, jnp.concatenate([y0 * w0, y1 * w1], axis=0).astype(BF16))
        return c

    lax.fori_loop(0, n_both, both, 0)

    k_rest = jnp.where(n0 > n1, 0, 1)

    def rest(j, c):
        onehot, wcol = select(k_rest, j * r)
        xr = _dot(onehot, h_ref[...]).astype(BF16)
        y = _dot(ffn_in(xr, k_rest), wd_ref[k_rest])
        slot = pend[0]
        off = pl.multiple_of(slot * r, r)
        obuf[pl.ds(off, r), :] = onehot
        ybuf[pl.ds(off, r), :] = (y * wcol).astype(BF16)

        @pl.when(slot == 1)
        def _():
            scatter(obuf[...], ybuf[...])

        pend[0] = 1 - slot
        return c

    lax.fori_loop(n_both, jnp.maximum(n0, n1), rest, 0)

    @pl.when(ep == N_EXPERTS // 2 - 1)
    def _():
        @pl.when(pend[0] == 1)
        def _():
            scatter(obuf[0:r, :], ybuf[0:r, :])

        x2 = x1_ref[...] + mod_ref[5:6, :] * acc_scr[...]
        if final:
            ms = jnp.mean(x2 * x2, axis=-1, keepdims=True)
            x2 = x2 * lax.rsqrt(ms + RMS_EPS) * fg_ref[...]
        o_ref[...] = x2


def _moe(cnt, h, rank_t, w_t, w_gate, w_up, w_down, layer, x1, mod_l, final_g, final, tm, tiles_per_batch):
    n, d = h.shape
    de = w_gate.shape[-1]
    nt = n // tm
    grid_spec = pltpu.PrefetchScalarGridSpec(
        num_scalar_prefetch=1,
        grid=(nt, N_EXPERTS // 2),
        in_specs=[
            pl.BlockSpec((tm, d), lambda i, e, c: (i, 0)),
            pl.BlockSpec((2, 1, tm), lambda i, e, c: (e, 0, i)),
            pl.BlockSpec((2, 1, tm), lambda i, e, c: (e, 0, i)),
            pl.BlockSpec((None, 2, d, de), lambda i, e, c: (layer, e, 0, 0)),
            pl.BlockSpec((None, 2, d, de), lambda i, e, c: (layer, e, 0, 0)),
            pl.BlockSpec((None, 2, de, d), lambda i, e, c: (layer, e, 0, 0)),
            pl.BlockSpec((tm, d), lambda i, e, c: (i, 0)),
            pl.BlockSpec((None, 6, d), lambda i, e, c: (i // tiles_per_batch, 0, 0)),
            pl.BlockSpec((1, d), lambda i, e, c: (0, 0)),
        ],
        out_specs=pl.BlockSpec((tm, d), lambda i, e, c: (i, 0)),
        scratch_shapes=[pltpu.VMEM((tm, d), F32), pltpu.VMEM((2 * MOE_R, tm), BF16),
                        pltpu.VMEM((2 * MOE_R, d), BF16), pltpu.SMEM((1,), jnp.int32)],
    )
    return pl.pallas_call(
        functools.partial(_moe_kernel, final=final),
        grid_spec=grid_spec,
        out_shape=jax.ShapeDtypeStruct((n, d), F32),
        compiler_params=_cparams(("arbitrary", "arbitrary")),
        name="moe",
    )(cnt, h, rank_t.reshape(N_EXPERTS, 1, n), w_t.reshape(N_EXPERTS, 1, n), w_gate, w_up, w_down, x1, mod_l,
      final_g)


def _pad_cols(w, n):
    return jnp.pad(w, [(0, 0)] * (w.ndim - 1) + [(0, n - w.shape[-1])])


def _pad_rows(w, n):
    return jnp.pad(w, [(0, 0)] * (w.ndim - 2) + [(0, n - w.shape[-2]), (0, 0)])


def _layout_in_cols(w):
    rw = RWKV_W
    o = 3 * rw
    parts = [w[..., 0:o], _pad_cols(w[..., o:o + 64], LANES), _pad_cols(w[..., o + 64:o + 128], LANES),
             w[..., o + 128:o + 256]]
    o += 256
    parts.append(w[..., o:o + SB_COLS])
    o += SB_COLS
    parts.append(_pad_cols(w[..., o:], GLA_COLS_PAD))
    return jnp.concatenate(parts, axis=-1)


def kernel(x, c, rms_mix_g, rms_ffn_g, w_mod, b_mod, w_in, w_out, rwkv_mu, rwkv_w0, rwkv_w2, rwkv_a0, rwkv_a2, rwkv_g2, rwkv_k_k, rwkv_k_a, rwkv_r_k, rwkv_lnx_w, rwkv_lnx_b, sb_norm_g, gla_gk_up, gla_gk_b, gla_norm_g, w_router, router_bias, w_gate, w_up, w_down, final_g):
    bsz, s, d = x.shape
    depth = w_in.shape[0]
    n = bsz * s

    mod = _modulation(c, w_mod, b_mod).reshape(depth, bsz, 6, d)
    w_in_l = _layout_in_cols(w_in).astype(BF16)
    w_out_b = w_out.astype(BF16)
    mu_l = _layout_in_cols(jnp.pad(rwkv_mu, ((0, 0), (0, w_in.shape[-1] - rwkv_mu.shape[-1]))))[:, :RWKV_COLS_PAD]
    w2_p = _pad_rows(rwkv_w2, LANES)
    a2_p = _pad_rows(rwkv_a2, LANES)
    up_p = _pad_rows(gla_gk_up, LANES)
    gla_ng = jnp.tile(gla_norm_g, (1, GLA_HEADS))
    bd_rwkv = _block_diag_const(2 * LANES, HEAD_DIM, 1.0)
    bd_sb = _block_diag_const(SB_W, HEAD_DIM, 1.0)
    bd_gla = _block_diag_const(GLA_VW, HEAD_DIM, 1.0)
    expand = (jnp.arange(GLA_KW)[:, None] // GLA_KD == jnp.arange(GLA_VW)[None, :] // HEAD_DIM).astype(BF16)
    tm_moe = min(MOE_TM, s)
    tri = (jnp.arange(tm_moe)[:, None] < jnp.arange(tm_moe)[None, :]).astype(BF16)
    w_router_t = w_router.T
    bias_col = router_bias.reshape(N_EXPERTS, 1)
    wg_b, wu_b, wd_b = w_gate.astype(BF16), w_up.astype(BF16), w_down.astype(BF16)
    row = lambda a: a.reshape(1, -1)

    for l in range(depth):
        p_r, p_s, p_g = _inproj(x, mod[l], row(rms_mix_g[l]), w_in_l, l)
        y_r = _rwkv(p_r, row(mu_l[l]), row(rwkv_w0[l]), w2_p[l], row(rwkv_a0[l]), a2_p[l], rwkv_g2[l],
                    row(rwkv_k_k[l]), row(rwkv_k_a[l]), row(rwkv_r_k[l]), row(rwkv_lnx_w[l]),
                    row(rwkv_lnx_b[l]), bd_rwkv)
        y_s = _sb(p_s, row(sb_norm_g[l]), bd_sb)
        y_g = _gla(p_g, up_p[l], row(gla_gk_b[l]), row(gla_ng[l]), bd_gla, expand)
        x1, h2, logits_t = _outproj(y_r, y_s, y_g, x, mod[l], row(rms_ffn_g[l]), w_out_b, l, w_router_t)
        w_t, rank_t, cnt = _route(logits_t, bias_col, tri)
        cnt_i = cnt[:, :, 0].astype(jnp.int32).reshape(-1)
        x = _moe(cnt_i, h2.reshape(n, d), rank_t, w_t, wg_b, wu_b, wd_b, l, x1.reshape(n, d),
                 mod[l], row(final_g), l == depth - 1, tm_moe, s // tm_moe).reshape(bsz, s, d)
    return x
```
